```python
import math
import jax
import jax.numpy as jnp
from jax import lax
import numpy as np

D_MODEL = 1024
BATCH = 8
SEQ = 4096
DEPTH = 4
DEC_BATCH = 32
DEC_SEQ = 16
PAST_LEN = 2048

CHUNK = 64
Q_BLOCK = 128
HEAD_DIM = 64
D_MIX = D_MODEL
H_RWKV = D_MIX // (4 * HEAD_DIM)
H_FOX = D_MIX // (2 * HEAD_DIM)
H_GDN = D_MIX // (4 * HEAD_DIM)
D_RWKV = H_RWKV * HEAD_DIM
D_FOX = H_FOX * HEAD_DIM
D_GDN = H_GDN * HEAD_DIM
RWKV_DECAY_RANK = 64
RWKV_ICL_RANK = 64
N_SHIFT = 3 * D_RWKV + RWKV_DECAY_RANK + RWKV_ICL_RANK
GDN_CONV = 4
N_CONV = 3 * D_GDN
SPLIT = [N_SHIFT, D_RWKV,
         D_FOX, D_FOX, D_FOX, H_FOX, D_FOX,
         N_CONV, H_GDN, H_GDN, D_GDN]
N_IN = sum(SPLIT)
ALPHA = (2 * DEPTH) ** 0.25
OUT_SCALE = (8 * DEPTH) ** -0.25
LN_EPS = 1e-5
RWKV_GN_EPS = 64e-5
GDN_NORM_EPS = 1e-6
L2_EPS = 1e-6

kernel_name = "hybrid_rwkv7_fox_gdn_stream_step"


def split_cols(t, sizes):
    offs, acc = [], 0
    for s in sizes[:-1]:
        acc += s
        offs.append(acc)
    return jnp.split(t, offs, axis=-1)


def layer_norm(x, g, b):
    xf = x.astype(jnp.float32)
    mu = jnp.mean(xf, -1, keepdims=True)
    var = jnp.mean(jnp.square(xf - mu), -1, keepdims=True)
    return ((xf - mu) * lax.rsqrt(var + LN_EPS) * g.astype(jnp.float32) + b.astype(jnp.float32)).astype(x.dtype)


def l2norm(t):
    return t * lax.rsqrt(jnp.sum(t * t, -1, keepdims=True) + L2_EPS)


def rwkv_mix(p_sh, z, prev_row, s0, lp):
    bsz, t_len, _ = p_sh.shape
    f32 = jnp.float32
    prev = jnp.concatenate([prev_row[:, None, :].astype(p_sh.dtype), p_sh[:, :-1]], axis=1)
    xs = (p_sh + (prev - p_sh) * lp["rwkv_mu"]).astype(f32)
    r, k, v, w_lo, a_lo = split_cols(xs, [D_RWKV, D_RWKV, D_RWKV, RWKV_DECAY_RANK, RWKV_ICL_RANK])
    w_ll = -jax.nn.softplus(-(lp["rwkv_w0"].astype(f32) + jnp.tanh(w_lo) @ lp["rwkv_w2"].astype(f32))) - 0.5
    decay = jnp.exp(-jnp.exp(w_ll))
    a = jax.nn.sigmoid(lp["rwkv_a0"].astype(f32) + a_lo @ lp["rwkv_a2"].astype(f32))
    hd = lambda t: t.reshape(bsz, t_len, H_RWKV, HEAD_DIM)
    r, k, v, decay, a = hd(r), hd(k), hd(v), hd(decay), hd(a)
    kk = k * lp["rwkv_k_k"].astype(f32).reshape(H_RWKV, HEAD_DIM)
    kk = kk / jnp.maximum(jnp.sqrt(jnp.sum(kk * kk, -1, keepdims=True)), 1e-12)
    k = k * (1.0 + (a - 1.0) * lp["rwkv_k_a"].astype(f32).reshape(H_RWKV, HEAD_DIM))

    def step(s, inp):
        r_t, k_t, v_t, w_t, kk_t, a_t = inp
        s_kk = jnp.einsum('bhvk,bhk->bhv', s, kk_t)
        s = (s * w_t[:, :, None, :] - s_kk[..., None] * (kk_t * a_t)[:, :, None, :]
             + v_t[..., None] * k_t[:, :, None, :])
        return s, jnp.einsum('bhvk,bhk->bhv', s, r_t)

    seq = tuple(jnp.moveaxis(t, 1, 0) for t in (r, k, v, decay, kk, a))
    s_fin, o = lax.scan(step, s0.astype(f32), seq)
    o = jnp.moveaxis(o, 0, 1)
    mean = jnp.mean(o, -1, keepdims=True)
    var = jnp.mean(jnp.square(o - mean), -1, keepdims=True)
    o = (((o - mean) * lax.rsqrt(var + RWKV_GN_EPS)).reshape(bsz, t_len, D_RWKV)
         * lp["rwkv_gn_g"].astype(f32) + lp["rwkv_gn_b"].astype(f32))
    bonus = jnp.sum(r * k * lp["rwkv_r_k"].astype(f32), -1, keepdims=True) * v
    o = (o + bonus.reshape(bsz, t_len, D_RWKV)) * jax.nn.silu(z.astype(f32))
    return o.astype(p_sh.dtype), p_sh[:, -1], s_fin


def fox_attend(q, cq, qpos, k, v, ck, kpos):
    f32 = jnp.float32
    s = jnp.einsum('bqhd,bkhd->bhqk', q.astype(f32), k.astype(f32)) * (HEAD_DIM ** -0.5)
    s = s + jnp.transpose(cq, (0, 2, 1))[..., :, None] - jnp.transpose(ck, (0, 2, 1))[..., None, :]
    s = jnp.where(kpos[None, :] <= qpos[:, None], s, -jnp.inf)
    p = jax.nn.softmax(s, axis=-1)
    return jnp.einsum('bhqk,bkhd->bqhd', p, v.astype(f32))


def fox_prompt(q, k, v, logf):
    bsz, t_len, n_h, d = q.shape
    c = jnp.cumsum(logf, axis=1)
    kpos = jnp.arange(t_len)

    def block(i):
        s0 = i * Q_BLOCK
        qb = lax.dynamic_slice_in_dim(q, s0, Q_BLOCK, axis=1)
        cb = lax.dynamic_slice_in_dim(c, s0, Q_BLOCK, axis=1)
        return fox_attend(qb, cb, s0 + jnp.arange(Q_BLOCK), k, v, c, kpos)

    o = lax.map(block, jnp.arange(t_len // Q_BLOCK))
    return jnp.moveaxis(o, 0, 1).reshape(bsz, t_len, n_h, d)


def fox_cached(q, k, v, logf, cache_k, cache_v, cache_logf):
    past, t_len = cache_k.shape[1], q.shape[1]
    k_all = jnp.concatenate([cache_k.astype(k.dtype), k], axis=1)
    v_all = jnp.concatenate([cache_v.astype(v.dtype), v], axis=1)
    c = jnp.cumsum(jnp.concatenate([cache_logf.astype(jnp.float32), logf], axis=1), axis=1)
    return fox_attend(q, c[:, past:], past + jnp.arange(t_len), k_all, v_all, c, jnp.arange(past + t_len))


def gated_delta_chunked(q, k, v, beta, g, s0):
    bsz, t_len, n_h, d = q.shape
    c = min(CHUNK, t_len)
    n = t_len // c

    def chunks(t):
        t = t.reshape((bsz, n, c, n_h) + t.shape[3:])
        return jnp.moveaxis(t, (1, 3), (0, 2))

    q, k, v, beta, g = chunks(q), chunks(k), chunks(v), chunks(beta), chunks(g)
    gc = jnp.cumsum(g, axis=-1)
    incl = jnp.tril(jnp.ones((c, c), bool))
    strict = jnp.tril(jnp.ones((c, c), bool), -1)
    diff = gc[..., :, None] - gc[..., None, :]
    dmat = jnp.where(incl, jnp.exp(jnp.where(incl, diff, 0.0)), 0.0)
    kb = k * beta[..., None]
    m = jnp.where(strict, jnp.einsum('...id,...jd->...ij', kb, k) * dmat, 0.0)
    a_mat = m + jnp.eye(c, dtype=m.dtype)
    rhs = jnp.concatenate([v * beta[..., None], kb * jnp.exp(gc)[..., None]], axis=-1)
    sol = lax.linalg.triangular_solve(a_mat, rhs, left_side=True, lower=True, unit_diagonal=True)
    u, w = sol[..., :d], sol[..., d:]
    qk = jnp.einsum('...id,...jd->...ij', q, k) * dmat
    qg = q * jnp.exp(gc)[..., None]
    kd = k * jnp.exp(gc[..., -1:] - gc)[..., None]
    gl = jnp.exp(gc[..., -1])

    def step(s, xs):
        u_c, w_c, qk_c, qg_c, kd_c, gl_c = xs
        v_new = u_c - jnp.einsum('bhcd,bhde->bhce', w_c, s)
        o = jnp.einsum('bhcd,bhde->bhce', qg_c, s) + jnp.einsum('bhij,bhje->bhie', qk_c, v_new)
        s = s * gl_c[..., None, None] + jnp.einsum('bhcd,bhce->bhde', kd_c, v_new)
        return s, o

    s_fin, o = lax.scan(step, s0, (u, w, qk, qg, kd, gl))
    o = jnp.moveaxis(o, (0, 2), (1, 3)).reshape(bsz, t_len, n_h, d)
    return o, s_fin


def gdn_mix(p_qkv, b_col, a_col, z, conv_prev, s0, lp):
    bsz, t_len, _ = p_qkv.shape
    f32 = jnp.float32
    conv_w = lp["gdn_conv_w"]
    xp = jnp.concatenate([conv_prev.astype(p_qkv.dtype), p_qkv], axis=1)
    y = xp[:, 0:t_len] * conv_w[0]
    for i in range(1, GDN_CONV):
        y = y + xp[:, i:i + t_len] * conv_w[i]
    y = jax.nn.silu(y.astype(f32))
    q, k, v = [t.reshape(bsz, t_len, H_GDN, HEAD_DIM) for t in split_cols(y, [D_GDN, D_GDN, D_GDN])]
    q = l2norm(q) * (HEAD_DIM ** -0.5)
    k = l2norm(k)
    beta = jax.nn.sigmoid(b_col.astype(f32))
    g = -jnp.exp(lp["gdn_a_log"].astype(f32)) * jax.nn.softplus(a_col.astype(f32) + lp["gdn_dt_bias"].astype(f32))
    o, s_fin = gated_delta_chunked(q, k, v, beta, g, s0.astype(f32))
    o = o * lax.rsqrt(jnp.mean(o * o, -1, keepdims=True) + GDN_NORM_EPS) * lp["gdn_norm_g"].astype(f32)
    o = o.reshape(bsz, t_len, D_GDN) * jax.nn.silu(z.astype(f32))
    return o.astype(p_qkv.dtype), xp[:, t_len:], s_fin


def trunk_layer(x, lp, rwkv_prev, rwkv_s0, gdn_conv_prev, gdn_s0, fox_cache):
    bsz, t_len, _ = x.shape
    f32 = jnp.float32
    proj = x @ lp["w_in"]
    a_sh, a_z, b_q, b_k, b_v, b_f, b_z, c_qkv, c_b, c_a, c_z = split_cols(proj, SPLIT)
    o_a, rwkv_last, rwkv_s = rwkv_mix(a_sh, a_z, rwkv_prev, rwkv_s0, lp)
    heads = lambda t: t.reshape(bsz, t_len, H_FOX, HEAD_DIM)
    q, k, v = heads(b_q), heads(b_k), heads(b_v)
    logf = jax.nn.log_sigmoid(b_f.astype(f32) + lp["fox_b_f"].astype(f32))
    if fox_cache is None:
        o_b = fox_prompt(q, k, v, logf)
    else:
        o_b = fox_cached(q, k, v, logf, *fox_cache)
    o_b = (o_b.reshape(bsz, t_len, D_FOX) * jax.nn.silu(b_z.astype(f32))).astype(x.dtype)
    o_c, conv_last, gdn_s = gdn_mix(c_qkv, c_b, c_a, c_z, gdn_conv_prev, gdn_s0, lp)
    h = jnp.concatenate([o_a, o_b, o_c], axis=-1) @ lp["w_out"]
    x_new = layer_norm(ALPHA * x + h, lp["ln_g"], lp["ln_b"])
    return x_new, k, v, logf, rwkv_last, rwkv_s, conv_last, gdn_s


def setup_inputs(seed: int = 0) -> dict:
    key = jax.random.key(seed)
    ks = iter(jax.random.split(key, 40))
    nrm = lambda shape, s=1.0: s * jax.random.normal(next(ks), shape, jnp.float32)
    uni = lambda shape, lo, hi: jax.random.uniform(next(ks), shape, jnp.float32, lo, hi)
    x_prompt = nrm((BATCH, SEQ, D_MODEL))
    x_sample = nrm((DEC_BATCH, DEC_SEQ, D_MODEL))
    cache_fox_k = nrm((DEPTH, DEC_BATCH, PAST_LEN, H_FOX, HEAD_DIM))
    cache_fox_v = nrm((DEPTH, DEC_BATCH, PAST_LEN, H_FOX, HEAD_DIM))
    cache_fox_logf = jax.nn.log_sigmoid(nrm((DEPTH, DEC_BATCH, PAST_LEN, H_FOX)) + 3.0)
    state_rwkv_shift = nrm((DEPTH, DEC_BATCH, N_SHIFT))
    state_rwkv_wkv = nrm((DEPTH, DEC_BATCH, H_RWKV, HEAD_DIM, HEAD_DIM), 0.3)
    state_gdn_conv = nrm((DEPTH, DEC_BATCH, GDN_CONV - 1, N_CONV))
    state_gdn_wkv = nrm((DEPTH, DEC_BATCH, H_GDN, HEAD_DIM, HEAD_DIM), 0.3)
    dt = jnp.exp(uni((DEPTH, H_GDN), math.log(1e-3), math.log(1e-1)))
    return {
        "x_prompt": x_prompt,
        "x_sample": x_sample,
        "cache_fox_k": cache_fox_k,
        "cache_fox_v": cache_fox_v,
        "cache_fox_logf": cache_fox_logf,
        "state_rwkv_shift": state_rwkv_shift,
        "state_rwkv_wkv": state_rwkv_wkv,
        "state_gdn_conv": state_gdn_conv,
        "state_gdn_wkv": state_gdn_wkv,
        "ln_in_g": 1.0 + nrm((D_MODEL,), 0.02),
        "ln_in_b": nrm((D_MODEL,), 0.02),
        "w_in": nrm((DEPTH, D_MODEL, N_IN), D_MODEL ** -0.5),
        "rwkv_mu": uni((DEPTH, N_SHIFT), 0.0, 1.0),
        "rwkv_w0": uni((DEPTH, D_RWKV), -6.0, -1.0),
        "rwkv_w2": nrm((DEPTH, RWKV_DECAY_RANK, D_RWKV), 0.5 * RWKV_DECAY_RANK ** -0.5),
        "rwkv_a0": nrm((DEPTH, D_RWKV), 0.1),
        "rwkv_a2": nrm((DEPTH, RWKV_ICL_RANK, D_RWKV), 0.5 * RWKV_ICL_RANK ** -0.5),
        "rwkv_k_k": 0.85 + nrm((DEPTH, D_RWKV), 0.05),
        "rwkv_k_a": 1.0 + nrm((DEPTH, D_RWKV), 0.05),
        "rwkv_r_k": nrm((DEPTH, H_RWKV, HEAD_DIM), 0.1),
        "rwkv_gn_g": 1.0 + nrm((DEPTH, D_RWKV), 0.02),
        "rwkv_gn_b": nrm((DEPTH, D_RWKV), 0.02),
        "fox_b_f": uni((DEPTH, H_FOX), 1.0, 5.0),
        "gdn_conv_w": nrm((DEPTH, GDN_CONV, N_CONV), GDN_CONV ** -0.5),
        "gdn_a_log": jnp.log(uni((DEPTH, H_GDN), 1.0, 16.0)),
        "gdn_dt_bias": dt + jnp.log(-jnp.expm1(-dt)),
        "gdn_norm_g": 1.0 + nrm((DEPTH, HEAD_DIM), 0.02),
        "w_out": nrm((DEPTH, D_MIX, D_MODEL), OUT_SCALE * D_MIX ** -0.5),
        "ln_post_g": 1.0 + nrm((DEPTH, D_MODEL), 0.02),
        "ln_post_b": nrm((DEPTH, D_MODEL), 0.02),
    }


def reference(x_prompt, x_sample, cache_fox_k, cache_fox_v, cache_fox_logf, state_rwkv_shift, state_rwkv_wkv,
              state_gdn_conv, state_gdn_wkv, ln_in_g, ln_in_b, w_in, rwkv_mu, rwkv_w0, rwkv_w2, rwkv_a0, rwkv_a2,
              rwkv_k_k, rwkv_k_a, rwkv_r_k, rwkv_gn_g, rwkv_gn_b, fox_b_f, gdn_conv_w, gdn_a_log, gdn_dt_bias,
              gdn_norm_g, w_out, ln_post_g, ln_post_b):
    xp = layer_norm(x_prompt, ln_in_g, ln_in_b)
    xs = layer_norm(x_sample, ln_in_g, ln_in_b)
    bp = xp.shape[0]
    fk_p, fv_p, fl_p, sh_p, rw_p, cv_p, gd_p = [], [], [], [], [], [], []
    fk_s, fv_s, fl_s, sh_s, rw_s, cv_s, gd_s = [], [], [], [], [], [], []
    for l in range(DEPTH):
        lp = dict(w_in=w_in[l], rwkv_mu=rwkv_mu[l], rwkv_w0=rwkv_w0[l], rwkv_w2=rwkv_w2[l], rwkv_a0=rwkv_a0[l],
                  rwkv_a2=rwkv_a2[l], rwkv_k_k=rwkv_k_k[l], rwkv_k_a=rwkv_k_a[l], rwkv_r_k=rwkv_r_k[l],
                  rwkv_gn_g=rwkv_gn_g[l], rwkv_gn_b=rwkv_gn_b[l], fox_b_f=fox_b_f[l], gdn_conv_w=gdn_conv_w[l],
                  gdn_a_log=gdn_a_log[l], gdn_dt_bias=gdn_dt_bias[l], gdn_norm_g=gdn_norm_g[l], w_out=w_out[l],
                  ln_g=ln_post_g[l], ln_b=ln_post_b[l])
        xp, k, v, lf, sh, rw, cv, gd = trunk_layer(
            xp, lp,
            jnp.zeros((bp, N_SHIFT), xp.dtype),
            jnp.zeros((bp, H_RWKV, HEAD_DIM, HEAD_DIM), jnp.float32),
            jnp.zeros((bp, GDN_CONV - 1, N_CONV), xp.dtype),
            jnp.zeros((bp, H_GDN, HEAD_DIM, HEAD_DIM), jnp.float32),
            None)
        fk_p.append(k); fv_p.append(v); fl_p.append(lf); sh_p.append(sh); rw_p.append(rw); cv_p.append(cv); gd_p.append(gd)
        xs, k, v, lf, sh, rw, cv, gd = trunk_layer(
            xs, lp, state_rwkv_shift[l], state_rwkv_wkv[l], state_gdn_conv[l], state_gdn_wkv[l],
            (cache_fox_k[l], cache_fox_v[l], cache_fox_logf[l]))
        fk_s.append(k); fv_s.append(v); fl_s.append(lf); sh_s.append(sh); rw_s.append(rw); cv_s.append(cv); gd_s.append(gd)
    return (xp, xs,
            jnp.stack(fk_p), jnp.stack(fv_p), jnp.stack(fl_p), jnp.stack(sh_p), jnp.stack(rw_p),
            jnp.stack(cv_p), jnp.stack(gd_p),
            jnp.stack(fk_s), jnp.stack(fv_s), jnp.stack(fl_s), jnp.stack(sh_s), jnp.stack(rw_s),
            jnp.stack(cv_s), jnp.stack(gd_s))
```

```python
import functools
import math

import jax
import jax.numpy as jnp
import numpy as np
from jax import lax
from jax.experimental import pallas as pl
from jax.experimental.pallas import tpu as pltpu

F32 = jnp.float32
BF16 = jnp.bfloat16

HEAD_DIM = 64
LANES = 128
N_STACK = 256
LN_EPS = 1e-5
RWKV_GN_EPS = 64e-5
GDN_NORM_EPS = 1e-6
L2_EPS = 1e-6
NEG_BIG = -1e30
VMEM_LIMIT = 56 * 1024 * 1024


def _sigmoid(x):
    return 1.0 / (1.0 + jnp.exp(-x))


def _softplus(x):
    return jnp.maximum(x, 0.0) + jnp.log(1.0 + jnp.exp(-jnp.abs(x)))


def _silu(x):
    return x * _sigmoid(x)


def _split3(x):
    hi = x.astype(BF16)
    r1 = x - hi.astype(F32)
    mid = r1.astype(BF16)
    lo = (r1 - mid.astype(F32)).astype(BF16)
    return hi, mid, lo


_NN = (((1,), (0,)), ((), ()))
_NT = (((1,), (1,)), ((), ()))
_TN = (((0,), (0,)), ((), ()))


def _dg(a, b, dims):
    return lax.dot_general(a, b, dims, preferred_element_type=F32)


def _mm(a, b, dims=_NN, passes=1):
    if passes == 1:
        return _dg(a.astype(BF16), b.astype(BF16), dims)
    ah, am, _ = _split3(a)
    bh, bm, _ = _split3(b)
    return _dg(ah, bh, dims) + (_dg(ah, bm, dims) + _dg(am, bh, dims))


def _mm_exact_rhs(a, e, dims=_NN):
    hi, mid, lo = _split3(a)
    return _dg(hi, e, dims) + (_dg(mid, e, dims) + _dg(lo, e, dims))


def _mm_exact_lhs(e, b, dims=_NN):
    hi, mid, lo = _split3(b)
    return _dg(e, hi, dims) + (_dg(e, mid, dims) + _dg(e, lo, dims))


def _iota(shape, dim):
    return lax.broadcasted_iota(jnp.int32, shape, dim)


def _div_pow2(x, n):
    assert n & (n - 1) == 0
    return lax.shift_right_logical(x, jnp.int32(int(math.log2(n))))


def _head_block_ones(n):
    r = _div_pow2(_iota((n, n), 0), HEAD_DIM)
    c = _div_pow2(_iota((n, n), 1), HEAD_DIM)
    return jnp.where(r == c, 1.0, 0.0).astype(BF16)


def _stack_heads(x, gb, c, n_heads):
    width = x.shape[1]
    lane_head = _div_pow2(_iota((1, width), 1), HEAD_DIM)
    parts = []
    for b in range(gb):
        xb = x[b * c:(b + 1) * c, :]
        for h in range(n_heads):
            parts.append(jnp.where(lane_head == h, xb, 0.0))
    return jnp.concatenate(parts, axis=0)


def _unstack_heads(xs, gb, c, n_heads):
    outs = []
    for b in range(gb):
        acc = xs[(b * n_heads) * c:(b * n_heads + 1) * c, :]
        for h in range(1, n_heads):
            acc = acc + xs[(b * n_heads + h) * c:(b * n_heads + h + 1) * c, :]
        outs.append(acc)
    return jnp.concatenate(outs, axis=0) if gb > 1 else outs[0]


def _unit_lower_inverse(p, c, passes):
    n = p.shape[0]
    eye = jnp.where(_iota((n, n), 0) == _iota((n, n), 1), 1.0, 0.0)
    t = eye + p
    pk = p
    for _ in range(int(math.log2(c)) - 1):
        pk = _mm(pk, pk, passes=passes)
        t = t + _mm(t, pk, passes=passes)
    return t


def _chunk_masks(n, c):
    row = _iota((n, n), 0)
    col = _iota((n, n), 1)
    same = _div_pow2(row, c) == _div_pow2(col, c)
    strict = jnp.logical_and(same, row > col)
    incl = jnp.logical_and(same, row >= col)
    return strict, incl


def _cumsum_rows(x, gb, c):
    n = gb * c
    strict, incl = _chunk_masks(n, c)
    del strict
    tri = jnp.where(incl, 1.0, 0.0).astype(BF16)
    return _mm_exact_lhs(tri, x)


def _shift_rows(x, j, prev_rows, gb, c):
    n_prev = prev_rows[0].shape[0]
    out = pltpu.roll(x, j, 0)
    rid = _iota((x.shape[0], 1), 0)
    for b in range(gb):
        for t in range(j):
            src = n_prev + t - j
            out = jnp.where(rid == b * c + t, prev_rows[b][src:src + 1, :], out)
    return out


def _cparams(sem):
    return pltpu.CompilerParams(dimension_semantics=sem, vmem_limit_bytes=VMEM_LIMIT)


def _ln_kernel(x_ref, g_ref, b_ref, o_ref):
    x = x_ref[...]
    mu = jnp.mean(x, axis=-1, keepdims=True)
    xc = x - mu
    var = jnp.mean(xc * xc, axis=-1, keepdims=True)
    o_ref[...] = xc * lax.rsqrt(var + LN_EPS) * g_ref[...] + b_ref[...]


def _layer_norm(x2, g, b, tm):
    n, d = x2.shape
    return pl.pallas_call(
        _ln_kernel,
        grid=(n // tm,),
        in_specs=[pl.BlockSpec((tm, d), lambda i: (i, 0)),
                  pl.BlockSpec((1, d), lambda i: (0, 0)),
                  pl.BlockSpec((1, d), lambda i: (0, 0))],
        out_specs=pl.BlockSpec((tm, d), lambda i: (i, 0)),
        out_shape=jax.ShapeDtypeStruct((n, d), F32),
        compiler_params=_cparams(("parallel",)),
        name="ln_in",
    )(x2, g.reshape(1, d), b.reshape(1, d))


def _proj_kernel(x_ref, w_ref, *o_refs):
    xb = x_ref[...].astype(BF16)
    off = 0
    for o_ref in o_refs:
        wdt = o_ref.shape[-1]
        o_ref[...] = jnp.dot(xb, w_ref[:, off:off + wdt], preferred_element_type=F32)
        off += wdt


def _project(x2, w_all, layer, widths, tm):
    n, d = x2.shape
    n_cols = w_all.shape[-1]
    return pl.pallas_call(
        _proj_kernel,
        grid=(n // tm,),
        in_specs=[pl.BlockSpec((tm, d), lambda i: (i, 0)),
                  pl.BlockSpec((None, d, n_cols), lambda i: (layer, 0, 0))],
        out_specs=[pl.BlockSpec((tm, w), lambda i: (i, 0)) for w in widths],
        out_shape=[jax.ShapeDtypeStruct((n, w), F32) for w in widths],
        compiler_params=_cparams(("parallel",)),
        name="proj_in",
    )(x2, w_all)


def _rwkv_kernel(ash_ref, az_ref, prev0_ref, s0_ref, mu_ref, w0_ref, a0_ref, w2a2_ref, kk_ref, ka_ref,
                 rk_ref, gng_ref, gnb_ref, o_ref, s_ref, prev_scr, *, gb, c, n_heads, passes):
    d = n_heads * HEAD_DIM
    step = pl.program_id(1)

    @pl.when(step == 0)
    def _():
        s_ref[...] = s0_ref[...]
        prev_scr[...] = prev0_ref[...]

    p = ash_ref[...].reshape(gb * c, ash_ref.shape[-1])
    prev_rows = [prev_scr[b] for b in range(gb)]
    prev = _shift_rows(p, 1, prev_rows, gb, c)
    for b in range(gb):
        prev_scr[b] = p[(b + 1) * c - 1:(b + 1) * c, :]
    xs = p + (prev - p) * mu_ref[...]
    r = xs[:, 0:d]
    k = xs[:, d:2 * d]
    v = xs[:, 2 * d:3 * d]
    wa = xs[:, 3 * d:3 * d + LANES]
    lane = _iota((1, LANES), 1)
    wa = jnp.where(lane < HEAD_DIM, jnp.tanh(wa), wa)
    pre = _mm(wa, w2a2_ref[...])
    w_ll = -_softplus(-(w0_ref[...] + pre[:, 0:d])) - 0.5
    lw = -jnp.exp(w_ll)
    a = _sigmoid(a0_ref[...] + pre[:, d:2 * d])

    ones_bd = _head_block_ones(d)
    kk = k * kk_ref[...]
    kk = kk / jnp.maximum(jnp.sqrt(_mm_exact_rhs(kk * kk, ones_bd)), 1e-12)
    kmod = k * (1.0 + (a - 1.0) * ka_ref[...])
    kka = kk * a

    cum = _cumsum_rows(lw, gb, c)
    cum_last = jnp.concatenate(
        [jnp.broadcast_to(cum[(b + 1) * c - 1:(b + 1) * c, :], (c, d)) for b in range(gb)], axis=0)
    inv_p = jnp.exp(-cum)
    to_end = jnp.exp(cum_last - cum)
    b_t = kk * jnp.exp(cum - lw)
    a_t = -kka * inv_p
    k_t = kmod * inv_p
    r_t = r * jnp.exp(cum)

    st = functools.partial(_stack_heads, gb=gb, c=c, n_heads=n_heads)
    bs, rs, as_, ks, vs = st(b_t), st(r_t), st(a_t), st(k_t), st(v)
    n = bs.shape[0]
    g = _mm(jnp.concatenate([bs, rs], axis=0), jnp.concatenate([as_, ks], axis=0), _NT, passes)
    strict, incl = _chunk_masks(n, c)
    m_ab = jnp.where(strict, g[0:n, 0:n], 0.0)
    m_bk = jnp.where(strict, g[0:n, n:2 * n], 0.0)
    m_ra = jnp.where(incl, g[n:2 * n, 0:n], 0.0)
    m_rk = jnp.where(incl, g[n:2 * n, n:2 * n], 0.0)

    hc = n_heads * c
    s_old = [s_ref[b] for b in range(gb)]
    bh = jnp.concatenate([_mm(bs[b * hc:(b + 1) * hc], s_old[b], _NT, passes) for b in range(gb)], axis=0)
    rh = jnp.concatenate([_mm(rs[b * hc:(b + 1) * hc], s_old[b], _NT, passes) for b in range(gb)], axis=0)
    t_inv = _unit_lower_inverse(m_ab, c, passes)
    u = _mm(t_inv, bh + _mm(m_bk, vs, passes=passes), passes=passes)
    o_s = rh + _mm(m_ra, u, passes=passes) + _mm(m_rk, vs, passes=passes)
    o = _unstack_heads(o_s, gb, c, n_heads)

    a_end = st(-kka * to_end)
    k_end = st(kmod * to_end)
    for b in range(gb):
        sl = slice(b * hc, (b + 1) * hc)
        p_end = jnp.exp(cum[(b + 1) * c - 1:(b + 1) * c, :])
        s_ref[b] = (s_old[b] * p_end + _mm(u[sl], a_end[sl], _TN, passes)
                    + _mm(vs[sl], k_end[sl], _TN, passes))

    mean = _mm_exact_rhs(o, ones_bd) * (1.0 / HEAD_DIM)
    oc = o - mean
    var = _mm_exact_rhs(oc * oc, ones_bd) * (1.0 / HEAD_DIM)
    o = oc * lax.rsqrt(var + RWKV_GN_EPS) * gng_ref[...] + gnb_ref[...]
    bonus = _mm_exact_rhs(r * kmod * rk_ref[...], ones_bd) * v
    z = az_ref[...].reshape(gb * c, d)
    o_ref[...] = ((o + bonus) * _silu(z)).reshape(o_ref.shape)


def _rwkv_mix(a_sh, a_z, prev0, s0, lp, gb, c, passes):
    bsz, t_len, n_shift = a_sh.shape
    d = a_z.shape[-1]
    n_heads = d // HEAD_DIM
    kern = functools.partial(_rwkv_kernel, gb=gb, c=c, n_heads=n_heads, passes=passes)
    vec = lambda w: pl.BlockSpec((1, w), lambda b, s: (0, 0))
    return pl.pallas_call(
        kern,
        grid=(bsz // gb, t_len // c),
        in_specs=[pl.BlockSpec((gb, c, n_shift), lambda b, s: (b, s, 0)),
                  pl.BlockSpec((gb, c, d), lambda b, s: (b, s, 0)),
                  pl.BlockSpec((gb, 1, n_shift), lambda b, s: (b, 0, 0)),
                  pl.BlockSpec((gb, d, d), lambda b, s: (b, 0, 0)),
                  vec(n_shift), vec(d), vec(d),
                  pl.BlockSpec((LANES, 2 * d), lambda b, s: (0, 0)),
                  vec(d), vec(d), vec(d), vec(d), vec(d)],
        out_specs=[pl.BlockSpec((gb, c, d), lambda b, s: (b, s, 0)),
                   pl.BlockSpec((gb, d, d), lambda b, s: (b, 0, 0))],
        out_shape=[jax.ShapeDtypeStruct((bsz, t_len, d), F32),
                   jax.ShapeDtypeStruct((bsz, d, d), F32)],
        scratch_shapes=[pltpu.VMEM((gb, 1, n_shift), F32)],
        compiler_params=_cparams(("parallel", "arbitrary")),
        name="rwkv_mix",
    )(a_sh, a_z, prev0, s0, lp["mu"], lp["w0"], lp["a0"], lp["w2a2"], lp["k_k"], lp["k_a"], lp["r_k"],
      lp["gn_g"], lp["gn_b"])


def _gdn_kernel(qkv_ref, small_ref, cz_ref, conv0_ref, s0_ref, convw_ref, alog_ref, dtb_ref, ng_ref,
                eb_ref, ea_ref, o_ref, s_ref, conv_scr, *, gb, c, n_heads, passes):
    d = n_heads * HEAD_DIM
    n_conv = convw_ref.shape[0]
    step = pl.program_id(1)

    @pl.when(step == 0)
    def _():
        s_ref[...] = s0_ref[...]
        conv_scr[...] = conv0_ref[...]

    x = qkv_ref[...].reshape(gb * c, qkv_ref.shape[-1])
    prev_rows = [conv_scr[b] for b in range(gb)]
    y = x * convw_ref[n_conv - 1:n_conv, :]
    for j in range(1, n_conv):
        y = y + _shift_rows(x, j, prev_rows, gb, c) * convw_ref[n_conv - 1 - j:n_conv - j, :]
    for b in range(gb):
        conv_scr[b] = x[(b + 1) * c - (n_conv - 1):(b + 1) * c, :]
    y = _silu(y)
    q = y[:, 0:d]
    k = y[:, d:2 * d]
    v = y[:, 2 * d:3 * d]
    ones_bd = _head_block_ones(d)
    q = q * lax.rsqrt(_mm_exact_rhs(q * q, ones_bd) + L2_EPS) * (HEAD_DIM ** -0.5)
    k = k * lax.rsqrt(_mm_exact_rhs(k * k, ones_bd) + L2_EPS)

    small = small_ref[...].reshape(gb * c, LANES)
    beta = _sigmoid(_mm_exact_rhs(small, eb_ref[...]))
    g = -jnp.exp(alog_ref[...]) * _softplus(_mm_exact_rhs(small, ea_ref[...]) + dtb_ref[...])
    gc = _cumsum_rows(g, gb, c)
    gc_last = jnp.concatenate(
        [jnp.broadcast_to(gc[(b + 1) * c - 1:(b + 1) * c, :], (c, d)) for b in range(gb)], axis=0)
    egc = jnp.exp(gc)

    st = functools.partial(_stack_heads, gb=gb, c=c, n_heads=n_heads)
    kb = k * beta
    kbs, ks, qs = st(kb), st(k), st(q)
    n = kbs.shape[0]
    strict, incl = _chunk_masks(n, c)
    gcol = jnp.min(st(gc), axis=1, keepdims=True)
    eye = _iota((n, n), 0) == _iota((n, n), 1)
    grow = jnp.sum(jnp.where(eye, jnp.broadcast_to(gcol, (n, n)), 0.0), axis=0, keepdims=True)
    dmat = jnp.exp(jnp.where(incl, gcol - grow, 0.0))
    g2 = _mm(jnp.concatenate([kbs, qs], axis=0), ks, _NT, passes)
    m = jnp.where(strict, g2[0:n] * dmat, 0.0)
    qk = jnp.where(incl, g2[n:2 * n] * dmat, 0.0)
    t_inv = _unit_lower_inverse(-m, c, passes)
    rhs = jnp.concatenate([st(v * beta), st(kb * egc)], axis=1)
    sol = _mm(t_inv, rhs, passes=passes)
    u = sol[:, 0:d]
    w = sol[:, d:2 * d]

    hc = n_heads * c
    qgs = st(q * egc)
    kds = st(k * jnp.exp(gc_last - gc))
    s_old = [s_ref[b] for b in range(gb)]
    v_new = u - jnp.concatenate(
        [_mm(w[b * hc:(b + 1) * hc], s_old[b], passes=passes) for b in range(gb)], axis=0)
    o_s = jnp.concatenate(
        [_mm(qgs[b * hc:(b + 1) * hc], s_old[b], passes=passes) for b in range(gb)], axis=0)
    o_s = o_s + _mm(qk, v_new, passes=passes)
    for b in range(gb):
        sl = slice(b * hc, (b + 1) * hc)
        gl = jnp.exp(gc[(b + 1) * c - 1:(b + 1) * c, :])
        s_ref[b] = s_old[b] * gl + _mm(kds[sl], v_new[sl], _TN, passes)

    o = _unstack_heads(o_s, gb, c, n_heads)
    ms = _mm_exact_rhs(o * o, ones_bd) * (1.0 / HEAD_DIM)
    o = o * lax.rsqrt(ms + GDN_NORM_EPS) * ng_ref[...]
    z = cz_ref[...].reshape(gb * c, d)
    o_ref[...] = (o * _silu(z)).reshape(o_ref.shape)


def _gdn_mix(c_qkv, small, c_z, conv0, s0, lp, gb, c, passes):
    bsz, t_len, n_qkv = c_qkv.shape
    d = c_z.shape[-1]
    n_heads = d // HEAD_DIM
    n_conv = lp["conv_w"].shape[0]
    kern = functools.partial(_gdn_kernel, gb=gb, c=c, n_heads=n_heads, passes=passes)
    vec = lambda w: pl.BlockSpec((1, w), lambda b, s: (0, 0))
    return pl.pallas_call(
        kern,
        grid=(bsz // gb, t_len // c),
        in_specs=[pl.BlockSpec((gb, c, n_qkv), lambda b, s: (b, s, 0)),
                  pl.BlockSpec((gb, c, LANES), lambda b, s: (b, s, 0)),
                  pl.BlockSpec((gb, c, d), lambda b, s: (b, s, 0)),
                  pl.BlockSpec((gb, n_conv - 1, n_qkv), lambda b, s: (b, 0, 0)),
                  pl.BlockSpec((gb, d, d), lambda b, s: (b, 0, 0)),
                  pl.BlockSpec((n_conv, n_qkv), lambda b, s: (0, 0)),
                  vec(d), vec(d), vec(d),
                  pl.BlockSpec((LANES, d), lambda b, s: (0, 0)),
                  pl.BlockSpec((LANES, d), lambda b, s: (0, 0))],
        out_specs=[pl.BlockSpec((gb, c, d), lambda b, s: (b, s, 0)),
                   pl.BlockSpec((gb, d, d), lambda b, s: (b, 0, 0))],
        out_shape=[jax.ShapeDtypeStruct((bsz, t_len, d), F32),
                   jax.ShapeDtypeStruct((bsz, d, d), F32)],
        scratch_shapes=[pltpu.VMEM((gb, n_conv - 1, n_qkv), F32)],
        compiler_params=_cparams(("parallel", "arbitrary")),
        name="gdn_mix",
    )(c_qkv, small, c_z, conv0, s0, lp["conv_w"], lp["a_log"], lp["dt_bias"], lp["norm_g"],
      lp["e_beta"], lp["e_a"])


def _fox_gate_kernel(small_ref, bf_ref, epair_ref, logf_ref, ccol_ref, crow_ref, cpair_ref, carry_scr,
                     *, n_heads):
    step = pl.program_id(1)

    @pl.when(step == 0)
    def _():
        carry_scr[...] = jnp.zeros_like(carry_scr)

    x = small_ref[...] + bf_ref[...]
    logf = -_softplus(-x)
    tb = x.shape[0]
    c = _cumsum_rows(logf, 1, tb) + carry_scr[...]
    carry_scr[...] = c[tb - 1:tb, :]
    logf_ref[...] = logf[:, 0:n_heads]
    ccol_ref[...] = c
    sel = jnp.where(_iota((n_heads, LANES), 0) == _iota((n_heads, LANES), 1), 1.0, 0.0).astype(BF16)
    crow_ref[...] = _mm_exact_lhs(sel, c, _NT)
    for p in range(n_heads // 2):
        cpair_ref[p] = _mm_exact_rhs(c, epair_ref[p])


def _fox_gates(small, bf_pad, e_pair, n_heads, tb):
    bsz, t_len, _ = small.shape
    kern = functools.partial(_fox_gate_kernel, n_heads=n_heads)
    n_pairs = n_heads // 2
    return pl.pallas_call(
        kern,
        grid=(bsz, t_len // tb),
        in_specs=[pl.BlockSpec((None, tb, LANES), lambda b, s: (b, s, 0)),
                  pl.BlockSpec((1, LANES), lambda b, s: (0, 0)),
                  pl.BlockSpec((n_pairs, LANES, LANES), lambda b, s: (0, 0, 0))],
        out_specs=[pl.BlockSpec((None, tb, n_heads), lambda b, s: (b, s, 0)),
                   pl.BlockSpec((None, tb, LANES), lambda b, s: (b, s, 0)),
                   pl.BlockSpec((None, n_heads, tb), lambda b, s: (b, 0, s)),
                   pl.BlockSpec((None, n_pairs, tb, LANES), lambda b, s: (b, 0, s, 0))],
        out_shape=[jax.ShapeDtypeStruct((bsz, t_len, n_heads), F32),
                   jax.ShapeDtypeStruct((bsz, t_len, LANES), F32),
                   jax.ShapeDtypeStruct((bsz, n_heads, t_len), F32),
                   jax.ShapeDtypeStruct((bsz, n_pairs, t_len, LANES), F32)],
        scratch_shapes=[pltpu.VMEM((1, LANES), F32)],
        compiler_params=_cparams(("parallel", "arbitrary")),
        name="fox_gates",
    )(small, bf_pad, e_pair)


def _fox_prompt_kernel(qi_tab, ki_tab, q_ref, k_ref, v_ref, cq_ref, ck_ref, bz_ref, o_ref,
                       m_scr, l_scr, acc_scr):
    t = pl.program_id(2)
    qi = qi_tab[t]
    ki = ki_tab[t]
    tq = q_ref.shape[0]
    tk = k_ref.shape[0]

    @pl.when(ki == 0)
    def _():
        m_scr[...] = jnp.full_like(m_scr, NEG_BIG)
        l_scr[...] = jnp.zeros_like(l_scr)
        acc_scr[...] = jnp.zeros_like(acc_scr)

    lane = _iota((1, LANES), 1)

    def block(masked):
        q = q_ref[...] * (HEAD_DIM ** -0.5)
        kb = k_ref[...].astype(BF16)
        vb = v_ref[...].astype(BF16)
        for hh in range(2):
            in_head = (lane >= hh * HEAD_DIM) & (lane < (hh + 1) * HEAD_DIM)
            qh = jnp.where(in_head, q, 0.0).astype(BF16)
            s = _dg(qh, kb, _NT)
            s = s + cq_ref[:, hh * HEAD_DIM:hh * HEAD_DIM + 1] - ck_ref[hh:hh + 1, :]
            if masked:
                s = jnp.where(_iota((tq, tk), 1) <= _iota((tq, tk), 0), s, NEG_BIG)
            m_prev = m_scr[hh]
            m_new = jnp.maximum(m_prev, jnp.max(s, axis=1, keepdims=True))
            alpha = jnp.exp(m_prev - m_new)
            p = jnp.exp(s - m_new[:, 0:1])
            l_scr[hh] = alpha * l_scr[hh] + jnp.sum(p, axis=1, keepdims=True)
            acc_scr[hh] = alpha * acc_scr[hh] + _dg(p.astype(BF16), vb, _NN)
            m_scr[hh] = m_new

    @pl.when(ki < qi)
    def _():
        block(False)

    @pl.when(ki == qi)
    def _():
        block(True)
        o = jnp.where(lane < HEAD_DIM, acc_scr[0] / l_scr[0], acc_scr[1] / l_scr[1])
        o_ref[...] = o * _silu(bz_ref[...])


def _fox_prompt(q, k, v, cpair, crow, b_z, tq):
    bsz, t_len, d = q.shape
    n_pairs = d // LANES
    nq = t_len // tq
    qi_tab = np.concatenate([np.full(i + 1, i, np.int32) for i in range(nq)])
    ki_tab = np.concatenate([np.arange(i + 1, dtype=np.int32) for i in range(nq)])
    crow4 = crow.reshape(bsz, n_pairs, 2, t_len)
    grid_spec = pltpu.PrefetchScalarGridSpec(
        num_scalar_prefetch=2,
        grid=(bsz, n_pairs, len(qi_tab)),
        in_specs=[pl.BlockSpec((None, tq, LANES), lambda b, p, t, qt, kt: (b, qt[t], p)),
                  pl.BlockSpec((None, tq, LANES), lambda b, p, t, qt, kt: (b, kt[t], p)),
                  pl.BlockSpec((None, tq, LANES), lambda b, p, t, qt, kt: (b, kt[t], p)),
                  pl.BlockSpec((None, None, tq, LANES), lambda b, p, t, qt, kt: (b, p, qt[t], 0)),
                  pl.BlockSpec((None, None, 2, tq), lambda b, p, t, qt, kt: (b, p, 0, kt[t])),
                  pl.BlockSpec((None, tq, LANES), lambda b, p, t, qt, kt: (b, qt[t], p))],
        out_specs=pl.BlockSpec((None, tq, LANES), lambda b, p, t, qt, kt: (b, qt[t], p)),
        scratch_shapes=[pltpu.VMEM((2, tq, LANES), F32),
                        pltpu.VMEM((2, tq, LANES), F32),
                        pltpu.VMEM((2, tq, LANES), F32)],
    )
    return pl.pallas_call(
        _fox_prompt_kernel,
        grid_spec=grid_spec,
        out_shape=jax.ShapeDtypeStruct((bsz, t_len, d), F32),
        compiler_params=_cparams(("parallel", "parallel", "arbitrary")),
        name="fox_prompt",
    )(jnp.asarray(qi_tab), jnp.asarray(ki_tab), q, k, v, cpair, crow4, b_z)


def _fox_cached_kernel(q_ref, k_ref, v_ref, ck_ref, cv_ref, clf_ref, ccol_ref, crow_ref, bz_ref, o_ref,
                       *, n_heads, lane_blk):
    t_len, d = q_ref.shape
    past = ck_ref.shape[0]
    n_rows = n_heads * t_len

    triu = jnp.where(_iota((lane_blk, lane_blk), 0) <= _iota((lane_blk, lane_blk), 1), 1.0, 0.0).astype(BF16)
    carry = jnp.zeros((n_heads, 1), F32)
    pieces = []
    for j in range(past // lane_blk):
        cj = _mm_exact_rhs(clf_ref[:, j * lane_blk:(j + 1) * lane_blk], triu) + carry
        carry = cj[:, lane_blk - 1:lane_blk]
        pieces.append(cj)
    c_cache = jnp.concatenate(pieces, axis=1)
    total = carry

    rows = lambda f: jnp.concatenate([f(h) for h in range(n_heads)], axis=0)
    ck_cache = rows(lambda h: jnp.broadcast_to(c_cache[h:h + 1, :], (t_len, past)))
    ck_new = rows(lambda h: jnp.broadcast_to(total[h:h + 1, :] + crow_ref[h:h + 1, :], (t_len, t_len)))
    cq = rows(lambda h: total[h:h + 1, :] + ccol_ref[:, h:h + 1])

    lane_head = _div_pow2(_iota((1, d), 1), HEAD_DIM)
    q = q_ref[...] * (HEAD_DIM ** -0.5)
    q_bd = rows(lambda h: jnp.where(lane_head == h, q, 0.0)).astype(BF16)
    s_c = _dg(q_bd, ck_ref[...].astype(BF16), _NT) + cq - ck_cache
    s_n = _dg(q_bd, k_ref[...].astype(BF16), _NT) + cq - ck_new
    q_pos = _iota((n_rows, t_len), 0) & (t_len - 1)
    s_n = jnp.where(_iota((n_rows, t_len), 1) <= q_pos, s_n, NEG_BIG)
    m = jnp.maximum(jnp.max(s_c, axis=1, keepdims=True), jnp.max(s_n, axis=1, keepdims=True))
    p_c = jnp.exp(s_c - m)
    p_n = jnp.exp(s_n - m)
    l = jnp.sum(p_c, axis=1, keepdims=True) + jnp.sum(p_n, axis=1, keepdims=True)
    o_all = (_dg(p_c.astype(BF16), cv_ref[...].astype(BF16), _NN)
             + _dg(p_n.astype(BF16), v_ref[...].astype(BF16), _NN)) / l
    o = jnp.where(lane_head == 0, o_all[0:t_len], 0.0)
    for h in range(1, n_heads):
        o = o + jnp.where(lane_head == h, o_all[h * t_len:(h + 1) * t_len], 0.0)
    o_ref[...] = o * _silu(bz_ref[...])


def _fox_cached(q, k, v, cache_k, cache_v, cache_logf_t, ccol, crow, b_z, layer):
    bsz, t_len, d = q.shape
    n_heads = d // HEAD_DIM
    past = cache_k.shape[2]
    kern = functools.partial(_fox_cached_kernel, n_heads=n_heads, lane_blk=min(past, 256))
    cur = lambda w: pl.BlockSpec((None, t_len, w), lambda b: (b, 0, 0))
    return pl.pallas_call(
        kern,
        grid=(bsz,),
        in_specs=[cur(d), cur(d), cur(d),
                  pl.BlockSpec((None, None, past, d), lambda b: (layer, b, 0, 0)),
                  pl.BlockSpec((None, None, past, d), lambda b: (layer, b, 0, 0)),
                  pl.BlockSpec((None, None, n_heads, past), lambda b: (layer, b, 0, 0)),
                  cur(LANES),
                  pl.BlockSpec((None, n_heads, t_len), lambda b: (b, 0, 0)),
                  cur(d)],
        out_specs=cur(d),
        out_shape=jax.ShapeDtypeStruct((bsz, t_len, d), F32),
        compiler_params=_cparams(("parallel",)),
        name="fox_cached",
    )(q, k, v, cache_k, cache_v, cache_logf_t, ccol, crow, b_z)


def _out_kernel(oa_ref, ob_ref, oc_ref, x_ref, w_ref, g_ref, b_ref, y_ref, *, alpha):
    da = oa_ref.shape[-1]
    db = ob_ref.shape[-1]
    h = jnp.dot(oa_ref[...].astype(BF16), w_ref[0:da, :], preferred_element_type=F32)
    h = h + jnp.dot(ob_ref[...].astype(BF16), w_ref[da:da + db, :], preferred_element_type=F32)
    h = h + jnp.dot(oc_ref[...].astype(BF16), w_ref[da + db:, :], preferred_element_type=F32)
    y = alpha * x_ref[...] + h
    mu = jnp.mean(y, axis=-1, keepdims=True)
    yc = y - mu
    var = jnp.mean(yc * yc, axis=-1, keepdims=True)
    y_ref[...] = yc * lax.rsqrt(var + LN_EPS) * g_ref[...] + b_ref[...]


def _out_project(o_a, o_b, o_c, x2, w_all, layer, g, b, alpha, tm):
    n, d = x2.shape
    d_mix = w_all.shape[1]
    row = lambda w: pl.BlockSpec((tm, w), lambda i: (i, 0))
    return pl.pallas_call(
        functools.partial(_out_kernel, alpha=alpha),
        grid=(n // tm,),
        in_specs=[row(o_a.shape[-1]), row(o_b.shape[-1]), row(o_c.shape[-1]), row(d),
                  pl.BlockSpec((None, d_mix, d), lambda i: (layer, 0, 0)),
                  pl.BlockSpec((1, d), lambda i: (0, 0)),
                  pl.BlockSpec((1, d), lambda i: (0, 0))],
        out_specs=row(d),
        out_shape=jax.ShapeDtypeStruct((n, d), F32),
        compiler_params=_cparams(("parallel",)),
        name="proj_out",
    )(o_a, o_b, o_c, x2, w_all, g.reshape(1, d), b.reshape(1, d))


def _block_diag_state(s):
    bsz, n_h, d0, d1 = s.shape
    eye = jnp.eye(n_h, dtype=s.dtype)
    return jnp.einsum('bhij,hg->bhigj', s, eye).reshape(bsz, n_h * d0, n_h * d1)


def _diag_blocks(s, n_h):
    bsz = s.shape[0]
    s5 = s.reshape(bsz, n_h, HEAD_DIM, n_h, HEAD_DIM)
    return jnp.stack([s5[:, h, :, h, :] for h in range(n_h)], axis=1)


def _expand_heads(p):
    return jnp.repeat(p.astype(F32), HEAD_DIM).reshape(1, -1)


def _select_matrix(rows, n_cols, first_row):
    e = np.zeros((LANES, n_cols), np.float32)
    for h in range(rows):
        e[first_row + h, h * HEAD_DIM:(h + 1) * HEAD_DIM] = 1.0
    return jnp.asarray(e, BF16)


def _row_tile(n):
    for tm in (256, 128, 64, 32, 16, 8):
        if n % tm == 0:
            return tm
    raise ValueError(f"row count {n} must be a multiple of 8")


def kernel(x_prompt, x_sample, cache_fox_k, cache_fox_v, cache_fox_logf, state_rwkv_shift, state_rwkv_wkv,
           state_gdn_conv, state_gdn_wkv, ln_in_g, ln_in_b, w_in, rwkv_mu, rwkv_w0, rwkv_w2, rwkv_a0, rwkv_a2,
           rwkv_k_k, rwkv_k_a, rwkv_r_k, rwkv_gn_g, rwkv_gn_b, fox_b_f, gdn_conv_w, gdn_a_log, gdn_dt_bias,
           gdn_norm_g, w_out, ln_post_g, ln_post_b):
    depth, d_model, _ = w_in.shape
    bp, seq, _ = x_prompt.shape
    bs, dec_seq, _ = x_sample.shape
    h_fox = fox_b_f.shape[1]
    h_gdn = gdn_a_log.shape[1]
    h_rwkv = rwkv_r_k.shape[1]
    d_rwkv, d_fox, d_gdn = h_rwkv * HEAD_DIM, h_fox * HEAD_DIM, h_gdn * HEAD_DIM
    rank_w = rwkv_w2.shape[1]
    rank_a = rwkv_a2.shape[1]
    n_shift = 3 * d_rwkv + rank_w + rank_a
    n_conv_cols = 3 * d_gdn
    n_conv = gdn_conv_w.shape[1]
    alpha = (2 * depth) ** 0.25
    assert rank_w + rank_a == LANES and h_fox + 2 * h_gdn <= LANES

    split = [n_shift, d_rwkv, d_fox, d_fox, d_fox, h_fox, d_fox, n_conv_cols, h_gdn, h_gdn, d_gdn]
    offs = np.concatenate([[0], np.cumsum(split)])
    seg = lambda i: w_in[:, :, offs[i]:offs[i + 1]]
    small_w = jnp.concatenate([seg(5), seg(8), seg(9)], axis=-1)
    small_w = jnp.pad(small_w, ((0, 0), (0, 0), (0, LANES - small_w.shape[-1])))
    w_in_p = jnp.concatenate([seg(0), seg(1), seg(2), seg(3), seg(4), seg(6), seg(7), seg(10), small_w],
                             axis=-1).astype(BF16)
    widths = [n_shift, d_rwkv, d_fox, d_fox, d_fox, d_fox, n_conv_cols, d_gdn, LANES]
    w_out_b = w_out.astype(BF16)

    zw = jnp.zeros((depth, rank_w, d_rwkv), F32)
    w2a2 = jnp.concatenate([jnp.concatenate([rwkv_w2, zw], axis=-1),
                            jnp.concatenate([zw, rwkv_a2], axis=-1)], axis=1).astype(BF16)
    bf_pad = jnp.pad(fox_b_f, ((0, 0), (0, LANES - h_fox)))
    e_beta = _select_matrix(h_gdn, d_gdn, h_fox)
    e_a = _select_matrix(h_gdn, d_gdn, h_fox + h_gdn)
    e_pair_np = np.zeros((h_fox // 2, LANES, LANES), np.float32)
    for p in range(h_fox // 2):
        e_pair_np[p, 2 * p, 0:HEAD_DIM] = 1.0
        e_pair_np[p, 2 * p + 1, HEAD_DIM:] = 1.0
    e_pair = jnp.asarray(e_pair_np, BF16)
    cache_logf_t = jnp.swapaxes(cache_fox_logf, 2, 3)

    def layer_params(l):
        row = lambda a: a[l].reshape(1, -1).astype(F32)
        return dict(
            rwkv=dict(mu=row(rwkv_mu), w0=row(rwkv_w0), a0=row(rwkv_a0), w2a2=w2a2[l], k_k=row(rwkv_k_k),
                      k_a=row(rwkv_k_a), r_k=row(rwkv_r_k), gn_g=row(rwkv_gn_g), gn_b=row(rwkv_gn_b)),
            gdn=dict(conv_w=gdn_conv_w[l], a_log=_expand_heads(gdn_a_log[l]),
                     dt_bias=_expand_heads(gdn_dt_bias[l]),
                     norm_g=jnp.tile(gdn_norm_g[l].reshape(1, -1), (1, h_gdn)), e_beta=e_beta, e_a=e_a),
        )

    def run_layer(l, x2, bsz, t_len, prev0, s_rwkv0, conv0, s_gdn0, cached):
        lp = layer_params(l)
        tm = _row_tile(x2.shape[0])
        a_sh, a_z, b_q, b_k, b_v, b_z, c_qkv, c_z, small = _project(x2, w_in_p, l, widths, tm)
        r3 = lambda a: a.reshape(bsz, t_len, a.shape[-1])
        a_sh, a_z, b_q, b_k, b_v, b_z, c_qkv, c_z, small = map(r3, (a_sh, a_z, b_q, b_k, b_v, b_z, c_qkv, c_z, small))
        c = min(HEAD_DIM, t_len)
        gb = N_STACK // (h_rwkv * c)
        o_a, s_rwkv = _rwkv_mix(a_sh, a_z, prev0, s_rwkv0, lp["rwkv"], gb, c, 3)
        gb = N_STACK // (h_gdn * c)
        o_c, s_gdn = _gdn_mix(c_qkv, small, c_z, conv0, s_gdn0, lp["gdn"], gb, c, 3)
        logf, ccol, crow, cpair = _fox_gates(small, bf_pad[l:l + 1], e_pair, h_fox, min(t_len, 256))
        if cached:
            o_b = _fox_cached(b_q, b_k, b_v, cache_fox_k.reshape(depth, bsz, -1, d_fox),
                              cache_fox_v.reshape(depth, bsz, -1, d_fox), cache_logf_t, ccol, crow, b_z, l)
        else:
            o_b = _fox_prompt(b_q, b_k, b_v, cpair, crow, b_z, min(t_len, 512))
        f2 = lambda a: a.reshape(bsz * t_len, a.shape[-1])
        x_new = _out_project(f2(o_a), f2(o_b), f2(o_c), x2, w_out_b, l, ln_post_g[l], ln_post_b[l], alpha, tm)
        outs = (b_k.reshape(bsz, t_len, h_fox, HEAD_DIM), b_v.reshape(bsz, t_len, h_fox, HEAD_DIM), logf,
                a_sh[:, -1], _diag_blocks(s_rwkv, h_rwkv), c_qkv[:, t_len - (n_conv - 1):],
                _diag_blocks(s_gdn, h_gdn))
        return x_new, outs

    xp = _layer_norm(x_prompt.reshape(bp * seq, d_model), ln_in_g, ln_in_b, _row_tile(bp * seq))
    xs = _layer_norm(x_sample.reshape(bs * dec_seq, d_model), ln_in_g, ln_in_b, _row_tile(bs * dec_seq))
    zeros_p = (jnp.zeros((bp, 1, n_shift), F32), jnp.zeros((bp, d_rwkv, d_rwkv), F32),
               jnp.zeros((bp, n_conv - 1, n_conv_cols), F32), jnp.zeros((bp, d_gdn, d_gdn), F32))
    outs_p, outs_s = [], []
    for l in range(depth):
        xp, o = run_layer(l, xp, bp, seq, *zeros_p, False)
        outs_p.append(o)
        xs, o = run_layer(l, xs, bs, dec_seq, state_rwkv_shift[l][:, None, :],
                          _block_diag_state(state_rwkv_wkv[l]), state_gdn_conv[l],
                          _block_diag_state(state_gdn_wkv[l]), True)
        outs_s.append(o)
    stack = lambda outs, i: jnp.stack([o[i] for o in outs])
    return ((xp.reshape(bp, seq, d_model), xs.reshape(bs, dec_seq, d_model))
            + tuple(stack(outs_p, i) for i in range(7)) + tuple(stack(outs_s, i) for i in range(7)))
```

```python
import functools
import math

import jax
import jax.numpy as jnp
import numpy as np
from jax import lax
from jax.experimental import pallas as pl
from jax.experimental.pallas import tpu as pltpu

F32 = jnp.float32
BF16 = jnp.bfloat16

HEAD_DIM = 64
LANES = 128
N_STACK = 256
FOX_BLOCK = 512
N_CHAINS = 4
CHUNK_PASSES = 1
LN_EPS = 1e-5
RWKV_GN_EPS = 64e-5
GDN_NORM_EPS = 1e-6
L2_EPS = 1e-6
NEG_BIG = -1e30
LOG2E = math.log2(math.e)
VMEM_LIMIT = 56 * 1024 * 1024


def _sigmoid(x):
    return 1.0 / (1.0 + jnp.exp(-x))


def _softplus(x):
    return jnp.maximum(x, 0.0) + jnp.log(1.0 + jnp.exp(-jnp.abs(x)))


def _silu(x):
    return x * _sigmoid(x)


def _split3(x):
    hi = x.astype(BF16)
    r1 = x - hi.astype(F32)
    mid = r1.astype(BF16)
    lo = (r1 - mid.astype(F32)).astype(BF16)
    return hi, mid, lo


_NN = (((1,), (0,)), ((), ()))
_NT = (((1,), (1,)), ((), ()))
_TN = (((0,), (0,)), ((), ()))


def _dg(a, b, dims):
    return lax.dot_general(a, b, dims, preferred_element_type=F32)


def _mm(a, b, dims=_NN, passes=1):
    if passes == 1:
        return _dg(a.astype(BF16), b.astype(BF16), dims)
    ah, am, _ = _split3(a)
    bh, bm, _ = _split3(b)
    return _dg(ah, bh, dims) + (_dg(ah, bm, dims) + _dg(am, bh, dims))


def _mm_exact_rhs(a, e, dims=_NN):
    hi, mid, lo = _split3(a)
    return _dg(hi, e, dims) + (_dg(mid, e, dims) + _dg(lo, e, dims))


def _mm_exact_lhs(e, b, dims=_NN):
    hi, mid, lo = _split3(b)
    return _dg(e, hi, dims) + (_dg(e, mid, dims) + _dg(e, lo, dims))


def _iota(shape, dim):
    return lax.broadcasted_iota(jnp.int32, shape, dim)


def _div_pow2(x, n):
    assert n & (n - 1) == 0
    return lax.shift_right_logical(x, jnp.int32(int(math.log2(n))))


def _head_block_ones(n):
    r = _div_pow2(_iota((n, n), 0), HEAD_DIM)
    c = _div_pow2(_iota((n, n), 1), HEAD_DIM)
    return jnp.where(r == c, 1.0, 0.0).astype(BF16)


def _stack_heads(x, gb, c, n_heads):
    width = x.shape[1]
    lane_head = _div_pow2(_iota((1, width), 1), HEAD_DIM)
    parts = []
    for b in range(gb):
        xb = x[b * c:(b + 1) * c, :]
        for h in range(n_heads):
            parts.append(jnp.where(lane_head == h, xb, 0.0))
    return jnp.concatenate(parts, axis=0)


def _unstack_heads(xs, gb, c, n_heads):
    outs = []
    for b in range(gb):
        acc = xs[(b * n_heads) * c:(b * n_heads + 1) * c, :]
        for h in range(1, n_heads):
            acc = acc + xs[(b * n_heads + h) * c:(b * n_heads + h + 1) * c, :]
        outs.append(acc)
    return jnp.concatenate(outs, axis=0) if gb > 1 else outs[0]


def _unit_lower_inverse(p, c, passes):
    n = p.shape[0]
    eye = jnp.where(_iota((n, n), 0) == _iota((n, n), 1), 1.0, 0.0)
    t = eye + p
    n_rounds = int(math.log2(c)) - 1
    pk = _mm(p, p, passes=passes)
    yield
    for i in range(n_rounds):
        t_next = t + _mm(t, pk, passes=passes)
        if i + 1 < n_rounds:
            pk = _mm(pk, pk, passes=passes)
        t = t_next
        yield
    return t


def _run_interleaved(chains):
    results = [None] * len(chains)
    active = list(enumerate(chains))
    while active:
        still = []
        for i, ch in active:
            try:
                next(ch)
                still.append((i, ch))
            except StopIteration as stop:
                results[i] = stop.value
        active = still
    return results


def _chunk_masks(n, c):
    row = _iota((n, n), 0)
    col = _iota((n, n), 1)
    same = _div_pow2(row, c) == _div_pow2(col, c)
    strict = jnp.logical_and(same, row > col)
    incl = jnp.logical_and(same, row >= col)
    return strict, incl


def _cumsum_rows(x, gb, c):
    n = gb * c
    strict, incl = _chunk_masks(n, c)
    del strict
    tri = jnp.where(incl, 1.0, 0.0).astype(BF16)
    return _mm_exact_lhs(tri, x)


def _shift_rows(x, j, prev_rows, gb, c):
    n_prev = prev_rows[0].shape[0]
    out = pltpu.roll(x, j, 0)
    rid = _iota((x.shape[0], 1), 0)
    for b in range(gb):
        for t in range(j):
            src = n_prev + t - j
            out = jnp.where(rid == b * c + t, prev_rows[b][src:src + 1, :], out)
    return out


def _cparams(sem):
    return pltpu.CompilerParams(dimension_semantics=sem, vmem_limit_bytes=VMEM_LIMIT)


def _ln_kernel(x_ref, g_ref, b_ref, o_ref):
    x = x_ref[...]
    mu = jnp.mean(x, axis=-1, keepdims=True)
    xc = x - mu
    var = jnp.mean(xc * xc, axis=-1, keepdims=True)
    o_ref[...] = xc * lax.rsqrt(var + LN_EPS) * g_ref[...] + b_ref[...]


def _layer_norm(x2, g, b, tm):
    n, d = x2.shape
    return pl.pallas_call(
        _ln_kernel,
        grid=(n // tm,),
        in_specs=[pl.BlockSpec((tm, d), lambda i: (i, 0)),
                  pl.BlockSpec((1, d), lambda i: (0, 0)),
                  pl.BlockSpec((1, d), lambda i: (0, 0))],
        out_specs=pl.BlockSpec((tm, d), lambda i: (i, 0)),
        out_shape=jax.ShapeDtypeStruct((n, d), F32),
        compiler_params=_cparams(("parallel",)),
        name="ln_in",
    )(x2, g.reshape(1, d), b.reshape(1, d))


def _proj_kernel(x_ref, w_ref, *o_refs):
    xb = x_ref[...].astype(BF16)
    off = 0
    for o_ref in o_refs:
        wdt = o_ref.shape[-1]
        o_ref[...] = jnp.dot(xb, w_ref[:, off:off + wdt], preferred_element_type=F32)
        off += wdt


def _project(x2, w_all, layer, widths, tm):
    n, d = x2.shape
    n_cols = w_all.shape[-1]
    return pl.pallas_call(
        _proj_kernel,
        grid=(n // tm,),
        in_specs=[pl.BlockSpec((tm, d), lambda i: (i, 0)),
                  pl.BlockSpec((None, d, n_cols), lambda i: (layer, 0, 0))],
        out_specs=[pl.BlockSpec((tm, w), lambda i: (i, 0)) for w in widths],
        out_shape=[jax.ShapeDtypeStruct((n, w), F32) for w in widths],
        compiler_params=_cparams(("parallel",)),
        name="proj_in",
    )(x2, w_all)


def _rwkv_kernel(ash_ref, az_ref, prev0_ref, s0_ref, mu_ref, w0_ref, a0_ref, w2a2_ref, kk_ref, ka_ref,
                 rk_ref, gng_ref, gnb_ref, o_ref, s_ref, prev_scr, *, gb, n_chains, c, n_heads, passes):
    @pl.when(pl.program_id(1) == 0)
    def _():
        s_ref[...] = s0_ref[...]
        prev_scr[...] = prev0_ref[...]

    prm = dict(mu=mu_ref[...], w0=w0_ref[...], a0=a0_ref[...], w2a2=w2a2_ref[...], k_k=kk_ref[...],
               k_a=ka_ref[...], r_k=rk_ref[...], gn_g=gng_ref[...], gn_b=gnb_ref[...])
    ins = []
    for ch in range(n_chains):
        b0 = ch * gb
        ins.append((ash_ref[b0:b0 + gb].reshape(gb * c, ash_ref.shape[-1]),
                    az_ref[b0:b0 + gb].reshape(gb * c, az_ref.shape[-1]),
                    [prev_scr[b0 + b] for b in range(gb)],
                    [s_ref[b0 + b] for b in range(gb)]))
    outs = _run_interleaved([_rwkv_chain(*args, prm, gb=gb, c=c, n_heads=n_heads, passes=passes) for args in ins])
    for ch, (o, s_new, last_rows) in enumerate(outs):
        b0 = ch * gb
        o_ref[b0:b0 + gb] = o.reshape(gb, c, o.shape[-1])
        for b in range(gb):
            s_ref[b0 + b] = s_new[b]
            prev_scr[b0 + b] = last_rows[b]


def _rwkv_chain(p, z, prev_rows, s_old, prm, *, gb, c, n_heads, passes):
    d = n_heads * HEAD_DIM
    prev = _shift_rows(p, 1, prev_rows, gb, c)
    last_rows = [p[(b + 1) * c - 1:(b + 1) * c, :] for b in range(gb)]
    xs = p + (prev - p) * prm["mu"]
    r = xs[:, 0:d]
    k = xs[:, d:2 * d]
    v = xs[:, 2 * d:3 * d]
    wa = xs[:, 3 * d:3 * d + LANES]
    lane = _iota((1, LANES), 1)
    wa = jnp.where(lane < HEAD_DIM, jnp.tanh(wa), wa)
    pre = _mm(wa, prm["w2a2"])
    yield
    w_ll = -_softplus(-(prm["w0"] + pre[:, 0:d])) - 0.5
    lw = -jnp.exp(w_ll)
    a = _sigmoid(prm["a0"] + pre[:, d:2 * d])

    ones_bd = _head_block_ones(d)
    kk = k * prm["k_k"]
    kk = kk / jnp.maximum(jnp.sqrt(_mm_exact_rhs(kk * kk, ones_bd)), 1e-12)
    kmod = k * (1.0 + (a - 1.0) * prm["k_a"])
    kka = kk * a

    cum = _cumsum_rows(lw, gb, c)
    yield
    cum_last = jnp.concatenate(
        [jnp.broadcast_to(cum[(b + 1) * c - 1:(b + 1) * c, :], (c, d)) for b in range(gb)], axis=0)
    inv_p = jnp.exp(-cum)
    to_end = jnp.exp(cum_last - cum)
    b_t = kk * jnp.exp(cum - lw)
    a_t = -kka * inv_p
    k_t = kmod * inv_p
    r_t = r * jnp.exp(cum)

    st = functools.partial(_stack_heads, gb=gb, c=c, n_heads=n_heads)
    bs, rs, as_, ks, vs = st(b_t), st(r_t), st(a_t), st(k_t), st(v)
    n = bs.shape[0]
    g = _mm(jnp.concatenate([bs, rs], axis=0), jnp.concatenate([as_, ks], axis=0), _NT, passes)
    yield
    strict, incl = _chunk_masks(n, c)
    m_ab = jnp.where(strict, g[0:n, 0:n], 0.0)
    m_bk = jnp.where(strict, g[0:n, n:2 * n], 0.0)
    m_ra = jnp.where(incl, g[n:2 * n, 0:n], 0.0)
    m_rk = jnp.where(incl, g[n:2 * n, n:2 * n], 0.0)

    hc = n_heads * c
    bh = jnp.concatenate([_mm(bs[b * hc:(b + 1) * hc], s_old[b], _NT, passes) for b in range(gb)], axis=0)
    rh = jnp.concatenate([_mm(rs[b * hc:(b + 1) * hc], s_old[b], _NT, passes) for b in range(gb)], axis=0)
    rhs_u = bh + _mm(m_bk, vs, passes=passes)
    o_s = rh + _mm(m_rk, vs, passes=passes)
    t_inv = yield from _unit_lower_inverse(m_ab, c, passes)
    u = _mm(t_inv, rhs_u, passes=passes)
    yield
    o_s = o_s + _mm(m_ra, u, passes=passes)
    o = _unstack_heads(o_s, gb, c, n_heads)

    a_end = st(-kka * to_end)
    k_end = st(kmod * to_end)
    s_new = []
    for b in range(gb):
        sl = slice(b * hc, (b + 1) * hc)
        p_end = jnp.exp(cum[(b + 1) * c - 1:(b + 1) * c, :])
        s_new.append(s_old[b] * p_end + _mm(u[sl], a_end[sl], _TN, passes)
                     + _mm(vs[sl], k_end[sl], _TN, passes))
    yield

    mean = _mm_exact_rhs(o, ones_bd) * (1.0 / HEAD_DIM)
    oc = o - mean
    yield
    var = _mm_exact_rhs(oc * oc, ones_bd) * (1.0 / HEAD_DIM)
    o = oc * lax.rsqrt(var + RWKV_GN_EPS) * prm["gn_g"] + prm["gn_b"]
    bonus = _mm_exact_rhs(r * kmod * prm["r_k"], ones_bd) * v
    return (o + bonus) * _silu(z), s_new, last_rows


def _rwkv_mix(a_sh, a_z, prev0, s0, lp, gb, n_chains, c, passes):
    bsz, t_len, n_shift = a_sh.shape
    d = a_z.shape[-1]
    n_heads = d // HEAD_DIM
    kern = functools.partial(_rwkv_kernel, gb=gb, n_chains=n_chains, c=c, n_heads=n_heads, passes=passes)
    gb = gb * n_chains
    vec = lambda w: pl.BlockSpec((1, w), lambda b, s: (0, 0))
    return pl.pallas_call(
        kern,
        grid=(bsz // gb, t_len // c),
        in_specs=[pl.BlockSpec((gb, c, n_shift), lambda b, s: (b, s, 0)),
                  pl.BlockSpec((gb, c, d), lambda b, s: (b, s, 0)),
                  pl.BlockSpec((gb, 1, n_shift), lambda b, s: (b, 0, 0)),
                  pl.BlockSpec((gb, d, d), lambda b, s: (b, 0, 0)),
                  vec(n_shift), vec(d), vec(d),
                  pl.BlockSpec((LANES, 2 * d), lambda b, s: (0, 0)),
                  vec(d), vec(d), vec(d), vec(d), vec(d)],
        out_specs=[pl.BlockSpec((gb, c, d), lambda b, s: (b, s, 0)),
                   pl.BlockSpec((gb, d, d), lambda b, s: (b, 0, 0))],
        out_shape=[jax.ShapeDtypeStruct((bsz, t_len, d), F32),
                   jax.ShapeDtypeStruct((bsz, d, d), F32)],
        scratch_shapes=[pltpu.VMEM((gb, 1, n_shift), F32)],
        compiler_params=_cparams(("parallel", "arbitrary")),
        name="rwkv_mix",
    )(a_sh, a_z, prev0, s0, lp["mu"], lp["w0"], lp["a0"], lp["w2a2"], lp["k_k"], lp["k_a"], lp["r_k"],
      lp["gn_g"], lp["gn_b"])


def _gdn_kernel(qkv_ref, small_ref, cz_ref, conv0_ref, s0_ref, convw_ref, alog_ref, dtb_ref, ng_ref,
                eb_ref, ea_ref, o_ref, s_ref, conv_scr, *, gb, n_chains, c, n_heads, passes):
    @pl.when(pl.program_id(1) == 0)
    def _():
        s_ref[...] = s0_ref[...]
        conv_scr[...] = conv0_ref[...]

    prm = dict(conv_w=convw_ref[...], a_log=alog_ref[...], dt_bias=dtb_ref[...], norm_g=ng_ref[...],
               e_beta=eb_ref[...], e_a=ea_ref[...])
    ins = []
    for ch in range(n_chains):
        b0 = ch * gb
        ins.append((qkv_ref[b0:b0 + gb].reshape(gb * c, qkv_ref.shape[-1]),
                    small_ref[b0:b0 + gb].reshape(gb * c, LANES),
                    cz_ref[b0:b0 + gb].reshape(gb * c, cz_ref.shape[-1]),
                    [conv_scr[b0 + b] for b in range(gb)],
                    [s_ref[b0 + b] for b in range(gb)]))
    outs = _run_interleaved([_gdn_chain(*args, prm, gb=gb, c=c, n_heads=n_heads, passes=passes) for args in ins])
    for ch, (o, s_new, last_rows) in enumerate(outs):
        b0 = ch * gb
        o_ref[b0:b0 + gb] = o.reshape(gb, c, o.shape[-1])
        for b in range(gb):
            s_ref[b0 + b] = s_new[b]
            conv_scr[b0 + b] = last_rows[b]


def _gdn_chain(x, small, z, prev_rows, s_old, prm, *, gb, c, n_heads, passes):
    d = n_heads * HEAD_DIM
    conv_w = prm["conv_w"]
    n_conv = conv_w.shape[0]
    y = x * conv_w[n_conv - 1:n_conv, :]
    for j in range(1, n_conv):
        y = y + _shift_rows(x, j, prev_rows, gb, c) * conv_w[n_conv - 1 - j:n_conv - j, :]
    last_rows = [x[(b + 1) * c - (n_conv - 1):(b + 1) * c, :] for b in range(gb)]
    y = _silu(y)
    q = y[:, 0:d]
    k = y[:, d:2 * d]
    v = y[:, 2 * d:3 * d]
    ones_bd = _head_block_ones(d)
    q = q * lax.rsqrt(_mm_exact_rhs(q * q, ones_bd) + L2_EPS) * (HEAD_DIM ** -0.5)
    k = k * lax.rsqrt(_mm_exact_rhs(k * k, ones_bd) + L2_EPS)

    beta = _sigmoid(_mm_exact_rhs(small, prm["e_beta"]))
    g = -jnp.exp(prm["a_log"]) * _softplus(_mm_exact_rhs(small, prm["e_a"]) + prm["dt_bias"])
    yield
    gc = _cumsum_rows(g, gb, c)
    yield
    gc_last = jnp.concatenate(
        [jnp.broadcast_to(gc[(b + 1) * c - 1:(b + 1) * c, :], (c, d)) for b in range(gb)], axis=0)
    egc = jnp.exp(gc)

    st = functools.partial(_stack_heads, gb=gb, c=c, n_heads=n_heads)
    kb = k * beta
    kbs, ks, qs = st(kb), st(k), st(q)
    n = kbs.shape[0]
    strict, incl = _chunk_masks(n, c)
    gcol = jnp.min(st(gc), axis=1, keepdims=True)
    eye = _iota((n, n), 0) == _iota((n, n), 1)
    grow = jnp.sum(jnp.where(eye, jnp.broadcast_to(gcol, (n, n)), 0.0), axis=0, keepdims=True)
    dmat = jnp.exp(jnp.where(incl, gcol - grow, 0.0))
    g2 = _mm(jnp.concatenate([kbs, qs], axis=0), ks, _NT, passes)
    yield
    m = jnp.where(strict, g2[0:n] * dmat, 0.0)
    qk = jnp.where(incl, g2[n:2 * n] * dmat, 0.0)
    t_inv = yield from _unit_lower_inverse(-m, c, passes)
    rhs = jnp.concatenate([st(v * beta), st(kb * egc)], axis=1)
    sol = _mm(t_inv, rhs, passes=passes)
    yield
    u = sol[:, 0:d]
    w = sol[:, d:2 * d]

    hc = n_heads * c
    qgs = st(q * egc)
    kds = st(k * jnp.exp(gc_last - gc))
    v_new = u - jnp.concatenate(
        [_mm(w[b * hc:(b + 1) * hc], s_old[b], passes=passes) for b in range(gb)], axis=0)
    o_s = jnp.concatenate(
        [_mm(qgs[b * hc:(b + 1) * hc], s_old[b], passes=passes) for b in range(gb)], axis=0)
    yield
    o_s = o_s + _mm(qk, v_new, passes=passes)
    s_new = []
    for b in range(gb):
        sl = slice(b * hc, (b + 1) * hc)
        gl = jnp.exp(gc[(b + 1) * c - 1:(b + 1) * c, :])
        s_new.append(s_old[b] * gl + _mm(kds[sl], v_new[sl], _TN, passes))
    yield

    o = _unstack_heads(o_s, gb, c, n_heads)
    ms = _mm_exact_rhs(o * o, ones_bd) * (1.0 / HEAD_DIM)
    o = o * lax.rsqrt(ms + GDN_NORM_EPS) * prm["norm_g"]
    return o * _silu(z), s_new, last_rows


def _gdn_mix(c_qkv, small, c_z, conv0, s0, lp, gb, n_chains, c, passes):
    bsz, t_len, n_qkv = c_qkv.shape
    d = c_z.shape[-1]
    n_heads = d // HEAD_DIM
    n_conv = lp["conv_w"].shape[0]
    kern = functools.partial(_gdn_kernel, gb=gb, n_chains=n_chains, c=c, n_heads=n_heads, passes=passes)
    gb = gb * n_chains
    vec = lambda w: pl.BlockSpec((1, w), lambda b, s: (0, 0))
    return pl.pallas_call(
        kern,
        grid=(bsz // gb, t_len // c),
        in_specs=[pl.BlockSpec((gb, c, n_qkv), lambda b, s: (b, s, 0)),
                  pl.BlockSpec((gb, c, LANES), lambda b, s: (b, s, 0)),
                  pl.BlockSpec((gb, c, d), lambda b, s: (b, s, 0)),
                  pl.BlockSpec((gb, n_conv - 1, n_qkv), lambda b, s: (b, 0, 0)),
                  pl.BlockSpec((gb, d, d), lambda b, s: (b, 0, 0)),
                  pl.BlockSpec((n_conv, n_qkv), lambda b, s: (0, 0)),
                  vec(d), vec(d), vec(d),
                  pl.BlockSpec((LANES, d), lambda b, s: (0, 0)),
                  pl.BlockSpec((LANES, d), lambda b, s: (0, 0))],
        out_specs=[pl.BlockSpec((gb, c, d), lambda b, s: (b, s, 0)),
                   pl.BlockSpec((gb, d, d), lambda b, s: (b, 0, 0))],
        out_shape=[jax.ShapeDtypeStruct((bsz, t_len, d), F32),
                   jax.ShapeDtypeStruct((bsz, d, d), F32)],
        scratch_shapes=[pltpu.VMEM((gb, n_conv - 1, n_qkv), F32)],
        compiler_params=_cparams(("parallel", "arbitrary")),
        name="gdn_mix",
    )(c_qkv, small, c_z, conv0, s0, lp["conv_w"], lp["a_log"], lp["dt_bias"], lp["norm_g"],
      lp["e_beta"], lp["e_a"])


def _log_forget_cumsum(small_ref, bf_ref, carry_scr):
    @pl.when(pl.program_id(1) == 0)
    def _():
        carry_scr[...] = jnp.zeros_like(carry_scr)

    logf = -_softplus(-(small_ref[...] + bf_ref[...]))
    tb = logf.shape[0]
    c = _cumsum_rows(logf, 1, tb) + carry_scr[...]
    carry_scr[...] = c[tb - 1:tb, :]
    return logf, c


def _fox_gate_kernel(small_ref, bf_ref, logf_ref, ccol_ref, crow_ref, carry_scr, *, n_heads):
    logf, c = _log_forget_cumsum(small_ref, bf_ref, carry_scr)
    logf_ref[...] = logf[:, 0:n_heads]
    ccol_ref[...] = c
    sel = jnp.where(_iota((n_heads, LANES), 0) == _iota((n_heads, LANES), 1), 1.0, 0.0).astype(BF16)
    crow_ref[...] = _mm_exact_lhs(sel, c, _NT)


def _fox_gates(small, bf_pad, n_heads, tb):
    bsz, t_len, _ = small.shape
    kern = functools.partial(_fox_gate_kernel, n_heads=n_heads)
    return pl.pallas_call(
        kern,
        grid=(bsz, t_len // tb),
        in_specs=[pl.BlockSpec((None, tb, LANES), lambda b, s: (b, s, 0)),
                  pl.BlockSpec((1, LANES), lambda b, s: (0, 0))],
        out_specs=[pl.BlockSpec((None, tb, n_heads), lambda b, s: (b, s, 0)),
                   pl.BlockSpec((None, tb, LANES), lambda b, s: (b, s, 0)),
                   pl.BlockSpec((None, n_heads, tb), lambda b, s: (b, 0, s))],
        out_shape=[jax.ShapeDtypeStruct((bsz, t_len, n_heads), F32),
                   jax.ShapeDtypeStruct((bsz, t_len, LANES), F32),
                   jax.ShapeDtypeStruct((bsz, n_heads, t_len), F32)],
        scratch_shapes=[pltpu.VMEM((1, LANES), F32)],
        compiler_params=_cparams(("parallel", "arbitrary")),
        name="fox_gates",
    )(small, bf_pad)


N_AUX = 3


def _fox_prep_kernel(q_ref, k_ref, v_ref, small_ref, bf_ref, eaux_ref, logf_ref, qa_ref, ka_ref, va_ref,
                     carry_scr, *, n_heads):
    logf, c = _log_forget_cumsum(small_ref, bf_ref, carry_scr)
    logf_ref[...] = logf[:, 0:n_heads]
    parts = jnp.concatenate(_split3(c * LOG2E), axis=1)
    lane = _iota((1, LANES), 1)
    for h in range(n_heads):
        pair, hh = divmod(h, 2)
        own = (lane < HEAD_DIM) if hh == 0 else (lane >= HEAD_DIM)
        a0 = HEAD_DIM if hh == 0 else 0
        aux_q = jnp.where((lane >= a0) & (lane < a0 + N_AUX), 1.0, 0.0)
        aux_k = _dg(parts, eaux_ref[h], _NN)
        aux_v = jnp.where(lane == a0, 1.0, 0.0)
        cols = slice(pair * LANES, (pair + 1) * LANES)
        qa_ref[h] = jnp.where(own, q_ref[:, cols] * (HEAD_DIM ** -0.5 * LOG2E), aux_q).astype(BF16)
        ka_ref[h] = jnp.where(own, k_ref[:, cols], aux_k).astype(BF16)
        va_ref[h] = jnp.where(own, v_ref[:, cols], aux_v).astype(BF16)


def _fox_prep(q, k, v, small, bf_pad, e_aux, n_heads, tb):
    bsz, t_len, d = q.shape
    kern = functools.partial(_fox_prep_kernel, n_heads=n_heads)
    row = lambda w: pl.BlockSpec((None, tb, w), lambda b, s: (b, s, 0))
    per_head = pl.BlockSpec((None, n_heads, tb, LANES), lambda b, s: (b, 0, s, 0))
    return pl.pallas_call(
        kern,
        grid=(bsz, t_len // tb),
        in_specs=[row(d), row(d), row(d), row(LANES),
                  pl.BlockSpec((1, LANES), lambda b, s: (0, 0)),
                  pl.BlockSpec((n_heads, N_AUX * LANES, LANES), lambda b, s: (0, 0, 0))],
        out_specs=[row(n_heads), per_head, per_head, per_head],
        out_shape=[jax.ShapeDtypeStruct((bsz, t_len, n_heads), F32)]
        + [jax.ShapeDtypeStruct((bsz, n_heads, t_len, LANES), BF16)] * 3,
        scratch_shapes=[pltpu.VMEM((1, LANES), F32)],
        compiler_params=_cparams(("parallel", "arbitrary")),
        name="fox_prep",
    )(q, k, v, small, bf_pad, e_aux)


def _fox_prompt_kernel(qa_ref, ka_ref, va_ref, bz_ref, o_ref, m_scr, acc_scr, *, blk):
    qi = pl.program_id(2)
    m_scr[...] = jnp.full_like(m_scr, NEG_BIG)
    acc_scr[...] = jnp.zeros_like(acc_scr)

    def block(ki, masked):
        k0 = pl.multiple_of(ki * blk, blk)
        for hh in range(2):
            s = _dg(qa_ref[hh], ka_ref[hh, pl.ds(k0, blk), :], _NT)
            if masked:
                s = jnp.where(_iota((blk, blk), 1) <= _iota((blk, blk), 0), s, NEG_BIG)
            m_prev = m_scr[hh]
            m_new = jnp.maximum(m_prev, jnp.max(s, axis=1, keepdims=True))
            alpha = jnp.exp2(m_prev - m_new)
            p = jnp.exp2(s - pltpu.repeat(m_new, blk // LANES, axis=1))
            acc_scr[hh] = alpha * acc_scr[hh] + _dg(p.astype(BF16), va_ref[hh, pl.ds(k0, blk), :], _NN)
            m_scr[hh] = m_new

    def below_diagonal(ki, carry):
        block(ki, False)
        return carry

    lax.fori_loop(0, qi, below_diagonal, 0)
    block(qi, True)
    lane = _iota((1, LANES), 1)
    acc0 = acc_scr[0]
    acc1 = acc_scr[1]
    o = jnp.where(lane < HEAD_DIM, acc0 / acc0[:, HEAD_DIM:HEAD_DIM + 1], acc1 / acc1[:, 0:1])
    o_ref[...] = o * _silu(bz_ref[...])


def _fox_prompt(qa, ka, va, b_z, blk):
    bsz, n_heads, t_len, _ = qa.shape
    d = b_z.shape[-1]
    n_pairs = n_heads // 2
    return pl.pallas_call(
        functools.partial(_fox_prompt_kernel, blk=blk),
        grid=(bsz, n_pairs, t_len // blk),
        in_specs=[pl.BlockSpec((None, 2, blk, LANES), lambda b, p, i: (b, p, i, 0)),
                  pl.BlockSpec((None, 2, t_len, LANES), lambda b, p, i: (b, p, 0, 0)),
                  pl.BlockSpec((None, 2, t_len, LANES), lambda b, p, i: (b, p, 0, 0)),
                  pl.BlockSpec((None, blk, LANES), lambda b, p, i: (b, i, p))],
        out_specs=pl.BlockSpec((None, blk, LANES), lambda b, p, i: (b, i, p)),
        out_shape=jax.ShapeDtypeStruct((bsz, t_len, d), F32),
        scratch_shapes=[pltpu.VMEM((2, blk, LANES), F32),
                        pltpu.VMEM((2, blk, LANES), F32)],
        compiler_params=_cparams(("parallel", "parallel", "arbitrary")),
        name="fox_prompt",
    )(qa, ka, va, b_z)


def _fox_cached_kernel(q_ref, k_ref, v_ref, ck_ref, cv_ref, clf_ref, ccol_ref, crow_ref, bz_ref, o_ref,
                       *, n_heads, lane_blk):
    t_len, d = q_ref.shape
    past = ck_ref.shape[0]
    n_rows = n_heads * t_len

    triu = jnp.where(_iota((lane_blk, lane_blk), 0) <= _iota((lane_blk, lane_blk), 1), 1.0, 0.0).astype(BF16)
    carry = jnp.zeros((n_heads, 1), F32)
    pieces = []
    for j in range(past // lane_blk):
        cj = _mm_exact_rhs(clf_ref[:, j * lane_blk:(j + 1) * lane_blk], triu) + carry
        carry = cj[:, lane_blk - 1:lane_blk]
        pieces.append(cj)
    c_cache = jnp.concatenate(pieces, axis=1)
    total = carry

    rows = lambda f: jnp.concatenate([f(h) for h in range(n_heads)], axis=0)
    ck_cache = rows(lambda h: jnp.broadcast_to(c_cache[h:h + 1, :], (t_len, past)))
    ck_new = rows(lambda h: jnp.broadcast_to(total[h:h + 1, :] + crow_ref[h:h + 1, :], (t_len, t_len)))
    cq = rows(lambda h: total[h:h + 1, :] + ccol_ref[:, h:h + 1])

    lane_head = _div_pow2(_iota((1, d), 1), HEAD_DIM)
    q = q_ref[...] * (HEAD_DIM ** -0.5)
    q_bd = rows(lambda h: jnp.where(lane_head == h, q, 0.0)).astype(BF16)
    s_c = _dg(q_bd, ck_ref[...].astype(BF16), _NT) + cq - ck_cache
    s_n = _dg(q_bd, k_ref[...].astype(BF16), _NT) + cq - ck_new
    q_pos = _iota((n_rows, t_len), 0) & (t_len - 1)
    s_n = jnp.where(_iota((n_rows, t_len), 1) <= q_pos, s_n, NEG_BIG)
    m = jnp.maximum(jnp.max(s_c, axis=1, keepdims=True), jnp.max(s_n, axis=1, keepdims=True))
    p_c = jnp.exp(s_c - m)
    p_n = jnp.exp(s_n - m)
    l = jnp.sum(p_c, axis=1, keepdims=True) + jnp.sum(p_n, axis=1, keepdims=True)
    o_all = (_dg(p_c.astype(BF16), cv_ref[...].astype(BF16), _NN)
             + _dg(p_n.astype(BF16), v_ref[...].astype(BF16), _NN)) / l
    o = jnp.where(lane_head == 0, o_all[0:t_len], 0.0)
    for h in range(1, n_heads):
        o = o + jnp.where(lane_head == h, o_all[h * t_len:(h + 1) * t_len], 0.0)
    o_ref[...] = o * _silu(bz_ref[...])


def _fox_cached(q, k, v, cache_k, cache_v, cache_logf_t, ccol, crow, b_z, layer):
    bsz, t_len, d = q.shape
    n_heads = d // HEAD_DIM
    past = cache_k.shape[2]
    kern = functools.partial(_fox_cached_kernel, n_heads=n_heads, lane_blk=min(past, 256))
    cur = lambda w: pl.BlockSpec((None, t_len, w), lambda b: (b, 0, 0))
    return pl.pallas_call(
        kern,
        grid=(bsz,),
        in_specs=[cur(d), cur(d), cur(d),
                  pl.BlockSpec((None, None, past, d), lambda b: (layer, b, 0, 0)),
                  pl.BlockSpec((None, None, past, d), lambda b: (layer, b, 0, 0)),
                  pl.BlockSpec((None, None, n_heads, past), lambda b: (layer, b, 0, 0)),
                  cur(LANES),
                  pl.BlockSpec((None, n_heads, t_len), lambda b: (b, 0, 0)),
                  cur(d)],
        out_specs=cur(d),
        out_shape=jax.ShapeDtypeStruct((bsz, t_len, d), F32),
        compiler_params=_cparams(("parallel",)),
        name="fox_cached",
    )(q, k, v, cache_k, cache_v, cache_logf_t, ccol, crow, b_z)


def _out_kernel(oa_ref, ob_ref, oc_ref, x_ref, w_ref, g_ref, b_ref, y_ref, *, alpha):
    da = oa_ref.shape[-1]
    db = ob_ref.shape[-1]
    h = jnp.dot(oa_ref[...].astype(BF16), w_ref[0:da, :], preferred_element_type=F32)
    h = h + jnp.dot(ob_ref[...].astype(BF16), w_ref[da:da + db, :], preferred_element_type=F32)
    h = h + jnp.dot(oc_ref[...].astype(BF16), w_ref[da + db:, :], preferred_element_type=F32)
    y = alpha * x_ref[...] + h
    mu = jnp.mean(y, axis=-1, keepdims=True)
    yc = y - mu
    var = jnp.mean(yc * yc, axis=-1, keepdims=True)
    y_ref[...] = yc * lax.rsqrt(var + LN_EPS) * g_ref[...] + b_ref[...]


def _out_project(o_a, o_b, o_c, x2, w_all, layer, g, b, alpha, tm):
    n, d = x2.shape
    d_mix = w_all.shape[1]
    row = lambda w: pl.BlockSpec((tm, w), lambda i: (i, 0))
    return pl.pallas_call(
        functools.partial(_out_kernel, alpha=alpha),
        grid=(n // tm,),
        in_specs=[row(o_a.shape[-1]), row(o_b.shape[-1]), row(o_c.shape[-1]), row(d),
                  pl.BlockSpec((None, d_mix, d), lambda i: (layer, 0, 0)),
                  pl.BlockSpec((1, d), lambda i: (0, 0)),
                  pl.BlockSpec((1, d), lambda i: (0, 0))],
        out_specs=row(d),
        out_shape=jax.ShapeDtypeStruct((n, d), F32),
        compiler_params=_cparams(("parallel",)),
        name="proj_out",
    )(o_a, o_b, o_c, x2, w_all, g.reshape(1, d), b.reshape(1, d))


def _block_diag_state(s):
    bsz, n_h, d0, d1 = s.shape
    eye = jnp.eye(n_h, dtype=s.dtype)
    return jnp.einsum('bhij,hg->bhigj', s, eye).reshape(bsz, n_h * d0, n_h * d1)


def _diag_blocks(s, n_h):
    bsz = s.shape[0]
    s5 = s.reshape(bsz, n_h, HEAD_DIM, n_h, HEAD_DIM)
    return jnp.stack([s5[:, h, :, h, :] for h in range(n_h)], axis=1)


def _expand_heads(p):
    return jnp.repeat(p.astype(F32), HEAD_DIM).reshape(1, -1)


def _select_matrix(rows, n_cols, first_row):
    e = np.zeros((LANES, n_cols), np.float32)
    for h in range(rows):
        e[first_row + h, h * HEAD_DIM:(h + 1) * HEAD_DIM] = 1.0
    return jnp.asarray(e, BF16)


def _row_tile(n):
    for tm in (256, 128, 64, 32, 16, 8):
        if n % tm == 0:
            return tm
    raise ValueError(f"row count {n} must be a multiple of 8")


def kernel(x_prompt, x_sample, cache_fox_k, cache_fox_v, cache_fox_logf, state_rwkv_shift, state_rwkv_wkv,
           state_gdn_conv, state_gdn_wkv, ln_in_g, ln_in_b, w_in, rwkv_mu, rwkv_w0, rwkv_w2, rwkv_a0, rwkv_a2,
           rwkv_k_k, rwkv_k_a, rwkv_r_k, rwkv_gn_g, rwkv_gn_b, fox_b_f, gdn_conv_w, gdn_a_log, gdn_dt_bias,
           gdn_norm_g, w_out, ln_post_g, ln_post_b):
    depth, d_model, _ = w_in.shape
    bp, seq, _ = x_prompt.shape
    bs, dec_seq, _ = x_sample.shape
    h_fox = fox_b_f.shape[1]
    h_gdn = gdn_a_log.shape[1]
    h_rwkv = rwkv_r_k.shape[1]
    d_rwkv, d_fox, d_gdn = h_rwkv * HEAD_DIM, h_fox * HEAD_DIM, h_gdn * HEAD_DIM
    rank_w = rwkv_w2.shape[1]
    rank_a = rwkv_a2.shape[1]
    n_shift = 3 * d_rwkv + rank_w + rank_a
    n_conv_cols = 3 * d_gdn
    n_conv = gdn_conv_w.shape[1]
    alpha = (2 * depth) ** 0.25
    assert rank_w + rank_a == LANES and h_fox + 2 * h_gdn <= LANES

    split = [n_shift, d_rwkv, d_fox, d_fox, d_fox, h_fox, d_fox, n_conv_cols, h_gdn, h_gdn, d_gdn]
    offs = np.concatenate([[0], np.cumsum(split)])
    seg = lambda i: w_in[:, :, offs[i]:offs[i + 1]]
    small_w = jnp.concatenate([seg(5), seg(8), seg(9)], axis=-1)
    small_w = jnp.pad(small_w, ((0, 0), (0, 0), (0, LANES - small_w.shape[-1])))
    w_in_p = jnp.concatenate([seg(0), seg(1), seg(2), seg(3), seg(4), seg(6), seg(7), seg(10), small_w],
                             axis=-1).astype(BF16)
    widths = [n_shift, d_rwkv, d_fox, d_fox, d_fox, d_fox, n_conv_cols, d_gdn, LANES]
    w_out_b = w_out.astype(BF16)

    zw = jnp.zeros((depth, rank_w, d_rwkv), F32)
    w2a2 = jnp.concatenate([jnp.concatenate([rwkv_w2, zw], axis=-1),
                            jnp.concatenate([zw, rwkv_a2], axis=-1)], axis=1).astype(BF16)
    bf_pad = jnp.pad(fox_b_f, ((0, 0), (0, LANES - h_fox)))
    e_beta = _select_matrix(h_gdn, d_gdn, h_fox)
    e_a = _select_matrix(h_gdn, d_gdn, h_fox + h_gdn)
    e_aux_np = np.zeros((h_fox, N_AUX * LANES, LANES), np.float32)
    for h in range(h_fox):
        for j in range(N_AUX):
            e_aux_np[h, j * LANES + h, (HEAD_DIM if h % 2 == 0 else 0) + j] = -1.0
    e_aux = jnp.asarray(e_aux_np, BF16)
    cache_logf_t = jnp.swapaxes(cache_fox_logf, 2, 3)

    def layer_params(l):
        row = lambda a: a[l].reshape(1, -1).astype(F32)
        return dict(
            rwkv=dict(mu=row(rwkv_mu), w0=row(rwkv_w0), a0=row(rwkv_a0), w2a2=w2a2[l], k_k=row(rwkv_k_k),
                      k_a=row(rwkv_k_a), r_k=row(rwkv_r_k), gn_g=row(rwkv_gn_g), gn_b=row(rwkv_gn_b)),
            gdn=dict(conv_w=gdn_conv_w[l], a_log=_expand_heads(gdn_a_log[l]),
                     dt_bias=_expand_heads(gdn_dt_bias[l]),
                     norm_g=jnp.tile(gdn_norm_g[l].reshape(1, -1), (1, h_gdn)), e_beta=e_beta, e_a=e_a),
        )

    def run_layer(l, x2, bsz, t_len, prev0, s_rwkv0, conv0, s_gdn0, cached):
        lp = layer_params(l)
        tm = _row_tile(x2.shape[0])
        a_sh, a_z, b_q, b_k, b_v, b_z, c_qkv, c_z, small = _project(x2, w_in_p, l, widths, tm)
        r3 = lambda a: a.reshape(bsz, t_len, a.shape[-1])
        a_sh, a_z, b_q, b_k, b_v, b_z, c_qkv, c_z, small = map(r3, (a_sh, a_z, b_q, b_k, b_v, b_z, c_qkv, c_z, small))
        c = min(HEAD_DIM, t_len)
        gb = N_STACK // (h_rwkv * c)
        n_chains = max(1, min(N_CHAINS, bsz // gb))
        o_a, s_rwkv = _rwkv_mix(a_sh, a_z, prev0, s_rwkv0, lp["rwkv"], gb, n_chains, c, CHUNK_PASSES)
        gb = N_STACK // (h_gdn * c)
        o_c, s_gdn = _gdn_mix(c_qkv, small, c_z, conv0, s_gdn0, lp["gdn"], gb, n_chains, c, CHUNK_PASSES)
        if cached:
            logf, ccol, crow = _fox_gates(small, bf_pad[l:l + 1], h_fox, min(t_len, 256))
            o_b = _fox_cached(b_q, b_k, b_v, cache_fox_k.reshape(depth, bsz, -1, d_fox),
                              cache_fox_v.reshape(depth, bsz, -1, d_fox), cache_logf_t, ccol, crow, b_z, l)
        else:
            logf, qa, ka, va = _fox_prep(b_q, b_k, b_v, small, bf_pad[l:l + 1], e_aux, h_fox, min(t_len, 256))
            o_b = _fox_prompt(qa, ka, va, b_z, min(t_len, FOX_BLOCK))
        f2 = lambda a: a.reshape(bsz * t_len, a.shape[-1])
        x_new = _out_project(f2(o_a), f2(o_b), f2(o_c), x2, w_out_b, l, ln_post_g[l], ln_post_b[l], alpha, tm)
        outs = (b_k.reshape(bsz, t_len, h_fox, HEAD_DIM), b_v.reshape(bsz, t_len, h_fox, HEAD_DIM), logf,
                a_sh[:, -1], _diag_blocks(s_rwkv, h_rwkv), c_qkv[:, t_len - (n_conv - 1):],
                _diag_blocks(s_gdn, h_gdn))
        return x_new, outs

    xp = _layer_norm(x_prompt.reshape(bp * seq, d_model), ln_in_g, ln_in_b, _row_tile(bp * seq))
    xs = _layer_norm(x_sample.reshape(bs * dec_seq, d_model), ln_in_g, ln_in_b, _row_tile(bs * dec_seq))
    zeros_p = (jnp.zeros((bp, 1, n_shift), F32), jnp.zeros((bp, d_rwkv, d_rwkv), F32),
               jnp.zeros((bp, n_conv - 1, n_conv_cols), F32), jnp.zeros((bp, d_gdn, d_gdn), F32))
    outs_p, outs_s = [], []
    for l in range(depth):
        xp, o = run_layer(l, xp, bp, seq, *zeros_p, False)
        outs_p.append(o)
        xs, o = run_layer(l, xs, bs, dec_seq, state_rwkv_shift[l][:, None, :],
                          _block_diag_state(state_rwkv_wkv[l]), state_gdn_conv[l],
                          _block_diag_state(state_gdn_wkv[l]), True)
        outs_s.append(o)
    stack = lambda outs, i: jnp.stack([o[i] for o in outs])
    return ((xp.reshape(bp, seq, d_model), xs.reshape(bs, dec_seq, d_model))
            + tuple(stack(outs_p, i) for i in range(7)) + tuple(stack(outs_s, i) for i in range(7)))
```

```python
import functools
import math

import jax
import jax.numpy as jnp
import numpy as np
from jax import lax
from jax.experimental import pallas as pl
from jax.experimental.pallas import tpu as pltpu

F32 = jnp.float32
BF16 = jnp.bfloat16

HEAD_DIM = 64
LANES = 128
N_STACK = 256
PROJ_ROWS = 512
FOX_BLOCK = 512
N_CHAINS = 4
CHUNK_PASSES = 1
LN_EPS = 1e-5
RWKV_GN_EPS = 64e-5
GDN_NORM_EPS = 1e-6
L2_EPS = 1e-6
NEG_BIG = -1e30
LOG2E = math.log2(math.e)
VMEM_LIMIT = 56 * 1024 * 1024


def _sigmoid(x):
    return 1.0 / (1.0 + jnp.exp(-x))


def _softplus(x):
    return jnp.maximum(x, 0.0) + jnp.log(1.0 + jnp.exp(-jnp.abs(x)))


def _silu(x):
    return x * _sigmoid(x)


def _split3(x):
    hi = x.astype(BF16)
    r1 = x - hi.astype(F32)
    mid = r1.astype(BF16)
    lo = (r1 - mid.astype(F32)).astype(BF16)
    return hi, mid, lo


_NN = (((1,), (0,)), ((), ()))
_NT = (((1,), (1,)), ((), ()))
_TN = (((0,), (0,)), ((), ()))


def _dg(a, b, dims):
    return lax.dot_general(a, b, dims, preferred_element_type=F32)


def _mm(a, b, dims=_NN, passes=1):
    if passes == 1:
        return _dg(a.astype(BF16), b.astype(BF16), dims)
    ah, am, _ = _split3(a)
    bh, bm, _ = _split3(b)
    return _dg(ah, bh, dims) + (_dg(ah, bm, dims) + _dg(am, bh, dims))


def _mm_exact_rhs(a, e, dims=_NN):
    hi, mid, lo = _split3(a)
    return _dg(hi, e, dims) + (_dg(mid, e, dims) + _dg(lo, e, dims))


def _mm_exact_lhs(e, b, dims=_NN):
    hi, mid, lo = _split3(b)
    return _dg(e, hi, dims) + (_dg(e, mid, dims) + _dg(e, lo, dims))


def _iota(shape, dim):
    return lax.broadcasted_iota(jnp.int32, shape, dim)


def _div_pow2(x, n):
    assert n & (n - 1) == 0
    return lax.shift_right_logical(x, jnp.int32(int(math.log2(n))))


def _head_block_ones(n):
    r = _div_pow2(_iota((n, n), 0), HEAD_DIM)
    c = _div_pow2(_iota((n, n), 1), HEAD_DIM)
    return jnp.where(r == c, 1.0, 0.0).astype(BF16)


def _stack_heads(x, gb, c, n_heads):
    width = x.shape[1]
    lane_head = _div_pow2(_iota((1, width), 1), HEAD_DIM)
    parts = []
    for b in range(gb):
        xb = x[b * c:(b + 1) * c, :]
        for h in range(n_heads):
            parts.append(jnp.where(lane_head == h, xb, 0.0))
    return jnp.concatenate(parts, axis=0)


def _unstack_heads(xs, gb, c, n_heads):
    outs = []
    for b in range(gb):
        acc = xs[(b * n_heads) * c:(b * n_heads + 1) * c, :]
        for h in range(1, n_heads):
            acc = acc + xs[(b * n_heads + h) * c:(b * n_heads + h + 1) * c, :]
        outs.append(acc)
    return jnp.concatenate(outs, axis=0) if gb > 1 else outs[0]


def _unit_lower_inverse(p, c, passes):
    n = p.shape[0]
    eye = jnp.where(_iota((n, n), 0) == _iota((n, n), 1), 1.0, 0.0)
    t = eye + p
    n_rounds = int(math.log2(c)) - 1
    pk = _mm(p, p, passes=passes)
    yield
    for i in range(n_rounds):
        t_next = t + _mm(t, pk, passes=passes)
        if i + 1 < n_rounds:
            pk = _mm(pk, pk, passes=passes)
        t = t_next
        yield
    return t


def _run_interleaved(chains):
    results = [None] * len(chains)
    active = list(enumerate(chains))
    while active:
        still = []
        for i, ch in active:
            try:
                next(ch)
                still.append((i, ch))
            except StopIteration as stop:
                results[i] = stop.value
        active = still
    return results


def _chunk_masks(n, c):
    row = _iota((n, n), 0)
    col = _iota((n, n), 1)
    same = _div_pow2(row, c) == _div_pow2(col, c)
    strict = jnp.logical_and(same, row > col)
    incl = jnp.logical_and(same, row >= col)
    return strict, incl


def _cumsum_rows(x, gb, c):
    n = gb * c
    strict, incl = _chunk_masks(n, c)
    del strict
    tri = jnp.where(incl, 1.0, 0.0).astype(BF16)
    return _mm_exact_lhs(tri, x)


def _shift_rows(x, j, prev_rows, gb, c):
    n_prev = prev_rows[0].shape[0]
    out = pltpu.roll(x, j, 0)
    rid = _iota((x.shape[0], 1), 0)
    for b in range(gb):
        for t in range(j):
            src = n_prev + t - j
            out = jnp.where(rid == b * c + t, prev_rows[b][src:src + 1, :], out)
    return out


def _cparams(sem):
    return pltpu.CompilerParams(dimension_semantics=sem, vmem_limit_bytes=VMEM_LIMIT)


def _ln_kernel(x_ref, g_ref, b_ref, o_ref):
    x = x_ref[...]
    mu = jnp.mean(x, axis=-1, keepdims=True)
    xc = x - mu
    var = jnp.mean(xc * xc, axis=-1, keepdims=True)
    o_ref[...] = xc * lax.rsqrt(var + LN_EPS) * g_ref[...] + b_ref[...]


def _layer_norm(x2, g, b, tm):
    n, d = x2.shape
    return pl.pallas_call(
        _ln_kernel,
        grid=(n // tm,),
        in_specs=[pl.BlockSpec((tm, d), lambda i: (i, 0)),
                  pl.BlockSpec((1, d), lambda i: (0, 0)),
                  pl.BlockSpec((1, d), lambda i: (0, 0))],
        out_specs=pl.BlockSpec((tm, d), lambda i: (i, 0)),
        out_shape=jax.ShapeDtypeStruct((n, d), F32),
        compiler_params=_cparams(("parallel",)),
        name="ln_in",
    )(x2, g.reshape(1, d), b.reshape(1, d))


def _proj_kernel(x_ref, w_ref, *o_refs):
    xb = x_ref[...].astype(BF16)
    off = 0
    for o_ref in o_refs:
        wdt = o_ref.shape[-1]
        o_ref[...] = jnp.dot(xb, w_ref[:, off:off + wdt], preferred_element_type=F32)
        off += wdt


def _project(x2, w_all, layer, widths, tm):
    n, d = x2.shape
    n_cols = w_all.shape[-1]
    return pl.pallas_call(
        _proj_kernel,
        grid=(n // tm,),
        in_specs=[pl.BlockSpec((tm, d), lambda i: (i, 0)),
                  pl.BlockSpec((None, d, n_cols), lambda i: (layer, 0, 0))],
        out_specs=[pl.BlockSpec((tm, w), lambda i: (i, 0)) for w in widths],
        out_shape=[jax.ShapeDtypeStruct((n, w), F32) for w in widths],
        compiler_params=_cparams(("parallel",)),
        name="proj_in",
    )(x2, w_all)


def _rwkv_kernel(ash_ref, az_ref, prev0_ref, s0_ref, mu_ref, w0_ref, a0_ref, w2a2_ref, kk_ref, ka_ref,
                 rk_ref, gng_ref, gnb_ref, o_ref, s_ref, prev_scr, *, gb, n_chains, c, n_heads, passes):
    @pl.when(pl.program_id(1) == 0)
    def _():
        s_ref[...] = s0_ref[...]
        prev_scr[...] = prev0_ref[...]

    prm = dict(mu=mu_ref[...], w0=w0_ref[...], a0=a0_ref[...], w2a2=w2a2_ref[...], k_k=kk_ref[...],
               k_a=ka_ref[...], r_k=rk_ref[...], gn_g=gng_ref[...], gn_b=gnb_ref[...])
    ins = []
    for ch in range(n_chains):
        b0 = ch * gb
        ins.append((ash_ref[b0:b0 + gb].reshape(gb * c, ash_ref.shape[-1]),
                    az_ref[b0:b0 + gb].reshape(gb * c, az_ref.shape[-1]),
                    [prev_scr[b0 + b] for b in range(gb)],
                    [s_ref[b0 + b] for b in range(gb)]))
    outs = _run_interleaved([_rwkv_chain(*args, prm, gb=gb, c=c, n_heads=n_heads, passes=passes) for args in ins])
    for ch, (o, s_new, last_rows) in enumerate(outs):
        b0 = ch * gb
        o_ref[b0:b0 + gb] = o.reshape(gb, c, o.shape[-1])
        for b in range(gb):
            s_ref[b0 + b] = s_new[b]
            prev_scr[b0 + b] = last_rows[b]


def _rwkv_chain(p, z, prev_rows, s_old, prm, *, gb, c, n_heads, passes):
    d = n_heads * HEAD_DIM
    prev = _shift_rows(p, 1, prev_rows, gb, c)
    last_rows = [p[(b + 1) * c - 1:(b + 1) * c, :] for b in range(gb)]
    xs = p + (prev - p) * prm["mu"]
    r = xs[:, 0:d]
    k = xs[:, d:2 * d]
    v = xs[:, 2 * d:3 * d]
    wa = xs[:, 3 * d:3 * d + LANES]
    lane = _iota((1, LANES), 1)
    wa = jnp.where(lane < HEAD_DIM, jnp.tanh(wa), wa)
    pre = _mm(wa, prm["w2a2"])
    yield
    w_ll = -_softplus(-(prm["w0"] + pre[:, 0:d])) - 0.5
    lw = -jnp.exp(w_ll)
    a = _sigmoid(prm["a0"] + pre[:, d:2 * d])

    ones_bd = _head_block_ones(d)
    kk = k * prm["k_k"]
    kk = kk / jnp.maximum(jnp.sqrt(_mm_exact_rhs(kk * kk, ones_bd)), 1e-12)
    kmod = k * (1.0 + (a - 1.0) * prm["k_a"])
    kka = kk * a

    cum = _cumsum_rows(lw, gb, c)
    yield
    cum_last = jnp.concatenate(
        [jnp.broadcast_to(cum[(b + 1) * c - 1:(b + 1) * c, :], (c, d)) for b in range(gb)], axis=0)
    inv_p = jnp.exp(-cum)
    to_end = jnp.exp(cum_last - cum)
    b_t = kk * jnp.exp(cum - lw)
    a_t = -kka * inv_p
    k_t = kmod * inv_p
    r_t = r * jnp.exp(cum)

    st = functools.partial(_stack_heads, gb=gb, c=c, n_heads=n_heads)
    bs, rs, as_, ks, vs = st(b_t), st(r_t), st(a_t), st(k_t), st(v)
    n = bs.shape[0]
    g = _mm(jnp.concatenate([bs, rs], axis=0), jnp.concatenate([as_, ks], axis=0), _NT, passes)
    yield
    strict, incl = _chunk_masks(n, c)
    m_ab = jnp.where(strict, g[0:n, 0:n], 0.0)
    m_bk = jnp.where(strict, g[0:n, n:2 * n], 0.0)
    m_ra = jnp.where(incl, g[n:2 * n, 0:n], 0.0)
    m_rk = jnp.where(incl, g[n:2 * n, n:2 * n], 0.0)

    hc = n_heads * c
    bh = jnp.concatenate([_mm(bs[b * hc:(b + 1) * hc], s_old[b], _NT, passes) for b in range(gb)], axis=0)
    rh = jnp.concatenate([_mm(rs[b * hc:(b + 1) * hc], s_old[b], _NT, passes) for b in range(gb)], axis=0)
    rhs_u = bh + _mm(m_bk, vs, passes=passes)
    o_s = rh + _mm(m_rk, vs, passes=passes)
    t_inv = yield from _unit_lower_inverse(m_ab, c, passes)
    u = _mm(t_inv, rhs_u, passes=passes)
    yield
    o_s = o_s + _mm(m_ra, u, passes=passes)
    o = _unstack_heads(o_s, gb, c, n_heads)

    a_end = st(-kka * to_end)
    k_end = st(kmod * to_end)
    s_new = []
    for b in range(gb):
        sl = slice(b * hc, (b + 1) * hc)
        p_end = jnp.exp(cum[(b + 1) * c - 1:(b + 1) * c, :])
        s_new.append(s_old[b] * p_end + _mm(u[sl], a_end[sl], _TN, passes)
                     + _mm(vs[sl], k_end[sl], _TN, passes))
    yield

    mean = _mm_exact_rhs(o, ones_bd) * (1.0 / HEAD_DIM)
    oc = o - mean
    yield
    var = _mm_exact_rhs(oc * oc, ones_bd) * (1.0 / HEAD_DIM)
    o = oc * lax.rsqrt(var + RWKV_GN_EPS) * prm["gn_g"] + prm["gn_b"]
    bonus = _mm_exact_rhs(r * kmod * prm["r_k"], ones_bd) * v
    return (o + bonus) * _silu(z), s_new, last_rows


def _rwkv_mix(a_sh, a_z, prev0, s0, lp, gb, n_chains, c, passes):
    bsz, t_len, n_shift = a_sh.shape
    d = a_z.shape[-1]
    n_heads = d // HEAD_DIM
    kern = functools.partial(_rwkv_kernel, gb=gb, n_chains=n_chains, c=c, n_heads=n_heads, passes=passes)
    gb = gb * n_chains
    vec = lambda w: pl.BlockSpec((1, w), lambda b, s: (0, 0))
    return pl.pallas_call(
        kern,
        grid=(bsz // gb, t_len // c),
        in_specs=[pl.BlockSpec((gb, c, n_shift), lambda b, s: (b, s, 0)),
                  pl.BlockSpec((gb, c, d), lambda b, s: (b, s, 0)),
                  pl.BlockSpec((gb, 1, n_shift), lambda b, s: (b, 0, 0)),
                  pl.BlockSpec((gb, d, d), lambda b, s: (b, 0, 0)),
                  vec(n_shift), vec(d), vec(d),
                  pl.BlockSpec((LANES, 2 * d), lambda b, s: (0, 0)),
                  vec(d), vec(d), vec(d), vec(d), vec(d)],
        out_specs=[pl.BlockSpec((gb, c, d), lambda b, s: (b, s, 0)),
                   pl.BlockSpec((gb, d, d), lambda b, s: (b, 0, 0))],
        out_shape=[jax.ShapeDtypeStruct((bsz, t_len, d), F32),
                   jax.ShapeDtypeStruct((bsz, d, d), F32)],
        scratch_shapes=[pltpu.VMEM((gb, 1, n_shift), F32)],
        compiler_params=_cparams(("parallel", "arbitrary")),
        name="rwkv_mix",
    )(a_sh, a_z, prev0, s0, lp["mu"], lp["w0"], lp["a0"], lp["w2a2"], lp["k_k"], lp["k_a"], lp["r_k"],
      lp["gn_g"], lp["gn_b"])


def _gdn_kernel(qkv_ref, small_ref, cz_ref, conv0_ref, s0_ref, convw_ref, alog_ref, dtb_ref, ng_ref,
                eb_ref, ea_ref, o_ref, s_ref, conv_scr, *, gb, n_chains, c, n_heads, passes):
    @pl.when(pl.program_id(1) == 0)
    def _():
        s_ref[...] = s0_ref[...]
        conv_scr[...] = conv0_ref[...]

    prm = dict(conv_w=convw_ref[...], a_log=alog_ref[...], dt_bias=dtb_ref[...], norm_g=ng_ref[...],
               e_beta=eb_ref[...], e_a=ea_ref[...])
    ins = []
    for ch in range(n_chains):
        b0 = ch * gb
        ins.append((qkv_ref[b0:b0 + gb].reshape(gb * c, qkv_ref.shape[-1]),
                    small_ref[b0:b0 + gb].reshape(gb * c, LANES),
                    cz_ref[b0:b0 + gb].reshape(gb * c, cz_ref.shape[-1]),
                    [conv_scr[b0 + b] for b in range(gb)],
                    [s_ref[b0 + b] for b in range(gb)]))
    outs = _run_interleaved([_gdn_chain(*args, prm, gb=gb, c=c, n_heads=n_heads, passes=passes) for args in ins])
    for ch, (o, s_new, last_rows) in enumerate(outs):
        b0 = ch * gb
        o_ref[b0:b0 + gb] = o.reshape(gb, c, o.shape[-1])
        for b in range(gb):
            s_ref[b0 + b] = s_new[b]
            conv_scr[b0 + b] = last_rows[b]


def _gdn_chain(x, small, z, prev_rows, s_old, prm, *, gb, c, n_heads, passes):
    d = n_heads * HEAD_DIM
    conv_w = prm["conv_w"]
    n_conv = conv_w.shape[0]
    y = x * conv_w[n_conv - 1:n_conv, :]
    for j in range(1, n_conv):
        y = y + _shift_rows(x, j, prev_rows, gb, c) * conv_w[n_conv - 1 - j:n_conv - j, :]
    last_rows = [x[(b + 1) * c - (n_conv - 1):(b + 1) * c, :] for b in range(gb)]
    y = _silu(y)
    q = y[:, 0:d]
    k = y[:, d:2 * d]
    v = y[:, 2 * d:3 * d]
    ones_bd = _head_block_ones(d)
    q = q * lax.rsqrt(_mm_exact_rhs(q * q, ones_bd) + L2_EPS) * (HEAD_DIM ** -0.5)
    k = k * lax.rsqrt(_mm_exact_rhs(k * k, ones_bd) + L2_EPS)

    beta = _sigmoid(_mm_exact_rhs(small, prm["e_beta"]))
    g = -jnp.exp(prm["a_log"]) * _softplus(_mm_exact_rhs(small, prm["e_a"]) + prm["dt_bias"])
    yield
    gc = _cumsum_rows(g, gb, c)
    yield
    gc_last = jnp.concatenate(
        [jnp.broadcast_to(gc[(b + 1) * c - 1:(b + 1) * c, :], (c, d)) for b in range(gb)], axis=0)
    egc = jnp.exp(gc)

    st = functools.partial(_stack_heads, gb=gb, c=c, n_heads=n_heads)
    kb = k * beta
    kbs, ks, qs = st(kb), st(k), st(q)
    n = kbs.shape[0]
    strict, incl = _chunk_masks(n, c)
    gcol = jnp.min(st(gc), axis=1, keepdims=True)
    eye = _iota((n, n), 0) == _iota((n, n), 1)
    grow = jnp.sum(jnp.where(eye, jnp.broadcast_to(gcol, (n, n)), 0.0), axis=0, keepdims=True)
    dmat = jnp.exp(jnp.where(incl, gcol - grow, 0.0))
    g2 = _mm(jnp.concatenate([kbs, qs], axis=0), ks, _NT, passes)
    yield
    m = jnp.where(strict, g2[0:n] * dmat, 0.0)
    qk = jnp.where(incl, g2[n:2 * n] * dmat, 0.0)
    t_inv = yield from _unit_lower_inverse(-m, c, passes)
    rhs = jnp.concatenate([st(v * beta), st(kb * egc)], axis=1)
    sol = _mm(t_inv, rhs, passes=passes)
    yield
    u = sol[:, 0:d]
    w = sol[:, d:2 * d]

    hc = n_heads * c
    qgs = st(q * egc)
    kds = st(k * jnp.exp(gc_last - gc))
    v_new = u - jnp.concatenate(
        [_mm(w[b * hc:(b + 1) * hc], s_old[b], passes=passes) for b in range(gb)], axis=0)
    o_s = jnp.concatenate(
        [_mm(qgs[b * hc:(b + 1) * hc], s_old[b], passes=passes) for b in range(gb)], axis=0)
    yield
    o_s = o_s + _mm(qk, v_new, passes=passes)
    s_new = []
    for b in range(gb):
        sl = slice(b * hc, (b + 1) * hc)
        gl = jnp.exp(gc[(b + 1) * c - 1:(b + 1) * c, :])
        s_new.append(s_old[b] * gl + _mm(kds[sl], v_new[sl], _TN, passes))
    yield

    o = _unstack_heads(o_s, gb, c, n_heads)
    ms = _mm_exact_rhs(o * o, ones_bd) * (1.0 / HEAD_DIM)
    o = o * lax.rsqrt(ms + GDN_NORM_EPS) * prm["norm_g"]
    return o * _silu(z), s_new, last_rows


def _gdn_mix(c_qkv, small, c_z, conv0, s0, lp, gb, n_chains, c, passes):
    bsz, t_len, n_qkv = c_qkv.shape
    d = c_z.shape[-1]
    n_heads = d // HEAD_DIM
    n_conv = lp["conv_w"].shape[0]
    kern = functools.partial(_gdn_kernel, gb=gb, n_chains=n_chains, c=c, n_heads=n_heads, passes=passes)
    gb = gb * n_chains
    vec = lambda w: pl.BlockSpec((1, w), lambda b, s: (0, 0))
    return pl.pallas_call(
        kern,
        grid=(bsz // gb, t_len // c),
        in_specs=[pl.BlockSpec((gb, c, n_qkv), lambda b, s: (b, s, 0)),
                  pl.BlockSpec((gb, c, LANES), lambda b, s: (b, s, 0)),
                  pl.BlockSpec((gb, c, d), lambda b, s: (b, s, 0)),
                  pl.BlockSpec((gb, n_conv - 1, n_qkv), lambda b, s: (b, 0, 0)),
                  pl.BlockSpec((gb, d, d), lambda b, s: (b, 0, 0)),
                  pl.BlockSpec((n_conv, n_qkv), lambda b, s: (0, 0)),
                  vec(d), vec(d), vec(d),
                  pl.BlockSpec((LANES, d), lambda b, s: (0, 0)),
                  pl.BlockSpec((LANES, d), lambda b, s: (0, 0))],
        out_specs=[pl.BlockSpec((gb, c, d), lambda b, s: (b, s, 0)),
                   pl.BlockSpec((gb, d, d), lambda b, s: (b, 0, 0))],
        out_shape=[jax.ShapeDtypeStruct((bsz, t_len, d), F32),
                   jax.ShapeDtypeStruct((bsz, d, d), F32)],
        scratch_shapes=[pltpu.VMEM((gb, n_conv - 1, n_qkv), F32)],
        compiler_params=_cparams(("parallel", "arbitrary")),
        name="gdn_mix",
    )(c_qkv, small, c_z, conv0, s0, lp["conv_w"], lp["a_log"], lp["dt_bias"], lp["norm_g"],
      lp["e_beta"], lp["e_a"])


def _log_forget_cumsum(small, bf, carry_scr):
    @pl.when(pl.program_id(1) == 0)
    def _():
        carry_scr[...] = jnp.zeros_like(carry_scr)

    logf = -_softplus(-(small + bf))
    tb = logf.shape[0]
    c = _cumsum_rows(logf, 1, tb) + carry_scr[...]
    carry_scr[...] = c[tb - 1:tb, :]
    return logf, c


def _fox_gate_kernel(small_ref, bf_ref, logf_ref, ccol_ref, crow_ref, carry_scr, *, n_heads):
    logf, c = _log_forget_cumsum(small_ref[...], bf_ref[...], carry_scr)
    logf_ref[...] = logf[:, 0:n_heads]
    ccol_ref[...] = c
    sel = jnp.where(_iota((n_heads, LANES), 0) == _iota((n_heads, LANES), 1), 1.0, 0.0).astype(BF16)
    crow_ref[...] = _mm_exact_lhs(sel, c, _NT)


def _fox_gates(small, bf_pad, n_heads, tb):
    bsz, t_len, _ = small.shape
    kern = functools.partial(_fox_gate_kernel, n_heads=n_heads)
    return pl.pallas_call(
        kern,
        grid=(bsz, t_len // tb),
        in_specs=[pl.BlockSpec((None, tb, LANES), lambda b, s: (b, s, 0)),
                  pl.BlockSpec((1, LANES), lambda b, s: (0, 0))],
        out_specs=[pl.BlockSpec((None, tb, n_heads), lambda b, s: (b, s, 0)),
                   pl.BlockSpec((None, tb, LANES), lambda b, s: (b, s, 0)),
                   pl.BlockSpec((None, n_heads, tb), lambda b, s: (b, 0, s))],
        out_shape=[jax.ShapeDtypeStruct((bsz, t_len, n_heads), F32),
                   jax.ShapeDtypeStruct((bsz, t_len, LANES), F32),
                   jax.ShapeDtypeStruct((bsz, n_heads, t_len), F32)],
        scratch_shapes=[pltpu.VMEM((1, LANES), F32)],
        compiler_params=_cparams(("parallel", "arbitrary")),
        name="fox_gates",
    )(small, bf_pad)


N_AUX = 3


def _proj_prompt_kernel(*refs, widths, n_heads, ln_in, n_alias):
    refs = list(refs)
    x_ref, w_ref, bf_ref, eaux_ref = refs[0:4]
    pos = 4
    if ln_in:
        g_ref, b_ref = refs[pos:pos + 2]
        pos += 2
    pos += n_alias
    (ash_ref, az_ref, bz_ref, cqkv_ref, cz_ref, small_ref, k_ref, v_ref, logf_ref,
     qa_ref, ka_ref, va_ref) = refs[pos:pos + 12]
    pos += 12
    if ln_in:
        xn_ref = refs[pos]
        pos += 1
    carry_scr = refs[pos]

    x = x_ref[...]
    if ln_in:
        mu = jnp.mean(x, axis=-1, keepdims=True)
        xc = x - mu
        var = jnp.mean(xc * xc, axis=-1, keepdims=True)
        x = xc * lax.rsqrt(var + LN_EPS) * g_ref[...] + b_ref[...]
        xn_ref[...] = x
    xb = x.astype(BF16)
    offs = np.concatenate([[0], np.cumsum(widths)])
    seg = lambda i: jnp.dot(xb, w_ref[:, int(offs[i]):int(offs[i + 1])], preferred_element_type=F32)
    ash_ref[...] = seg(0)
    az_ref[...] = seg(1)
    q, k, v = seg(2), seg(3), seg(4)
    bz_ref[...] = seg(5)
    cqkv_ref[...] = seg(6)
    cz_ref[...] = seg(7)
    small = seg(8)
    small_ref[...] = small
    k_ref[...] = k
    v_ref[...] = v

    logf, c = _log_forget_cumsum(small, bf_ref[...], carry_scr)
    logf_ref[...] = logf[:, 0:n_heads]
    parts = jnp.concatenate(_split3(c * LOG2E), axis=1)
    lane = _iota((1, LANES), 1)
    for h in range(n_heads):
        pair, hh = divmod(h, 2)
        own = (lane < HEAD_DIM) if hh == 0 else (lane >= HEAD_DIM)
        a0 = HEAD_DIM if hh == 0 else 0
        aux_q = jnp.where((lane >= a0) & (lane < a0 + N_AUX), 1.0, 0.0)
        aux_k = _dg(parts, eaux_ref[h], _NN)
        aux_v = jnp.where(lane == a0, 1.0, 0.0)
        cols = slice(pair * LANES, (pair + 1) * LANES)
        qa_ref[h] = jnp.where(own, q[:, cols] * (HEAD_DIM ** -0.5 * LOG2E), aux_q).astype(BF16)
        ka_ref[h] = jnp.where(own, k[:, cols], aux_k).astype(BF16)
        va_ref[h] = jnp.where(own, v[:, cols], aux_v).astype(BF16)


def _project_prompt(x3, w_all, layer, widths, bf_pad, e_aux, n_heads, tm, ln_params, stacked):
    bsz, t_len, d = x3.shape
    depth, _, n_cols = w_all.shape
    d_fox = n_heads * HEAD_DIM
    ln_in = ln_params is not None
    n_alias = 0 if stacked is None else len(stacked)
    kern = functools.partial(_proj_prompt_kernel, widths=tuple(widths), n_heads=n_heads, ln_in=ln_in,
                             n_alias=n_alias)
    row = lambda w: pl.BlockSpec((None, tm, w), lambda b, s: (b, s, 0))
    per_layer = lambda w: pl.BlockSpec((None, None, tm, w), lambda b, s: (layer, b, s, 0))
    per_head = pl.BlockSpec((None, n_heads, tm, LANES), lambda b, s: (b, 0, s, 0))
    const = lambda shape: pl.BlockSpec(shape, lambda b, s: (0,) * len(shape))
    in_specs = [row(d), pl.BlockSpec((None, d, n_cols), lambda b, s: (layer, 0, 0), pipeline_mode=pl.Buffered(1)),
                const((1, LANES)), const((n_heads, N_AUX * LANES, LANES))]
    args = [x3, w_all, bf_pad, e_aux]
    if ln_in:
        in_specs += [const((1, d)), const((1, d))]
        args += [p.reshape(1, d) for p in ln_params]
    aliases = {}
    if stacked is not None:
        for j, buf in enumerate(stacked):
            aliases[len(args)] = 6 + j
            in_specs.append(pl.BlockSpec(memory_space=pl.ANY))
            args.append(buf)
    f32 = lambda *shape: jax.ShapeDtypeStruct(shape, F32)
    out_specs = [row(widths[0]), row(widths[1]), row(widths[5]), row(widths[6]), row(widths[7]), row(LANES),
                 per_layer(d_fox), per_layer(d_fox), per_layer(n_heads), per_head, per_head, per_head]
    out_shape = [f32(bsz, t_len, widths[0]), f32(bsz, t_len, widths[1]), f32(bsz, t_len, widths[5]),
                 f32(bsz, t_len, widths[6]), f32(bsz, t_len, widths[7]), f32(bsz, t_len, LANES),
                 f32(depth, bsz, t_len, d_fox), f32(depth, bsz, t_len, d_fox), f32(depth, bsz, t_len, n_heads)]
    out_shape += [jax.ShapeDtypeStruct((bsz, n_heads, t_len, LANES), BF16)] * 3
    if ln_in:
        out_specs.append(row(d))
        out_shape.append(f32(bsz, t_len, d))
    return pl.pallas_call(
        kern,
        grid=(bsz, t_len // tm),
        in_specs=in_specs,
        out_specs=out_specs,
        out_shape=out_shape,
        input_output_aliases=aliases,
        scratch_shapes=[pltpu.VMEM((1, LANES), F32)],
        compiler_params=_cparams(("parallel", "arbitrary")),
        name="proj_prompt",
    )(*args)


def _fox_prompt_kernel(qa_ref, ka_ref, va_ref, bz_ref, o_ref, m_scr, acc_scr, *, blk):
    qi = pl.program_id(2)
    m_scr[...] = jnp.full_like(m_scr, NEG_BIG)
    acc_scr[...] = jnp.zeros_like(acc_scr)

    def block(ki, masked):
        k0 = pl.multiple_of(ki * blk, blk)
        logits = [_dg(qa_ref[hh], ka_ref[hh, pl.ds(k0, blk), :], _NT) for hh in range(2)]
        for hh in range(2):
            s = logits[hh]
            if masked:
                s = jnp.where(_iota((blk, blk), 1) <= _iota((blk, blk), 0), s, NEG_BIG)
            m_prev = m_scr[hh]
            m_new = jnp.maximum(m_prev, jnp.max(s, axis=1, keepdims=True))
            alpha = jnp.exp2(m_prev - m_new)
            p = jnp.exp2(s - jnp.tile(m_new, (1, blk // LANES)))
            acc_scr[hh] = alpha * acc_scr[hh] + _dg(p.astype(BF16), va_ref[hh, pl.ds(k0, blk), :], _NN)
            m_scr[hh] = m_new

    def below_diagonal(ki, carry):
        block(ki, False)
        return carry

    lax.fori_loop(0, qi, below_diagonal, 0)
    block(qi, True)
    lane = _iota((1, LANES), 1)
    acc0 = acc_scr[0]
    acc1 = acc_scr[1]
    o = jnp.where(lane < HEAD_DIM, acc0 / acc0[:, HEAD_DIM:HEAD_DIM + 1], acc1 / acc1[:, 0:1])
    o_ref[...] = o * _silu(bz_ref[...])


def _fox_prompt(qa, ka, va, b_z, blk):
    bsz, n_heads, t_len, _ = qa.shape
    d = b_z.shape[-1]
    n_pairs = n_heads // 2
    return pl.pallas_call(
        functools.partial(_fox_prompt_kernel, blk=blk),
        grid=(bsz, n_pairs, t_len // blk),
        in_specs=[pl.BlockSpec((None, 2, blk, LANES), lambda b, p, i: (b, p, i, 0)),
                  pl.BlockSpec((None, 2, t_len, LANES), lambda b, p, i: (b, p, 0, 0)),
                  pl.BlockSpec((None, 2, t_len, LANES), lambda b, p, i: (b, p, 0, 0)),
                  pl.BlockSpec((None, blk, LANES), lambda b, p, i: (b, i, p))],
        out_specs=pl.BlockSpec((None, blk, LANES), lambda b, p, i: (b, i, p)),
        out_shape=jax.ShapeDtypeStruct((bsz, t_len, d), F32),
        scratch_shapes=[pltpu.VMEM((2, blk, LANES), F32),
                        pltpu.VMEM((2, blk, LANES), F32)],
        compiler_params=_cparams(("parallel", "parallel", "arbitrary")),
        name="fox_prompt",
    )(qa, ka, va, b_z)


def _fox_cached_kernel(q_ref, k_ref, v_ref, ck_ref, cv_ref, clf_ref, ccol_ref, crow_ref, bz_ref, o_ref,
                       *, n_heads, lane_blk):
    t_len, d = q_ref.shape
    past = ck_ref.shape[0]
    n_rows = n_heads * t_len

    triu = jnp.where(_iota((lane_blk, lane_blk), 0) <= _iota((lane_blk, lane_blk), 1), 1.0, 0.0).astype(BF16)
    carry = jnp.zeros((n_heads, 1), F32)
    pieces = []
    for j in range(past // lane_blk):
        cj = _mm_exact_rhs(clf_ref[:, j * lane_blk:(j + 1) * lane_blk], triu) + carry
        carry = cj[:, lane_blk - 1:lane_blk]
        pieces.append(cj)
    c_cache = jnp.concatenate(pieces, axis=1)
    total = carry

    rows = lambda f: jnp.concatenate([f(h) for h in range(n_heads)], axis=0)
    ck_cache = rows(lambda h: jnp.broadcast_to(c_cache[h:h + 1, :], (t_len, past)))
    ck_new = rows(lambda h: jnp.broadcast_to(total[h:h + 1, :] + crow_ref[h:h + 1, :], (t_len, t_len)))
    cq = rows(lambda h: total[h:h + 1, :] + ccol_ref[:, h:h + 1])

    lane_head = _div_pow2(_iota((1, d), 1), HEAD_DIM)
    q = q_ref[...] * (HEAD_DIM ** -0.5)
    q_bd = rows(lambda h: jnp.where(lane_head == h, q, 0.0)).astype(BF16)
    s_c = _dg(q_bd, ck_ref[...].astype(BF16), _NT) + cq - ck_cache
    s_n = _dg(q_bd, k_ref[...].astype(BF16), _NT) + cq - ck_new
    q_pos = _iota((n_rows, t_len), 0) & (t_len - 1)
    s_n = jnp.where(_iota((n_rows, t_len), 1) <= q_pos, s_n, NEG_BIG)
    m = jnp.maximum(jnp.max(s_c, axis=1, keepdims=True), jnp.max(s_n, axis=1, keepdims=True))
    p_c = jnp.exp(s_c - m)
    p_n = jnp.exp(s_n - m)
    l = jnp.sum(p_c, axis=1, keepdims=True) + jnp.sum(p_n, axis=1, keepdims=True)
    o_all = (_dg(p_c.astype(BF16), cv_ref[...].astype(BF16), _NN)
             + _dg(p_n.astype(BF16), v_ref[...].astype(BF16), _NN)) / l
    o = jnp.where(lane_head == 0, o_all[0:t_len], 0.0)
    for h in range(1, n_heads):
        o = o + jnp.where(lane_head == h, o_all[h * t_len:(h + 1) * t_len], 0.0)
    o_ref[...] = o * _silu(bz_ref[...])


def _fox_cached(q, k, v, cache_k, cache_v, cache_logf_t, ccol, crow, b_z, layer):
    bsz, t_len, d = q.shape
    n_heads = d // HEAD_DIM
    past = cache_k.shape[2]
    kern = functools.partial(_fox_cached_kernel, n_heads=n_heads, lane_blk=min(past, 256))
    cur = lambda w: pl.BlockSpec((None, t_len, w), lambda b: (b, 0, 0))
    return pl.pallas_call(
        kern,
        grid=(bsz,),
        in_specs=[cur(d), cur(d), cur(d),
                  pl.BlockSpec((None, None, past, d), lambda b: (layer, b, 0, 0)),
                  pl.BlockSpec((None, None, past, d), lambda b: (layer, b, 0, 0)),
                  pl.BlockSpec((None, None, n_heads, past), lambda b: (layer, b, 0, 0)),
                  cur(LANES),
                  pl.BlockSpec((None, n_heads, t_len), lambda b: (b, 0, 0)),
                  cur(d)],
        out_specs=cur(d),
        out_shape=jax.ShapeDtypeStruct((bsz, t_len, d), F32),
        compiler_params=_cparams(("parallel",)),
        name="fox_cached",
    )(q, k, v, cache_k, cache_v, cache_logf_t, ccol, crow, b_z)


def _out_kernel(oa_ref, ob_ref, oc_ref, x_ref, w_ref, g_ref, b_ref, y_ref, *, alpha):
    da = oa_ref.shape[-1]
    db = ob_ref.shape[-1]
    h = jnp.dot(oa_ref[...].astype(BF16), w_ref[0:da, :], preferred_element_type=F32)
    h = h + jnp.dot(ob_ref[...].astype(BF16), w_ref[da:da + db, :], preferred_element_type=F32)
    h = h + jnp.dot(oc_ref[...].astype(BF16), w_ref[da + db:, :], preferred_element_type=F32)
    y = alpha * x_ref[...] + h
    mu = jnp.mean(y, axis=-1, keepdims=True)
    yc = y - mu
    var = jnp.mean(yc * yc, axis=-1, keepdims=True)
    y_ref[...] = yc * lax.rsqrt(var + LN_EPS) * g_ref[...] + b_ref[...]


def _out_project(o_a, o_b, o_c, x2, w_all, layer, g, b, alpha, tm):
    n, d = x2.shape
    d_mix = w_all.shape[1]
    row = lambda w: pl.BlockSpec((tm, w), lambda i: (i, 0))
    return pl.pallas_call(
        functools.partial(_out_kernel, alpha=alpha),
        grid=(n // tm,),
        in_specs=[row(o_a.shape[-1]), row(o_b.shape[-1]), row(o_c.shape[-1]), row(d),
                  pl.BlockSpec((None, d_mix, d), lambda i: (layer, 0, 0)),
                  pl.BlockSpec((1, d), lambda i: (0, 0)),
                  pl.BlockSpec((1, d), lambda i: (0, 0))],
        out_specs=row(d),
        out_shape=jax.ShapeDtypeStruct((n, d), F32),
        compiler_params=_cparams(("parallel",)),
        name="proj_out",
    )(o_a, o_b, o_c, x2, w_all, g.reshape(1, d), b.reshape(1, d))


def _block_diag_state(s):
    bsz, n_h, d0, d1 = s.shape
    eye = jnp.eye(n_h, dtype=s.dtype)
    return jnp.einsum('bhij,hg->bhigj', s, eye).reshape(bsz, n_h * d0, n_h * d1)


def _diag_blocks(s, n_h):
    bsz = s.shape[0]
    s5 = s.reshape(bsz, n_h, HEAD_DIM, n_h, HEAD_DIM)
    return jnp.stack([s5[:, h, :, h, :] for h in range(n_h)], axis=1)


def _expand_heads(p):
    return jnp.repeat(p.astype(F32), HEAD_DIM).reshape(1, -1)


def _select_matrix(rows, n_cols, first_row):
    e = np.zeros((LANES, n_cols), np.float32)
    for h in range(rows):
        e[first_row + h, h * HEAD_DIM:(h + 1) * HEAD_DIM] = 1.0
    return jnp.asarray(e, BF16)


def _row_tile(n):
    for tm in (256, 128, 64, 32, 16, 8):
        if n % tm == 0:
            return tm
    raise ValueError(f"row count {n} must be a multiple of 8")


def kernel(x_prompt, x_sample, cache_fox_k, cache_fox_v, cache_fox_logf, state_rwkv_shift, state_rwkv_wkv,
           state_gdn_conv, state_gdn_wkv, ln_in_g, ln_in_b, w_in, rwkv_mu, rwkv_w0, rwkv_w2, rwkv_a0, rwkv_a2,
           rwkv_k_k, rwkv_k_a, rwkv_r_k, rwkv_gn_g, rwkv_gn_b, fox_b_f, gdn_conv_w, gdn_a_log, gdn_dt_bias,
           gdn_norm_g, w_out, ln_post_g, ln_post_b):
    depth, d_model, _ = w_in.shape
    bp, seq, _ = x_prompt.shape
    bs, dec_seq, _ = x_sample.shape
    h_fox = fox_b_f.shape[1]
    h_gdn = gdn_a_log.shape[1]
    h_rwkv = rwkv_r_k.shape[1]
    d_rwkv, d_fox, d_gdn = h_rwkv * HEAD_DIM, h_fox * HEAD_DIM, h_gdn * HEAD_DIM
    rank_w = rwkv_w2.shape[1]
    rank_a = rwkv_a2.shape[1]
    n_shift = 3 * d_rwkv + rank_w + rank_a
    n_conv_cols = 3 * d_gdn
    n_conv = gdn_conv_w.shape[1]
    alpha = (2 * depth) ** 0.25
    assert rank_w + rank_a == LANES and h_fox + 2 * h_gdn <= LANES

    split = [n_shift, d_rwkv, d_fox, d_fox, d_fox, h_fox, d_fox, n_conv_cols, h_gdn, h_gdn, d_gdn]
    offs = np.concatenate([[0], np.cumsum(split)])
    seg = lambda i: w_in[:, :, offs[i]:offs[i + 1]]
    small_w = jnp.concatenate([seg(5), seg(8), seg(9)], axis=-1)
    small_w = jnp.pad(small_w, ((0, 0), (0, 0), (0, LANES - small_w.shape[-1])))
    w_in_p = jnp.concatenate([seg(0), seg(1), seg(2), seg(3), seg(4), seg(6), seg(7), seg(10), small_w],
                             axis=-1).astype(BF16)
    widths = [n_shift, d_rwkv, d_fox, d_fox, d_fox, d_fox, n_conv_cols, d_gdn, LANES]
    w_out_b = w_out.astype(BF16)

    zw = jnp.zeros((depth, rank_w, d_rwkv), F32)
    w2a2 = jnp.concatenate([jnp.concatenate([rwkv_w2, zw], axis=-1),
                            jnp.concatenate([zw, rwkv_a2], axis=-1)], axis=1).astype(BF16)
    bf_pad = jnp.pad(fox_b_f, ((0, 0), (0, LANES - h_fox)))
    e_beta = _select_matrix(h_gdn, d_gdn, h_fox)
    e_a = _select_matrix(h_gdn, d_gdn, h_fox + h_gdn)
    e_aux_np = np.zeros((h_fox, N_AUX * LANES, LANES), np.float32)
    for h in range(h_fox):
        for j in range(N_AUX):
            e_aux_np[h, j * LANES + h, (HEAD_DIM if h % 2 == 0 else 0) + j] = -1.0
    e_aux = jnp.asarray(e_aux_np, BF16)
    cache_logf_t = jnp.swapaxes(cache_fox_logf, 2, 3)

    def layer_params(l):
        row = lambda a: a[l].reshape(1, -1).astype(F32)
        return dict(
            rwkv=dict(mu=row(rwkv_mu), w0=row(rwkv_w0), a0=row(rwkv_a0), w2a2=w2a2[l], k_k=row(rwkv_k_k),
                      k_a=row(rwkv_k_a), r_k=row(rwkv_r_k), gn_g=row(rwkv_gn_g), gn_b=row(rwkv_gn_b)),
            gdn=dict(conv_w=gdn_conv_w[l], a_log=_expand_heads(gdn_a_log[l]),
                     dt_bias=_expand_heads(gdn_dt_bias[l]),
                     norm_g=jnp.tile(gdn_norm_g[l].reshape(1, -1), (1, h_gdn)), e_beta=e_beta, e_a=e_a),
        )

    def mix_and_project(l, x2, bsz, t_len, a_sh, a_z, c_qkv, small, c_z, o_b, prev0, s_rwkv0, conv0, s_gdn0):
        lp = layer_params(l)
        c = min(HEAD_DIM, t_len)
        gb = N_STACK // (h_rwkv * c)
        n_chains = max(1, min(N_CHAINS, bsz // gb))
        o_a, s_rwkv = _rwkv_mix(a_sh, a_z, prev0, s_rwkv0, lp["rwkv"], gb, n_chains, c, CHUNK_PASSES)
        gb = N_STACK // (h_gdn * c)
        o_c, s_gdn = _gdn_mix(c_qkv, small, c_z, conv0, s_gdn0, lp["gdn"], gb, n_chains, c, CHUNK_PASSES)
        f2 = lambda a: a.reshape(bsz * t_len, a.shape[-1])
        x_new = _out_project(f2(o_a), f2(o_b), f2(o_c), x2, w_out_b, l, ln_post_g[l], ln_post_b[l], alpha,
                             _row_tile(bsz * t_len))
        outs = (a_sh[:, -1], _diag_blocks(s_rwkv, h_rwkv), c_qkv[:, t_len - (n_conv - 1):],
                _diag_blocks(s_gdn, h_gdn))
        return x_new, outs

    def prompt_layer(l, x2, stacked):
        ln_params = (ln_in_g, ln_in_b) if l == 0 else None
        res = _project_prompt(x2.reshape(bp, seq, d_model), w_in_p, l, widths, bf_pad[l:l + 1], e_aux, h_fox,
                              min(seq, PROJ_ROWS), ln_params, stacked)
        a_sh, a_z, b_z, c_qkv, c_z, small, k_st, v_st, logf_st, qa, ka, va = res[:12]
        if l == 0:
            x2 = res[12].reshape(bp * seq, d_model)
        o_b = _fox_prompt(qa, ka, va, b_z, min(seq, FOX_BLOCK))
        x_new, outs = mix_and_project(l, x2, bp, seq, a_sh, a_z, c_qkv, small, c_z, o_b, *zeros_p)
        return x_new, outs, (k_st, v_st, logf_st)

    def sample_layer(l, x2):
        tm = _row_tile(bs * dec_seq)
        r3 = lambda a: a.reshape(bs, dec_seq, a.shape[-1])
        a_sh, a_z, b_q, b_k, b_v, b_z, c_qkv, c_z, small = map(r3, _project(x2, w_in_p, l, widths, tm))
        logf, ccol, crow = _fox_gates(small, bf_pad[l:l + 1], h_fox, min(dec_seq, 256))
        o_b = _fox_cached(b_q, b_k, b_v, cache_fox_k.reshape(depth, bs, -1, d_fox),
                          cache_fox_v.reshape(depth, bs, -1, d_fox), cache_logf_t, ccol, crow, b_z, l)
        x_new, outs = mix_and_project(l, x2, bs, dec_seq, a_sh, a_z, c_qkv, small, c_z, o_b,
                                      state_rwkv_shift[l][:, None, :], _block_diag_state(state_rwkv_wkv[l]),
                                      state_gdn_conv[l], _block_diag_state(state_gdn_wkv[l]))
        heads = lambda a: a.reshape(bs, dec_seq, h_fox, HEAD_DIM)
        return x_new, (heads(b_k), heads(b_v), logf) + outs

    xp = x_prompt.reshape(bp * seq, d_model)
    xs = _layer_norm(x_sample.reshape(bs * dec_seq, d_model), ln_in_g, ln_in_b, _row_tile(bs * dec_seq))
    zeros_p = (jnp.zeros((bp, 1, n_shift), F32), jnp.zeros((bp, d_rwkv, d_rwkv), F32),
               jnp.zeros((bp, n_conv - 1, n_conv_cols), F32), jnp.zeros((bp, d_gdn, d_gdn), F32))
    outs_p, outs_s, stacked = [], [], None
    for l in range(depth):
        xp, o, stacked = prompt_layer(l, xp, stacked)
        outs_p.append(o)
        xs, o = sample_layer(l, xs)
        outs_s.append(o)
    k_st, v_st, logf_st = stacked
    stack = lambda outs, i: jnp.stack([o[i] for o in outs])
    return ((xp.reshape(bp, seq, d_model), xs.reshape(bs, dec_seq, d_model),
             k_st.reshape(depth, bp, seq, h_fox, HEAD_DIM), v_st.reshape(depth, bp, seq, h_fox, HEAD_DIM), logf_st)
            + tuple(stack(outs_p, i) for i in range(4)) + tuple(stack(outs_s, i) for i in range(7)))
```

```python
import functools
import math

import jax
import jax.numpy as jnp
import numpy as np
from jax import lax
from jax.experimental import pallas as pl
from jax.experimental.pallas import tpu as pltpu

F32 = jnp.float32
BF16 = jnp.bfloat16

HEAD_DIM = 64
LANES = 128
N_STACK = 256
PROJ_ROWS = 512
FOX_BLOCK = 512
N_CHAINS = 4
CHUNK_PASSES = 1
LN_EPS = 1e-5
RWKV_GN_EPS = 64e-5
GDN_NORM_EPS = 1e-6
L2_EPS = 1e-6
NEG_BIG = -1e30
LOG2E = math.log2(math.e)
VMEM_LIMIT = 56 * 1024 * 1024


def _sigmoid(x):
    return 1.0 / (1.0 + jnp.exp(-x))


def _softplus(x):
    return jnp.maximum(x, 0.0) + jnp.log(1.0 + jnp.exp(-jnp.abs(x)))


def _silu(x):
    return x * _sigmoid(x)


def _split3(x):
    hi = x.astype(BF16)
    r1 = x - hi.astype(F32)
    mid = r1.astype(BF16)
    lo = (r1 - mid.astype(F32)).astype(BF16)
    return hi, mid, lo


_NN = (((1,), (0,)), ((), ()))
_NT = (((1,), (1,)), ((), ()))
_TN = (((0,), (0,)), ((), ()))


def _dg(a, b, dims):
    return lax.dot_general(a, b, dims, preferred_element_type=F32)


def _mm(a, b, dims=_NN, passes=1):
    if passes == 1:
        return _dg(a.astype(BF16), b.astype(BF16), dims)
    ah, am, _ = _split3(a)
    bh, bm, _ = _split3(b)
    return _dg(ah, bh, dims) + (_dg(ah, bm, dims) + _dg(am, bh, dims))


def _mm_exact_rhs(a, e, dims=_NN):
    hi, mid, lo = _split3(a)
    return _dg(hi, e, dims) + (_dg(mid, e, dims) + _dg(lo, e, dims))


def _mm_exact_lhs(e, b, dims=_NN):
    hi, mid, lo = _split3(b)
    return _dg(e, hi, dims) + (_dg(e, mid, dims) + _dg(e, lo, dims))


def _iota(shape, dim):
    return lax.broadcasted_iota(jnp.int32, shape, dim)


def _div_pow2(x, n):
    assert n & (n - 1) == 0
    return lax.shift_right_logical(x, jnp.int32(int(math.log2(n))))


def _head_block_ones(n):
    r = _div_pow2(_iota((n, n), 0), HEAD_DIM)
    c = _div_pow2(_iota((n, n), 1), HEAD_DIM)
    return jnp.where(r == c, 1.0, 0.0).astype(BF16)


def _block_diag(tall):
    n = tall.shape[0]
    wide = jnp.concatenate([tall] * (n // HEAD_DIM), axis=1)
    same = _div_pow2(_iota((n, n), 0), HEAD_DIM) == _div_pow2(_iota((n, n), 1), HEAD_DIM)
    return jnp.where(same, wide, 0.0)


def _stack_heads(x, gb, c, n_heads):
    width = x.shape[1]
    lane_head = _div_pow2(_iota((1, width), 1), HEAD_DIM)
    parts = []
    for b in range(gb):
        xb = x[b * c:(b + 1) * c, :]
        for h in range(n_heads):
            parts.append(jnp.where(lane_head == h, xb, 0.0))
    return jnp.concatenate(parts, axis=0)


def _unstack_heads(xs, gb, c, n_heads):
    outs = []
    for b in range(gb):
        acc = xs[(b * n_heads) * c:(b * n_heads + 1) * c, :]
        for h in range(1, n_heads):
            acc = acc + xs[(b * n_heads + h) * c:(b * n_heads + h + 1) * c, :]
        outs.append(acc)
    return jnp.concatenate(outs, axis=0) if gb > 1 else outs[0]


def _unit_lower_inverse(p, c, passes):
    n = p.shape[0]
    eye = jnp.where(_iota((n, n), 0) == _iota((n, n), 1), 1.0, 0.0)
    t = eye + p
    n_rounds = int(math.log2(c)) - 1
    pk = _mm(p, p, passes=passes)
    yield
    for i in range(n_rounds):
        t_next = t + _mm(t, pk, passes=passes)
        if i + 1 < n_rounds:
            pk = _mm(pk, pk, passes=passes)
        t = t_next
        yield
    return t


def _run_interleaved(chains):
    results = [None] * len(chains)
    active = list(enumerate(chains))
    while active:
        still = []
        for i, ch in active:
            try:
                next(ch)
                still.append((i, ch))
            except StopIteration as stop:
                results[i] = stop.value
        active = still
    return results


def _chunk_masks(n, c):
    row = _iota((n, n), 0)
    col = _iota((n, n), 1)
    same = _div_pow2(row, c) == _div_pow2(col, c)
    strict = jnp.logical_and(same, row > col)
    incl = jnp.logical_and(same, row >= col)
    return strict, incl


def _cumsum_rows(x, gb, c):
    n = gb * c
    strict, incl = _chunk_masks(n, c)
    del strict
    tri = jnp.where(incl, 1.0, 0.0).astype(BF16)
    return _mm_exact_lhs(tri, x)


def _shift_rows(x, j, prev_rows, gb, c):
    n_prev = prev_rows[0].shape[0]
    out = pltpu.roll(x, j, 0)
    rid = _iota((x.shape[0], 1), 0)
    for b in range(gb):
        for t in range(j):
            src = n_prev + t - j
            out = jnp.where(rid == b * c + t, prev_rows[b][src:src + 1, :], out)
    return out


def _cparams(sem):
    return pltpu.CompilerParams(dimension_semantics=sem, vmem_limit_bytes=VMEM_LIMIT)


def _ln_kernel(x_ref, g_ref, b_ref, o_ref):
    x = x_ref[...]
    mu = jnp.mean(x, axis=-1, keepdims=True)
    xc = x - mu
    var = jnp.mean(xc * xc, axis=-1, keepdims=True)
    o_ref[...] = xc * lax.rsqrt(var + LN_EPS) * g_ref[...] + b_ref[...]


def _layer_norm(x2, g, b, tm):
    n, d = x2.shape
    return pl.pallas_call(
        _ln_kernel,
        grid=(n // tm,),
        in_specs=[pl.BlockSpec((tm, d), lambda i: (i, 0)),
                  pl.BlockSpec((1, d), lambda i: (0, 0)),
                  pl.BlockSpec((1, d), lambda i: (0, 0))],
        out_specs=pl.BlockSpec((tm, d), lambda i: (i, 0)),
        out_shape=jax.ShapeDtypeStruct((n, d), F32),
        compiler_params=_cparams(("parallel",)),
        name="ln_in",
    )(x2, g.reshape(1, d), b.reshape(1, d))


def _proj_kernel(x_ref, w_ref, *o_refs):
    xb = x_ref[...].astype(BF16)
    off = 0
    for o_ref in o_refs:
        wdt = o_ref.shape[-1]
        o_ref[...] = jnp.dot(xb, w_ref[:, off:off + wdt], preferred_element_type=F32)
        off += wdt


def _project(x2, w_all, layer, widths, tm):
    n, d = x2.shape
    n_cols = w_all.shape[-1]
    return pl.pallas_call(
        _proj_kernel,
        grid=(n // tm,),
        in_specs=[pl.BlockSpec((tm, d), lambda i: (i, 0)),
                  pl.BlockSpec((None, d, n_cols), lambda i: (layer, 0, 0))],
        out_specs=[pl.BlockSpec((tm, w), lambda i: (i, 0)) for w in widths],
        out_shape=[jax.ShapeDtypeStruct((n, w), F32) for w in widths],
        compiler_params=_cparams(("parallel",)),
        name="proj_in",
    )(x2, w_all)


def _rwkv_kernel(ash_ref, az_ref, prev0_ref, s0_ref, mu_ref, w0_ref, a0_ref, w2a2_ref, kk_ref, ka_ref,
                 rk_ref, gng_ref, gnb_ref, o_ref, s_ref, prev_scr, *, gb, n_chains, c, n_heads, passes):
    @pl.when(pl.program_id(1) == 0)
    def _():
        for b in range(s_ref.shape[0]):
            s_ref[b] = _block_diag(s0_ref[b])
        prev_scr[...] = prev0_ref[...]

    prm = dict(mu=mu_ref[...], w0=w0_ref[...], a0=a0_ref[...], w2a2=w2a2_ref[...], k_k=kk_ref[...],
               k_a=ka_ref[...], r_k=rk_ref[...], gn_g=gng_ref[...], gn_b=gnb_ref[...])
    ins = []
    for ch in range(n_chains):
        b0 = ch * gb
        ins.append((ash_ref[b0:b0 + gb].reshape(gb * c, ash_ref.shape[-1]),
                    az_ref[b0:b0 + gb].reshape(gb * c, az_ref.shape[-1]),
                    [prev_scr[b0 + b] for b in range(gb)],
                    [s_ref[b0 + b] for b in range(gb)]))
    outs = _run_interleaved([_rwkv_chain(*args, prm, gb=gb, c=c, n_heads=n_heads, passes=passes) for args in ins])
    for ch, (o, s_new, last_rows) in enumerate(outs):
        b0 = ch * gb
        o_ref[b0:b0 + gb] = o.reshape(gb, c, o.shape[-1])
        for b in range(gb):
            s_ref[b0 + b] = s_new[b]
            prev_scr[b0 + b] = last_rows[b]


def _rwkv_chain(p, z, prev_rows, s_old, prm, *, gb, c, n_heads, passes):
    d = n_heads * HEAD_DIM
    prev = _shift_rows(p, 1, prev_rows, gb, c)
    last_rows = [p[(b + 1) * c - 1:(b + 1) * c, :] for b in range(gb)]
    xs = p + (prev - p) * prm["mu"]
    r = xs[:, 0:d]
    k = xs[:, d:2 * d]
    v = xs[:, 2 * d:3 * d]
    wa = xs[:, 3 * d:3 * d + LANES]
    lane = _iota((1, LANES), 1)
    wa = jnp.where(lane < HEAD_DIM, jnp.tanh(wa), wa)
    pre = _mm(wa, prm["w2a2"])
    yield
    w_ll = -_softplus(-(prm["w0"] + pre[:, 0:d])) - 0.5
    lw = -jnp.exp(w_ll)
    a = _sigmoid(prm["a0"] + pre[:, d:2 * d])

    ones_bd = _head_block_ones(d)
    kk = k * prm["k_k"]
    kk = kk / jnp.maximum(jnp.sqrt(_mm_exact_rhs(kk * kk, ones_bd)), 1e-12)
    kmod = k * (1.0 + (a - 1.0) * prm["k_a"])
    kka = kk * a

    cum = _cumsum_rows(lw, gb, c)
    yield
    cum_last = jnp.concatenate(
        [jnp.broadcast_to(cum[(b + 1) * c - 1:(b + 1) * c, :], (c, d)) for b in range(gb)], axis=0)
    inv_p = jnp.exp(-cum)
    to_end = jnp.exp(cum_last - cum)
    b_t = kk * jnp.exp(cum - lw)
    a_t = -kka * inv_p
    k_t = kmod * inv_p
    r_t = r * jnp.exp(cum)

    st = functools.partial(_stack_heads, gb=gb, c=c, n_heads=n_heads)
    bs, rs, as_, ks, vs = st(b_t), st(r_t), st(a_t), st(k_t), st(v)
    n = bs.shape[0]
    g = _mm(jnp.concatenate([bs, rs], axis=0), jnp.concatenate([as_, ks], axis=0), _NT, passes)
    yield
    strict, incl = _chunk_masks(n, c)
    m_ab = jnp.where(strict, g[0:n, 0:n], 0.0)
    m_bk = jnp.where(strict, g[0:n, n:2 * n], 0.0)
    m_ra = jnp.where(incl, g[n:2 * n, 0:n], 0.0)
    m_rk = jnp.where(incl, g[n:2 * n, n:2 * n], 0.0)

    hc = n_heads * c
    bh = jnp.concatenate([_mm(bs[b * hc:(b + 1) * hc], s_old[b], _NT, passes) for b in range(gb)], axis=0)
    rh = jnp.concatenate([_mm(rs[b * hc:(b + 1) * hc], s_old[b], _NT, passes) for b in range(gb)], axis=0)
    rhs_u = bh + _mm(m_bk, vs, passes=passes)
    o_s = rh + _mm(m_rk, vs, passes=passes)
    t_inv = yield from _unit_lower_inverse(m_ab, c, passes)
    u = _mm(t_inv, rhs_u, passes=passes)
    yield
    o_s = o_s + _mm(m_ra, u, passes=passes)
    o = _unstack_heads(o_s, gb, c, n_heads)

    a_end = st(-kka * to_end)
    k_end = st(kmod * to_end)
    s_new = []
    for b in range(gb):
        sl = slice(b * hc, (b + 1) * hc)
        p_end = jnp.exp(cum[(b + 1) * c - 1:(b + 1) * c, :])
        s_new.append(s_old[b] * p_end + _mm(u[sl], a_end[sl], _TN, passes)
                     + _mm(vs[sl], k_end[sl], _TN, passes))
    yield

    mean = _mm_exact_rhs(o, ones_bd) * (1.0 / HEAD_DIM)
    oc = o - mean
    yield
    var = _mm_exact_rhs(oc * oc, ones_bd) * (1.0 / HEAD_DIM)
    o = oc * lax.rsqrt(var + RWKV_GN_EPS) * prm["gn_g"] + prm["gn_b"]
    bonus = _mm_exact_rhs(r * kmod * prm["r_k"], ones_bd) * v
    return ((o + bonus) * _silu(z)).astype(BF16), s_new, last_rows


def _rwkv_mix(a_sh, a_z, prev0, s0, lp, gb, n_chains, c, passes):
    bsz, t_len, n_shift = a_sh.shape
    d = a_z.shape[-1]
    n_heads = d // HEAD_DIM
    kern = functools.partial(_rwkv_kernel, gb=gb, n_chains=n_chains, c=c, n_heads=n_heads, passes=passes)
    gb = gb * n_chains
    vec = lambda w: pl.BlockSpec((1, w), lambda b, s: (0, 0))
    return pl.pallas_call(
        kern,
        grid=(bsz // gb, t_len // c),
        in_specs=[pl.BlockSpec((gb, c, n_shift), lambda b, s: (b, s, 0)),
                  pl.BlockSpec((gb, c, d), lambda b, s: (b, s, 0)),
                  pl.BlockSpec((gb, 1, n_shift), lambda b, s: (b, 0, 0)),
                  pl.BlockSpec((gb, d, HEAD_DIM), lambda b, s: (b, 0, 0)),
                  vec(n_shift), vec(d), vec(d),
                  pl.BlockSpec((LANES, 2 * d), lambda b, s: (0, 0)),
                  vec(d), vec(d), vec(d), vec(d), vec(d)],
        out_specs=[pl.BlockSpec((gb, c, d), lambda b, s: (b, s, 0)),
                   pl.BlockSpec((gb, d, d), lambda b, s: (b, 0, 0))],
        out_shape=[jax.ShapeDtypeStruct((bsz, t_len, d), BF16),
                   jax.ShapeDtypeStruct((bsz, d, d), F32)],
        scratch_shapes=[pltpu.VMEM((gb, 1, n_shift), F32)],
        compiler_params=_cparams(("parallel", "arbitrary")),
        name="rwkv_mix",
    )(a_sh, a_z, prev0, s0, lp["mu"], lp["w0"], lp["a0"], lp["w2a2"], lp["k_k"], lp["k_a"], lp["r_k"],
      lp["gn_g"], lp["gn_b"])


def _gdn_kernel(qkv_ref, small_ref, cz_ref, conv0_ref, s0_ref, convw_ref, alog_ref, dtb_ref, ng_ref,
                eb_ref, ea_ref, o_ref, s_ref, conv_scr, *, gb, n_chains, c, n_heads, passes):
    @pl.when(pl.program_id(1) == 0)
    def _():
        for b in range(s_ref.shape[0]):
            s_ref[b] = _block_diag(s0_ref[b])
        conv_scr[...] = conv0_ref[...]

    prm = dict(conv_w=convw_ref[...], a_log=alog_ref[...], dt_bias=dtb_ref[...], norm_g=ng_ref[...],
               e_beta=eb_ref[...], e_a=ea_ref[...])
    ins = []
    for ch in range(n_chains):
        b0 = ch * gb
        ins.append((qkv_ref[b0:b0 + gb].reshape(gb * c, qkv_ref.shape[-1]),
                    small_ref[b0:b0 + gb].reshape(gb * c, LANES),
                    cz_ref[b0:b0 + gb].reshape(gb * c, cz_ref.shape[-1]),
                    [conv_scr[b0 + b] for b in range(gb)],
                    [s_ref[b0 + b] for b in range(gb)]))
    outs = _run_interleaved([_gdn_chain(*args, prm, gb=gb, c=c, n_heads=n_heads, passes=passes) for args in ins])
    for ch, (o, s_new, last_rows) in enumerate(outs):
        b0 = ch * gb
        o_ref[b0:b0 + gb] = o.reshape(gb, c, o.shape[-1])
        for b in range(gb):
            s_ref[b0 + b] = s_new[b]
            conv_scr[b0 + b] = last_rows[b]


def _gdn_chain(x, small, z, prev_rows, s_old, prm, *, gb, c, n_heads, passes):
    d = n_heads * HEAD_DIM
    conv_w = prm["conv_w"]
    n_conv = conv_w.shape[0]
    y = x * conv_w[n_conv - 1:n_conv, :]
    for j in range(1, n_conv):
        y = y + _shift_rows(x, j, prev_rows, gb, c) * conv_w[n_conv - 1 - j:n_conv - j, :]
    last_rows = [x[(b + 1) * c - (n_conv - 1):(b + 1) * c, :] for b in range(gb)]
    y = _silu(y)
    q = y[:, 0:d]
    k = y[:, d:2 * d]
    v = y[:, 2 * d:3 * d]
    ones_bd = _head_block_ones(d)
    q = q * lax.rsqrt(_mm_exact_rhs(q * q, ones_bd) + L2_EPS) * (HEAD_DIM ** -0.5)
    k = k * lax.rsqrt(_mm_exact_rhs(k * k, ones_bd) + L2_EPS)

    beta = _sigmoid(_mm_exact_rhs(small, prm["e_beta"]))
    g = -jnp.exp(prm["a_log"]) * _softplus(_mm_exact_rhs(small, prm["e_a"]) + prm["dt_bias"])
    yield
    gc = _cumsum_rows(g, gb, c)
    yield
    gc_last = jnp.concatenate(
        [jnp.broadcast_to(gc[(b + 1) * c - 1:(b + 1) * c, :], (c, d)) for b in range(gb)], axis=0)
    egc = jnp.exp(gc)

    st = functools.partial(_stack_heads, gb=gb, c=c, n_heads=n_heads)
    kb = k * beta
    kbs, ks, qs = st(kb), st(k), st(q)
    n = kbs.shape[0]
    strict, incl = _chunk_masks(n, c)
    gcol = jnp.min(st(gc), axis=1, keepdims=True)
    eye = _iota((n, n), 0) == _iota((n, n), 1)
    grow = jnp.sum(jnp.where(eye, jnp.broadcast_to(gcol, (n, n)), 0.0), axis=0, keepdims=True)
    dmat = jnp.exp(jnp.where(incl, gcol - grow, 0.0))
    g2 = _mm(jnp.concatenate([kbs, qs], axis=0), ks, _NT, passes)
    yield
    m = jnp.where(strict, g2[0:n] * dmat, 0.0)
    qk = jnp.where(incl, g2[n:2 * n] * dmat, 0.0)
    t_inv = yield from _unit_lower_inverse(-m, c, passes)
    rhs = jnp.concatenate([st(v * beta), st(kb * egc)], axis=1)
    sol = _mm(t_inv, rhs, passes=passes)
    yield
    u = sol[:, 0:d]
    w = sol[:, d:2 * d]

    hc = n_heads * c
    qgs = st(q * egc)
    kds = st(k * jnp.exp(gc_last - gc))
    v_new = u - jnp.concatenate(
        [_mm(w[b * hc:(b + 1) * hc], s_old[b], passes=passes) for b in range(gb)], axis=0)
    o_s = jnp.concatenate(
        [_mm(qgs[b * hc:(b + 1) * hc], s_old[b], passes=passes) for b in range(gb)], axis=0)
    yield
    o_s = o_s + _mm(qk, v_new, passes=passes)
    s_new = []
    for b in range(gb):
        sl = slice(b * hc, (b + 1) * hc)
        gl = jnp.exp(gc[(b + 1) * c - 1:(b + 1) * c, :])
        s_new.append(s_old[b] * gl + _mm(kds[sl], v_new[sl], _TN, passes))
    yield

    o = _unstack_heads(o_s, gb, c, n_heads)
    ms = _mm_exact_rhs(o * o, ones_bd) * (1.0 / HEAD_DIM)
    o = o * lax.rsqrt(ms + GDN_NORM_EPS) * prm["norm_g"]
    return (o * _silu(z)).astype(BF16), s_new, last_rows


def _gdn_mix(c_qkv, small, c_z, conv0, s0, lp, gb, n_chains, c, passes):
    bsz, t_len, n_qkv = c_qkv.shape
    d = c_z.shape[-1]
    n_heads = d // HEAD_DIM
    n_conv = lp["conv_w"].shape[0]
    kern = functools.partial(_gdn_kernel, gb=gb, n_chains=n_chains, c=c, n_heads=n_heads, passes=passes)
    gb = gb * n_chains
    vec = lambda w: pl.BlockSpec((1, w), lambda b, s: (0, 0))
    return pl.pallas_call(
        kern,
        grid=(bsz // gb, t_len // c),
        in_specs=[pl.BlockSpec((gb, c, n_qkv), lambda b, s: (b, s, 0)),
                  pl.BlockSpec((gb, c, LANES), lambda b, s: (b, s, 0)),
                  pl.BlockSpec((gb, c, d), lambda b, s: (b, s, 0)),
                  pl.BlockSpec((gb, n_conv - 1, n_qkv), lambda b, s: (b, 0, 0)),
                  pl.BlockSpec((gb, d, HEAD_DIM), lambda b, s: (b, 0, 0)),
                  pl.BlockSpec((n_conv, n_qkv), lambda b, s: (0, 0)),
                  vec(d), vec(d), vec(d),
                  pl.BlockSpec((LANES, d), lambda b, s: (0, 0)),
                  pl.BlockSpec((LANES, d), lambda b, s: (0, 0))],
        out_specs=[pl.BlockSpec((gb, c, d), lambda b, s: (b, s, 0)),
                   pl.BlockSpec((gb, d, d), lambda b, s: (b, 0, 0))],
        out_shape=[jax.ShapeDtypeStruct((bsz, t_len, d), BF16),
                   jax.ShapeDtypeStruct((bsz, d, d), F32)],
        scratch_shapes=[pltpu.VMEM((gb, n_conv - 1, n_qkv), F32)],
        compiler_params=_cparams(("parallel", "arbitrary")),
        name="gdn_mix",
    )(c_qkv, small, c_z, conv0, s0, lp["conv_w"], lp["a_log"], lp["dt_bias"], lp["norm_g"],
      lp["e_beta"], lp["e_a"])


def _log_forget_cumsum(small, bf, carry_scr):
    @pl.when(pl.program_id(1) == 0)
    def _():
        carry_scr[...] = jnp.zeros_like(carry_scr)

    logf = -_softplus(-(small + bf))
    tb = logf.shape[0]
    c = _cumsum_rows(logf, 1, tb) + carry_scr[...]
    carry_scr[...] = c[tb - 1:tb, :]
    return logf, c


def _fox_gate_kernel(small_ref, bf_ref, logf_ref, ccol_ref, crow_ref, carry_scr, *, n_heads):
    logf, c = _log_forget_cumsum(small_ref[...], bf_ref[...], carry_scr)
    logf_ref[...] = logf[:, 0:n_heads]
    ccol_ref[...] = c
    sel = jnp.where(_iota((n_heads, LANES), 0) == _iota((n_heads, LANES), 1), 1.0, 0.0).astype(BF16)
    crow_ref[...] = _mm_exact_lhs(sel, c, _NT)


def _fox_gates(small, bf_pad, n_heads, tb):
    bsz, t_len, _ = small.shape
    kern = functools.partial(_fox_gate_kernel, n_heads=n_heads)
    return pl.pallas_call(
        kern,
        grid=(bsz, t_len // tb),
        in_specs=[pl.BlockSpec((None, tb, LANES), lambda b, s: (b, s, 0)),
                  pl.BlockSpec((1, LANES), lambda b, s: (0, 0))],
        out_specs=[pl.BlockSpec((None, tb, n_heads), lambda b, s: (b, s, 0)),
                   pl.BlockSpec((None, tb, LANES), lambda b, s: (b, s, 0)),
                   pl.BlockSpec((None, n_heads, tb), lambda b, s: (b, 0, s))],
        out_shape=[jax.ShapeDtypeStruct((bsz, t_len, n_heads), F32),
                   jax.ShapeDtypeStruct((bsz, t_len, LANES), F32),
                   jax.ShapeDtypeStruct((bsz, n_heads, t_len), F32)],
        scratch_shapes=[pltpu.VMEM((1, LANES), F32)],
        compiler_params=_cparams(("parallel", "arbitrary")),
        name="fox_gates",
    )(small, bf_pad)


N_AUX = 3


def _proj_prompt_kernel(*refs, widths, n_heads, ln_in, n_alias):
    refs = list(refs)
    x_ref, w_ref, bf_ref, eaux_ref = refs[0:4]
    pos = 4
    if ln_in:
        g_ref, b_ref = refs[pos:pos + 2]
        pos += 2
    pos += n_alias
    (ash_ref, az_ref, bz_ref, cqkv_ref, cz_ref, small_ref, k_ref, v_ref, logf_ref,
     qa_ref, ka_ref, va_ref) = refs[pos:pos + 12]
    pos += 12
    if ln_in:
        xn_ref = refs[pos]
        pos += 1
    carry_scr = refs[pos]

    x = x_ref[...]
    if ln_in:
        mu = jnp.mean(x, axis=-1, keepdims=True)
        xc = x - mu
        var = jnp.mean(xc * xc, axis=-1, keepdims=True)
        x = xc * lax.rsqrt(var + LN_EPS) * g_ref[...] + b_ref[...]
        xn_ref[...] = x
    xb = x.astype(BF16)
    offs = np.concatenate([[0], np.cumsum(widths)])
    seg = lambda i: jnp.dot(xb, w_ref[:, int(offs[i]):int(offs[i + 1])], preferred_element_type=F32)
    ash_ref[...] = seg(0)
    az_ref[...] = seg(1)
    q, k, v = seg(2), seg(3), seg(4)
    bz_ref[...] = seg(5)
    cqkv_ref[...] = seg(6)
    cz_ref[...] = seg(7)
    small = seg(8)
    small_ref[...] = small
    k_ref[...] = k
    v_ref[...] = v

    logf, c = _log_forget_cumsum(small, bf_ref[...], carry_scr)
    logf_ref[...] = logf[:, 0:n_heads]
    parts = jnp.concatenate(_split3(c * LOG2E), axis=1)
    lane = _iota((1, LANES), 1)
    for h in range(n_heads):
        pair, hh = divmod(h, 2)
        own = (lane < HEAD_DIM) if hh == 0 else (lane >= HEAD_DIM)
        a0 = HEAD_DIM if hh == 0 else 0
        aux_q = jnp.where((lane >= a0) & (lane < a0 + N_AUX), 1.0, 0.0)
        aux_k = _dg(parts, eaux_ref[h], _NN)
        aux_v = jnp.where(lane == a0, 1.0, 0.0)
        cols = slice(pair * LANES, (pair + 1) * LANES)
        qa_ref[h] = jnp.where(own, q[:, cols] * (HEAD_DIM ** -0.5 * LOG2E), aux_q).astype(BF16)
        ka_ref[h] = jnp.where(own, k[:, cols], aux_k).astype(BF16)
        va_ref[h] = jnp.where(own, v[:, cols], aux_v).astype(BF16)


def _project_prompt(x3, w_all, layer, widths, bf_pad, e_aux, n_heads, tm, ln_params, stacked):
    bsz, t_len, d = x3.shape
    depth, _, n_cols = w_all.shape
    d_fox = n_heads * HEAD_DIM
    ln_in = ln_params is not None
    n_alias = 0 if stacked is None else len(stacked)
    kern = functools.partial(_proj_prompt_kernel, widths=tuple(widths), n_heads=n_heads, ln_in=ln_in,
                             n_alias=n_alias)
    row = lambda w: pl.BlockSpec((None, tm, w), lambda b, s: (b, s, 0))
    per_layer = lambda w: pl.BlockSpec((None, None, tm, w), lambda b, s: (layer, b, s, 0))
    per_head = pl.BlockSpec((None, n_heads, tm, LANES), lambda b, s: (b, 0, s, 0))
    const = lambda shape: pl.BlockSpec(shape, lambda b, s: (0,) * len(shape))
    in_specs = [row(d), pl.BlockSpec((None, d, n_cols), lambda b, s: (layer, 0, 0), pipeline_mode=pl.Buffered(1)),
                const((1, LANES)), const((n_heads, N_AUX * LANES, LANES))]
    args = [x3, w_all, bf_pad, e_aux]
    if ln_in:
        in_specs += [const((1, d)), const((1, d))]
        args += [p.reshape(1, d) for p in ln_params]
    aliases = {}
    if stacked is not None:
        for j, buf in enumerate(stacked):
            aliases[len(args)] = 6 + j
            in_specs.append(pl.BlockSpec(memory_space=pl.ANY))
            args.append(buf)
    f32 = lambda *shape: jax.ShapeDtypeStruct(shape, F32)
    out_specs = [row(widths[0]), row(widths[1]), row(widths[5]), row(widths[6]), row(widths[7]), row(LANES),
                 per_layer(d_fox), per_layer(d_fox), per_layer(n_heads), per_head, per_head, per_head]
    out_shape = [f32(bsz, t_len, widths[0]), f32(bsz, t_len, widths[1]), f32(bsz, t_len, widths[5]),
                 f32(bsz, t_len, widths[6]), f32(bsz, t_len, widths[7]), f32(bsz, t_len, LANES),
                 f32(depth, bsz, t_len, d_fox), f32(depth, bsz, t_len, d_fox), f32(depth, bsz, t_len, n_heads)]
    out_shape += [jax.ShapeDtypeStruct((bsz, n_heads, t_len, LANES), BF16)] * 3
    if ln_in:
        out_specs.append(row(d))
        out_shape.append(f32(bsz, t_len, d))
    return pl.pallas_call(
        kern,
        grid=(bsz, t_len // tm),
        in_specs=in_specs,
        out_specs=out_specs,
        out_shape=out_shape,
        input_output_aliases=aliases,
        scratch_shapes=[pltpu.VMEM((1, LANES), F32)],
        compiler_params=_cparams(("parallel", "arbitrary")),
        name="proj_prompt",
    )(*args)


def _fox_prompt_kernel(qa_ref, ka_ref, va_ref, bz_ref, o_ref, m_scr, acc_scr, *, blk):
    qi = pl.program_id(2)
    m_scr[...] = jnp.full_like(m_scr, NEG_BIG)
    acc_scr[...] = jnp.zeros_like(acc_scr)

    def block(ki, masked):
        k0 = pl.multiple_of(ki * blk, blk)
        logits = [_dg(qa_ref[hh], ka_ref[hh, pl.ds(k0, blk), :], _NT) for hh in range(2)]
        for hh in range(2):
            s = logits[hh]
            if masked:
                s = jnp.where(_iota((blk, blk), 1) <= _iota((blk, blk), 0), s, NEG_BIG)
            m_prev = m_scr[hh]
            m_new = jnp.maximum(m_prev, jnp.max(s, axis=1, keepdims=True))
            alpha = jnp.exp2(m_prev - m_new)
            p = jnp.exp2(s - jnp.tile(m_new, (1, blk // LANES)))
            acc_scr[hh] = alpha * acc_scr[hh] + _dg(p.astype(BF16), va_ref[hh, pl.ds(k0, blk), :], _NN)
            m_scr[hh] = m_new

    def below_diagonal(ki, carry):
        block(ki, False)
        return carry

    lax.fori_loop(0, qi, below_diagonal, 0)
    block(qi, True)
    lane = _iota((1, LANES), 1)
    acc0 = acc_scr[0]
    acc1 = acc_scr[1]
    o = jnp.where(lane < HEAD_DIM, acc0 / acc0[:, HEAD_DIM:HEAD_DIM + 1], acc1 / acc1[:, 0:1])
    o_ref[...] = (o * _silu(bz_ref[...])).astype(BF16)


def _fox_prompt(qa, ka, va, b_z, blk):
    bsz, n_heads, t_len, _ = qa.shape
    d = b_z.shape[-1]
    n_pairs = n_heads // 2
    return pl.pallas_call(
        functools.partial(_fox_prompt_kernel, blk=blk),
        grid=(bsz, n_pairs, t_len // blk),
        in_specs=[pl.BlockSpec((None, 2, blk, LANES), lambda b, p, i: (b, p, i, 0)),
                  pl.BlockSpec((None, 2, t_len, LANES), lambda b, p, i: (b, p, 0, 0)),
                  pl.BlockSpec((None, 2, t_len, LANES), lambda b, p, i: (b, p, 0, 0)),
                  pl.BlockSpec((None, blk, LANES), lambda b, p, i: (b, i, p))],
        out_specs=pl.BlockSpec((None, blk, LANES), lambda b, p, i: (b, i, p)),
        out_shape=jax.ShapeDtypeStruct((bsz, t_len, d), BF16),
        scratch_shapes=[pltpu.VMEM((2, blk, LANES), F32),
                        pltpu.VMEM((2, blk, LANES), F32)],
        compiler_params=_cparams(("parallel", "parallel", "arbitrary")),
        name="fox_prompt",
    )(qa, ka, va, b_z)


def _fox_cached_kernel(q_ref, k_ref, v_ref, ck_ref, cv_ref, clf_ref, ccol_ref, crow_ref, bz_ref, o_ref,
                       *, n_heads, lane_blk):
    t_len, d = q_ref.shape
    past = ck_ref.shape[1]
    n_rows = n_heads * t_len

    triu = jnp.where(_iota((lane_blk, lane_blk), 0) <= _iota((lane_blk, lane_blk), 1), 1.0, 0.0).astype(BF16)
    carry = jnp.zeros((n_heads, 1), F32)
    pieces = []
    for j in range(past // lane_blk):
        cj = _mm_exact_rhs(clf_ref[:, j * lane_blk:(j + 1) * lane_blk], triu) + carry
        carry = cj[:, lane_blk - 1:lane_blk]
        pieces.append(cj)
    c_cache = jnp.concatenate(pieces, axis=1)
    total = carry

    rows = lambda f: jnp.concatenate([f(h) for h in range(n_heads)], axis=0)
    ck_cache = rows(lambda h: jnp.broadcast_to(c_cache[h:h + 1, :], (t_len, past)))
    ck_new = rows(lambda h: jnp.broadcast_to(total[h:h + 1, :] + crow_ref[h:h + 1, :], (t_len, t_len)))
    cq = rows(lambda h: total[h:h + 1, :] + ccol_ref[:, h:h + 1])

    lane_head = _div_pow2(_iota((1, d), 1), HEAD_DIM)
    q = q_ref[...] * (HEAD_DIM ** -0.5)
    q_bd = rows(lambda h: jnp.where(lane_head == h, q, 0.0)).astype(BF16)
    s_c = _dg(q_bd, ck_ref[...].astype(BF16), _NN) + cq - ck_cache
    s_n = _dg(q_bd, k_ref[...].astype(BF16), _NT) + cq - ck_new
    q_pos = _iota((n_rows, t_len), 0) & (t_len - 1)
    s_n = jnp.where(_iota((n_rows, t_len), 1) <= q_pos, s_n, NEG_BIG)
    m = jnp.maximum(jnp.max(s_c, axis=1, keepdims=True), jnp.max(s_n, axis=1, keepdims=True))
    p_c = jnp.exp(s_c - m)
    p_n = jnp.exp(s_n - m)
    l = jnp.sum(p_c, axis=1, keepdims=True) + jnp.sum(p_n, axis=1, keepdims=True)
    o_all = (_dg(p_c.astype(BF16), cv_ref[...].astype(BF16), _NT)
             + _dg(p_n.astype(BF16), v_ref[...].astype(BF16), _NN)) / l
    o = jnp.where(lane_head == 0, o_all[0:t_len], 0.0)
    for h in range(1, n_heads):
        o = o + jnp.where(lane_head == h, o_all[h * t_len:(h + 1) * t_len], 0.0)
    o_ref[...] = (o * _silu(bz_ref[...])).astype(BF16)


def _fox_cached(q, k, v, cache_k, cache_v, cache_logf_t, ccol, crow, b_z, layer):
    bsz, t_len, d = q.shape
    n_heads = d // HEAD_DIM
    past = cache_k.shape[3]
    kern = functools.partial(_fox_cached_kernel, n_heads=n_heads, lane_blk=min(past, 256))
    cur = lambda w: pl.BlockSpec((None, t_len, w), lambda b: (b, 0, 0))
    return pl.pallas_call(
        kern,
        grid=(bsz,),
        in_specs=[cur(d), cur(d), cur(d),
                  pl.BlockSpec((None, None, d, past), lambda b: (layer, b, 0, 0)),
                  pl.BlockSpec((None, None, d, past), lambda b: (layer, b, 0, 0)),
                  pl.BlockSpec((None, None, n_heads, past), lambda b: (layer, b, 0, 0)),
                  cur(LANES),
                  pl.BlockSpec((None, n_heads, t_len), lambda b: (b, 0, 0)),
                  cur(d)],
        out_specs=cur(d),
        out_shape=jax.ShapeDtypeStruct((bsz, t_len, d), BF16),
        compiler_params=_cparams(("parallel",)),
        name="fox_cached",
    )(q, k, v, cache_k, cache_v, cache_logf_t, ccol, crow, b_z)


def _out_kernel(oa_ref, ob_ref, oc_ref, x_ref, w_ref, g_ref, b_ref, y_ref, *, alpha):
    da = oa_ref.shape[-1]
    db = ob_ref.shape[-1]
    h = jnp.dot(oa_ref[...], w_ref[0:da, :], preferred_element_type=F32)
    h = h + jnp.dot(ob_ref[...], w_ref[da:da + db, :], preferred_element_type=F32)
    h = h + jnp.dot(oc_ref[...], w_ref[da + db:, :], preferred_element_type=F32)
    y = alpha * x_ref[...] + h
    mu = jnp.mean(y, axis=-1, keepdims=True)
    yc = y - mu
    var = jnp.mean(yc * yc, axis=-1, keepdims=True)
    y_ref[...] = yc * lax.rsqrt(var + LN_EPS) * g_ref[...] + b_ref[...]


def _out_project(o_a, o_b, o_c, x2, w_all, layer, g, b, alpha, tm):
    n, d = x2.shape
    d_mix = w_all.shape[1]
    row = lambda w: pl.BlockSpec((tm, w), lambda i: (i, 0))
    return pl.pallas_call(
        functools.partial(_out_kernel, alpha=alpha),
        grid=(n // tm,),
        in_specs=[row(o_a.shape[-1]), row(o_b.shape[-1]), row(o_c.shape[-1]), row(d),
                  pl.BlockSpec((None, d_mix, d), lambda i: (layer, 0, 0)),
                  pl.BlockSpec((1, d), lambda i: (0, 0)),
                  pl.BlockSpec((1, d), lambda i: (0, 0))],
        out_specs=row(d),
        out_shape=jax.ShapeDtypeStruct((n, d), F32),
        compiler_params=_cparams(("parallel",)),
        name="proj_out",
    )(o_a, o_b, o_c, x2, w_all, g.reshape(1, d), b.reshape(1, d))


def _diag_blocks(s, n_h):
    bsz = s.shape[0]
    s5 = s.reshape(bsz, n_h, HEAD_DIM, n_h, HEAD_DIM)
    return jnp.stack([s5[:, h, :, h, :] for h in range(n_h)], axis=1)


def _expand_heads(p):
    return jnp.repeat(p.astype(F32), HEAD_DIM).reshape(1, -1)


def _select_matrix(rows, n_cols, first_row):
    e = np.zeros((LANES, n_cols), np.float32)
    for h in range(rows):
        e[first_row + h, h * HEAD_DIM:(h + 1) * HEAD_DIM] = 1.0
    return jnp.asarray(e, BF16)


def _row_tile(n):
    for tm in (256, 128, 64, 32, 16, 8):
        if n % tm == 0:
            return tm
    raise ValueError(f"row count {n} must be a multiple of 8")


def kernel(x_prompt, x_sample, cache_fox_k, cache_fox_v, cache_fox_logf, state_rwkv_shift, state_rwkv_wkv,
           state_gdn_conv, state_gdn_wkv, ln_in_g, ln_in_b, w_in, rwkv_mu, rwkv_w0, rwkv_w2, rwkv_a0, rwkv_a2,
           rwkv_k_k, rwkv_k_a, rwkv_r_k, rwkv_gn_g, rwkv_gn_b, fox_b_f, gdn_conv_w, gdn_a_log, gdn_dt_bias,
           gdn_norm_g, w_out, ln_post_g, ln_post_b):
    depth, d_model, _ = w_in.shape
    bp, seq, _ = x_prompt.shape
    bs, dec_seq, _ = x_sample.shape
    h_fox = fox_b_f.shape[1]
    h_gdn = gdn_a_log.shape[1]
    h_rwkv = rwkv_r_k.shape[1]
    d_rwkv, d_fox, d_gdn = h_rwkv * HEAD_DIM, h_fox * HEAD_DIM, h_gdn * HEAD_DIM
    rank_w = rwkv_w2.shape[1]
    rank_a = rwkv_a2.shape[1]
    n_shift = 3 * d_rwkv + rank_w + rank_a
    n_conv_cols = 3 * d_gdn
    n_conv = gdn_conv_w.shape[1]
    alpha = (2 * depth) ** 0.25
    assert rank_w + rank_a == LANES and h_fox + 2 * h_gdn <= LANES

    split = [n_shift, d_rwkv, d_fox, d_fox, d_fox, h_fox, d_fox, n_conv_cols, h_gdn, h_gdn, d_gdn]
    offs = np.concatenate([[0], np.cumsum(split)])
    seg = lambda i: w_in[:, :, offs[i]:offs[i + 1]]
    small_w = jnp.concatenate([seg(5), seg(8), seg(9)], axis=-1)
    small_w = jnp.pad(small_w, ((0, 0), (0, 0), (0, LANES - small_w.shape[-1])))
    w_in_p = jnp.concatenate([seg(0), seg(1), seg(2), seg(3), seg(4), seg(6), seg(7), seg(10), small_w],
                             axis=-1).astype(BF16)
    widths = [n_shift, d_rwkv, d_fox, d_fox, d_fox, d_fox, n_conv_cols, d_gdn, LANES]
    w_out_b = w_out.astype(BF16)

    zw = jnp.zeros((depth, rank_w, d_rwkv), F32)
    w2a2 = jnp.concatenate([jnp.concatenate([rwkv_w2, zw], axis=-1),
                            jnp.concatenate([zw, rwkv_a2], axis=-1)], axis=1).astype(BF16)
    bf_pad = jnp.pad(fox_b_f, ((0, 0), (0, LANES - h_fox)))
    e_beta = _select_matrix(h_gdn, d_gdn, h_fox)
    e_a = _select_matrix(h_gdn, d_gdn, h_fox + h_gdn)
    e_aux_np = np.zeros((h_fox, N_AUX * LANES, LANES), np.float32)
    for h in range(h_fox):
        for j in range(N_AUX):
            e_aux_np[h, j * LANES + h, (HEAD_DIM if h % 2 == 0 else 0) + j] = -1.0
    e_aux = jnp.asarray(e_aux_np, BF16)
    cache_logf_t = jnp.swapaxes(cache_fox_logf, 2, 3)
    cache_k_t = jnp.swapaxes(cache_fox_k.reshape(depth, bs, -1, d_fox), 2, 3)
    cache_v_t = jnp.swapaxes(cache_fox_v.reshape(depth, bs, -1, d_fox), 2, 3)

    def layer_params(l):
        row = lambda a: a[l].reshape(1, -1).astype(F32)
        return dict(
            rwkv=dict(mu=row(rwkv_mu), w0=row(rwkv_w0), a0=row(rwkv_a0), w2a2=w2a2[l], k_k=row(rwkv_k_k),
                      k_a=row(rwkv_k_a), r_k=row(rwkv_r_k), gn_g=row(rwkv_gn_g), gn_b=row(rwkv_gn_b)),
            gdn=dict(conv_w=gdn_conv_w[l], a_log=_expand_heads(gdn_a_log[l]),
                     dt_bias=_expand_heads(gdn_dt_bias[l]),
                     norm_g=jnp.tile(gdn_norm_g[l].reshape(1, -1), (1, h_gdn)), e_beta=e_beta, e_a=e_a),
        )

    def mix_and_project(l, x2, bsz, t_len, a_sh, a_z, c_qkv, small, c_z, o_b, prev0, s_rwkv0, conv0, s_gdn0):
        lp = layer_params(l)
        c = min(HEAD_DIM, t_len)
        gb = N_STACK // (h_rwkv * c)
        n_chains = max(1, min(N_CHAINS, bsz // gb))
        o_a, s_rwkv = _rwkv_mix(a_sh, a_z, prev0, s_rwkv0, lp["rwkv"], gb, n_chains, c, CHUNK_PASSES)
        gb = N_STACK // (h_gdn * c)
        o_c, s_gdn = _gdn_mix(c_qkv, small, c_z, conv0, s_gdn0, lp["gdn"], gb, n_chains, c, CHUNK_PASSES)
        f2 = lambda a: a.reshape(bsz * t_len, a.shape[-1])
        x_new = _out_project(f2(o_a), f2(o_b), f2(o_c), x2, w_out_b, l, ln_post_g[l], ln_post_b[l], alpha,
                             _row_tile(bsz * t_len))
        outs = (a_sh[:, -1], _diag_blocks(s_rwkv, h_rwkv), c_qkv[:, t_len - (n_conv - 1):],
                _diag_blocks(s_gdn, h_gdn))
        return x_new, outs

    def prompt_layer(l, x2, stacked):
        ln_params = (ln_in_g, ln_in_b) if l == 0 else None
        res = _project_prompt(x2.reshape(bp, seq, d_model), w_in_p, l, widths, bf_pad[l:l + 1], e_aux, h_fox,
                              min(seq, PROJ_ROWS), ln_params, stacked)
        a_sh, a_z, b_z, c_qkv, c_z, small, k_st, v_st, logf_st, qa, ka, va = res[:12]
        if l == 0:
            x2 = res[12].reshape(bp * seq, d_model)
        o_b = _fox_prompt(qa, ka, va, b_z, min(seq, FOX_BLOCK))
        x_new, outs = mix_and_project(l, x2, bp, seq, a_sh, a_z, c_qkv, small, c_z, o_b, *zeros_p)
        return x_new, outs, (k_st, v_st, logf_st)

    def sample_layer(l, x2):
        tm = _row_tile(bs * dec_seq)
        r3 = lambda a: a.reshape(bs, dec_seq, a.shape[-1])
        a_sh, a_z, b_q, b_k, b_v, b_z, c_qkv, c_z, small = map(r3, _project(x2, w_in_p, l, widths, tm))
        logf, ccol, crow = _fox_gates(small, bf_pad[l:l + 1], h_fox, min(dec_seq, 256))
        o_b = _fox_cached(b_q, b_k, b_v, cache_k_t, cache_v_t, cache_logf_t, ccol, crow, b_z, l)
        x_new, outs = mix_and_project(l, x2, bs, dec_seq, a_sh, a_z, c_qkv, small, c_z, o_b,
                                      state_rwkv_shift[l][:, None, :], state_rwkv_wkv[l].reshape(bs, d_rwkv, HEAD_DIM),
                                      state_gdn_conv[l], state_gdn_wkv[l].reshape(bs, d_gdn, HEAD_DIM))
        heads = lambda a: a.reshape(bs, dec_seq, h_fox, HEAD_DIM)
        return x_new, (heads(b_k), heads(b_v), logf) + outs

    xp = x_prompt.reshape(bp * seq, d_model)
    xs = _layer_norm(x_sample.reshape(bs * dec_seq, d_model), ln_in_g, ln_in_b, _row_tile(bs * dec_seq))
    zeros_p = (jnp.zeros((bp, 1, n_shift), F32), jnp.zeros((bp, d_rwkv, HEAD_DIM), F32),
               jnp.zeros((bp, n_conv - 1, n_conv_cols), F32), jnp.zeros((bp, d_gdn, HEAD_DIM), F32))
    outs_p, outs_s, stacked = [], [], None
    for l in range(depth):
        xp, o, stacked = prompt_layer(l, xp, stacked)
        outs_p.append(o)
        xs, o = sample_layer(l, xs)
        outs_s.append(o)
    k_st, v_st, logf_st = stacked
    stack = lambda outs, i: jnp.stack([o[i] for o in outs])
    return ((xp.reshape(bp, seq, d_model), xs.reshape(bs, dec_seq, d_model),
             k_st.reshape(depth, bp, seq, h_fox, HEAD_DIM), v_st.reshape(depth, bp, seq, h_fox, HEAD_DIM), logf_st)
            + tuple(stack(outs_p, i) for i in range(4)) + tuple(stack(outs_s, i) for i in range(7)))
```

```python
import functools
import math

import jax
import jax.numpy as jnp
import numpy as np
from jax import lax
from jax.experimental import pallas as pl
from jax.experimental.pallas import tpu as pltpu

F32 = jnp.float32
BF16 = jnp.bfloat16

HEAD_DIM = 64
LANES = 128
N_STACK = 256
PROJ_ROWS = 512
OUT_ROWS = 1024
FOX_BLOCK = 512
N_CHAINS = 4
CHUNK_PASSES = 1
LN_EPS = 1e-5
RWKV_GN_EPS = 64e-5
GDN_NORM_EPS = 1e-6
L2_EPS = 1e-6
NEG_BIG = -1e30
LOG2E = math.log2(math.e)
VMEM_LIMIT = 56 * 1024 * 1024


def _sigmoid(x):
    return 1.0 / (1.0 + jnp.exp(-x))


def _softplus(x):
    return jnp.maximum(x, 0.0) + jnp.log(1.0 + jnp.exp(-jnp.abs(x)))


def _silu(x):
    return x * _sigmoid(x)


def _split3(x):
    hi = x.astype(BF16)
    r1 = x - hi.astype(F32)
    mid = r1.astype(BF16)
    lo = (r1 - mid.astype(F32)).astype(BF16)
    return hi, mid, lo


_NN = (((1,), (0,)), ((), ()))
_NT = (((1,), (1,)), ((), ()))
_TN = (((0,), (0,)), ((), ()))


def _dg(a, b, dims):
    return lax.dot_general(a, b, dims, preferred_element_type=F32)


def _mm(a, b, dims=_NN, passes=1):
    if passes == 1:
        return _dg(a.astype(BF16), b.astype(BF16), dims)
    ah, am, _ = _split3(a)
    bh, bm, _ = _split3(b)
    return _dg(ah, bh, dims) + (_dg(ah, bm, dims) + _dg(am, bh, dims))


def _mm_exact_rhs(a, e, dims=_NN):
    m = a.shape[0]
    r = _dg(jnp.concatenate(_split3(a), axis=0), e, dims)
    return r[0:m] + (r[m:2 * m] + r[2 * m:3 * m])


def _mm_exact_lhs(e, b, dims=_NN):
    hi, mid, lo = _split3(b)
    return _dg(e, hi, dims) + (_dg(e, mid, dims) + _dg(e, lo, dims))


def _iota(shape, dim):
    return lax.broadcasted_iota(jnp.int32, shape, dim)


def _div_pow2(x, n):
    assert n & (n - 1) == 0
    return lax.shift_right_logical(x, jnp.int32(int(math.log2(n))))


def _head_block_ones(n):
    r = _div_pow2(_iota((n, n), 0), HEAD_DIM)
    c = _div_pow2(_iota((n, n), 1), HEAD_DIM)
    return jnp.where(r == c, 1.0, 0.0).astype(BF16)


def _block_diag(tall):
    n = tall.shape[0]
    wide = jnp.concatenate([tall] * (n // HEAD_DIM), axis=1)
    same = _div_pow2(_iota((n, n), 0), HEAD_DIM) == _div_pow2(_iota((n, n), 1), HEAD_DIM)
    return jnp.where(same, wide, 0.0)


def _stack_heads(x, gb, c, n_heads):
    width = x.shape[1]
    lane_head = _div_pow2(_iota((1, width), 1), HEAD_DIM)
    parts = []
    for b in range(gb):
        xb = x[b * c:(b + 1) * c, :]
        for h in range(n_heads):
            parts.append(jnp.where(lane_head == h, xb, 0.0))
    return jnp.concatenate(parts, axis=0)


def _unstack_heads(xs, gb, c, n_heads):
    outs = []
    for b in range(gb):
        acc = xs[(b * n_heads) * c:(b * n_heads + 1) * c, :]
        for h in range(1, n_heads):
            acc = acc + xs[(b * n_heads + h) * c:(b * n_heads + h + 1) * c, :]
        outs.append(acc)
    return jnp.concatenate(outs, axis=0) if gb > 1 else outs[0]


def _unit_lower_inverse(p, c, passes):
    n = p.shape[0]
    eye = jnp.where(_iota((n, n), 0) == _iota((n, n), 1), 1.0, 0.0)
    t = eye + p
    n_rounds = int(math.log2(c)) - 1
    pk = _mm(p, p, passes=passes)
    yield
    for i in range(n_rounds):
        t_next = t + _mm(t, pk, passes=passes)
        if i + 1 < n_rounds:
            pk = _mm(pk, pk, passes=passes)
        t = t_next
        yield
    return t


def _run_interleaved(chains):
    results = [None] * len(chains)
    active = list(enumerate(chains))
    while active:
        still = []
        for i, ch in active:
            try:
                next(ch)
                still.append((i, ch))
            except StopIteration as stop:
                results[i] = stop.value
        active = still
    return results


def _chunk_masks(n, c):
    row = _iota((n, n), 0)
    col = _iota((n, n), 1)
    same = _div_pow2(row, c) == _div_pow2(col, c)
    strict = jnp.logical_and(same, row > col)
    incl = jnp.logical_and(same, row >= col)
    return strict, incl


def _cumsum_rows(x, gb, c):
    n = gb * c
    strict, incl = _chunk_masks(n, c)
    del strict
    tri = jnp.where(incl, 1.0, 0.0).astype(BF16)
    return _mm_exact_lhs(tri, x)


def _shift_rows(x, j, prev_rows, gb, c):
    n_prev = prev_rows[0].shape[0]
    out = pltpu.roll(x, j, 0)
    rid = _iota((x.shape[0], 1), 0)
    for b in range(gb):
        for t in range(j):
            src = n_prev + t - j
            out = jnp.where(rid == b * c + t, prev_rows[b][src:src + 1, :], out)
    return out


def _cparams(sem):
    return pltpu.CompilerParams(dimension_semantics=sem, vmem_limit_bytes=VMEM_LIMIT)


def _ln_kernel(x_ref, g_ref, b_ref, o_ref):
    x = x_ref[...]
    mu = jnp.mean(x, axis=-1, keepdims=True)
    xc = x - mu
    var = jnp.mean(xc * xc, axis=-1, keepdims=True)
    o_ref[...] = xc * lax.rsqrt(var + LN_EPS) * g_ref[...] + b_ref[...]


def _layer_norm(x2, g, b, tm):
    n, d = x2.shape
    return pl.pallas_call(
        _ln_kernel,
        grid=(n // tm,),
        in_specs=[pl.BlockSpec((tm, d), lambda i: (i, 0)),
                  pl.BlockSpec((1, d), lambda i: (0, 0)),
                  pl.BlockSpec((1, d), lambda i: (0, 0))],
        out_specs=pl.BlockSpec((tm, d), lambda i: (i, 0)),
        out_shape=jax.ShapeDtypeStruct((n, d), F32),
        compiler_params=_cparams(("parallel",)),
        name="ln_in",
    )(x2, g.reshape(1, d), b.reshape(1, d))


def _proj_kernel(x_ref, w_ref, *o_refs):
    xb = x_ref[...].astype(BF16)
    off = 0
    for o_ref in o_refs:
        wdt = o_ref.shape[-1]
        o_ref[...] = jnp.dot(xb, w_ref[:, off:off + wdt], preferred_element_type=F32)
        off += wdt


def _project(x2, w_all, layer, widths, tm):
    n, d = x2.shape
    n_cols = w_all.shape[-1]
    return pl.pallas_call(
        _proj_kernel,
        grid=(n // tm,),
        in_specs=[pl.BlockSpec((tm, d), lambda i: (i, 0)),
                  pl.BlockSpec((None, d, n_cols), lambda i: (layer, 0, 0))],
        out_specs=[pl.BlockSpec((tm, w), lambda i: (i, 0)) for w in widths],
        out_shape=[jax.ShapeDtypeStruct((n, w), F32) for w in widths],
        compiler_params=_cparams(("parallel",)),
        name="proj_in",
    )(x2, w_all)


def _rwkv_kernel(ash_ref, az_ref, prev0_ref, s0_ref, mu_ref, w0_ref, a0_ref, w2a2_ref, kk_ref, ka_ref,
                 rk_ref, gng_ref, gnb_ref, o_ref, s_ref, prev_scr, *, gb, n_chains, c, n_heads, passes):
    @pl.when(pl.program_id(1) == 0)
    def _():
        for b in range(s_ref.shape[0]):
            s_ref[b] = _block_diag(s0_ref[b])
        prev_scr[...] = prev0_ref[...]

    prm = dict(mu=mu_ref[...], w0=w0_ref[...], a0=a0_ref[...], w2a2=w2a2_ref[...], k_k=kk_ref[...],
               k_a=ka_ref[...], r_k=rk_ref[...], gn_g=gng_ref[...], gn_b=gnb_ref[...])
    ins = []
    for ch in range(n_chains):
        b0 = ch * gb
        ins.append((ash_ref[b0:b0 + gb].reshape(gb * c, ash_ref.shape[-1]),
                    az_ref[b0:b0 + gb].reshape(gb * c, az_ref.shape[-1]),
                    [prev_scr[b0 + b] for b in range(gb)],
                    [s_ref[b0 + b] for b in range(gb)]))
    outs = _run_interleaved([_rwkv_chain(*args, prm, gb=gb, c=c, n_heads=n_heads, passes=passes) for args in ins])
    for ch, (o, s_new, last_rows) in enumerate(outs):
        b0 = ch * gb
        o_ref[b0:b0 + gb] = o.reshape(gb, c, o.shape[-1])
        for b in range(gb):
            s_ref[b0 + b] = s_new[b]
            prev_scr[b0 + b] = last_rows[b]


def _rwkv_chain(p, z, prev_rows, s_old, prm, *, gb, c, n_heads, passes):
    d = n_heads * HEAD_DIM
    prev = _shift_rows(p, 1, prev_rows, gb, c)
    last_rows = [p[(b + 1) * c - 1:(b + 1) * c, :] for b in range(gb)]
    xs = p + (prev - p) * prm["mu"]
    r = xs[:, 0:d]
    k = xs[:, d:2 * d]
    v = xs[:, 2 * d:3 * d]
    wa = xs[:, 3 * d:3 * d + LANES]
    lane = _iota((1, LANES), 1)
    wa = jnp.where(lane < HEAD_DIM, jnp.tanh(wa), wa)
    pre = _mm(wa, prm["w2a2"])
    yield
    w_ll = -_softplus(-(prm["w0"] + pre[:, 0:d])) - 0.5
    lw = -jnp.exp(w_ll)
    a = _sigmoid(prm["a0"] + pre[:, d:2 * d])

    ones_bd = _head_block_ones(d)
    kk = k * prm["k_k"]
    kk = kk / jnp.maximum(jnp.sqrt(_mm_exact_rhs(kk * kk, ones_bd)), 1e-12)
    kmod = k * (1.0 + (a - 1.0) * prm["k_a"])
    kka = kk * a

    cum = _cumsum_rows(lw, gb, c)
    yield
    cum_last = jnp.concatenate(
        [jnp.broadcast_to(cum[(b + 1) * c - 1:(b + 1) * c, :], (c, d)) for b in range(gb)], axis=0)
    inv_p = jnp.exp(-cum)
    to_end = jnp.exp(cum_last - cum)
    b_t = kk * jnp.exp(cum - lw)
    a_t = -kka * inv_p
    k_t = kmod * inv_p
    r_t = r * jnp.exp(cum)

    st = functools.partial(_stack_heads, gb=gb, c=c, n_heads=n_heads)
    bs, rs, as_, ks, vs = st(b_t), st(r_t), st(a_t), st(k_t), st(v)
    n = bs.shape[0]
    g = _mm(jnp.concatenate([bs, rs], axis=0), jnp.concatenate([as_, ks], axis=0), _NT, passes)
    yield
    strict, incl = _chunk_masks(n, c)
    m_ab = jnp.where(strict, g[0:n, 0:n], 0.0)
    m_bk = jnp.where(strict, g[0:n, n:2 * n], 0.0)
    m_ra = jnp.where(incl, g[n:2 * n, 0:n], 0.0)
    m_rk = jnp.where(incl, g[n:2 * n, n:2 * n], 0.0)

    hc = n_heads * c
    bh = jnp.concatenate([_mm(bs[b * hc:(b + 1) * hc], s_old[b], _NT, passes) for b in range(gb)], axis=0)
    rh = jnp.concatenate([_mm(rs[b * hc:(b + 1) * hc], s_old[b], _NT, passes) for b in range(gb)], axis=0)
    rhs_u = bh + _mm(m_bk, vs, passes=passes)
    o_s = rh + _mm(m_rk, vs, passes=passes)
    t_inv = yield from _unit_lower_inverse(m_ab, c, passes)
    u = _mm(t_inv, rhs_u, passes=passes)
    yield
    o_s = o_s + _mm(m_ra, u, passes=passes)
    o = _unstack_heads(o_s, gb, c, n_heads)

    a_end = st(-kka * to_end)
    k_end = st(kmod * to_end)
    s_new = []
    for b in range(gb):
        sl = slice(b * hc, (b + 1) * hc)
        p_end = jnp.exp(cum[(b + 1) * c - 1:(b + 1) * c, :])
        s_new.append(s_old[b] * p_end + _mm(u[sl], a_end[sl], _TN, passes)
                     + _mm(vs[sl], k_end[sl], _TN, passes))
    yield

    mean = _mm_exact_rhs(o, ones_bd) * (1.0 / HEAD_DIM)
    oc = o - mean
    yield
    var = _mm_exact_rhs(oc * oc, ones_bd) * (1.0 / HEAD_DIM)
    o = oc * lax.rsqrt(var + RWKV_GN_EPS) * prm["gn_g"] + prm["gn_b"]
    bonus = _mm_exact_rhs(r * kmod * prm["r_k"], ones_bd) * v
    return ((o + bonus) * _silu(z)).astype(BF16), s_new, last_rows


def _rwkv_mix(a_sh, a_z, prev0, s0, lp, gb, n_chains, c, passes):
    bsz, t_len, n_shift = a_sh.shape
    d = a_z.shape[-1]
    n_heads = d // HEAD_DIM
    kern = functools.partial(_rwkv_kernel, gb=gb, n_chains=n_chains, c=c, n_heads=n_heads, passes=passes)
    gb = gb * n_chains
    vec = lambda w: pl.BlockSpec((1, w), lambda b, s: (0, 0))
    return pl.pallas_call(
        kern,
        grid=(bsz // gb, t_len // c),
        in_specs=[pl.BlockSpec((gb, c, n_shift), lambda b, s: (b, s, 0)),
                  pl.BlockSpec((gb, c, d), lambda b, s: (b, s, 0)),
                  pl.BlockSpec((gb, 1, n_shift), lambda b, s: (b, 0, 0)),
                  pl.BlockSpec((gb, d, HEAD_DIM), lambda b, s: (b, 0, 0)),
                  vec(n_shift), vec(d), vec(d),
                  pl.BlockSpec((LANES, 2 * d), lambda b, s: (0, 0)),
                  vec(d), vec(d), vec(d), vec(d), vec(d)],
        out_specs=[pl.BlockSpec((gb, c, d), lambda b, s: (b, s, 0)),
                   pl.BlockSpec((gb, d, d), lambda b, s: (b, 0, 0))],
        out_shape=[jax.ShapeDtypeStruct((bsz, t_len, d), BF16),
                   jax.ShapeDtypeStruct((bsz, d, d), F32)],
        scratch_shapes=[pltpu.VMEM((gb, 1, n_shift), F32)],
        compiler_params=_cparams(("parallel", "arbitrary")),
        name="rwkv_mix",
    )(a_sh, a_z, prev0, s0, lp["mu"], lp["w0"], lp["a0"], lp["w2a2"], lp["k_k"], lp["k_a"], lp["r_k"],
      lp["gn_g"], lp["gn_b"])


def _gdn_kernel(qkv_ref, small_ref, cz_ref, conv0_ref, s0_ref, convw_ref, alog_ref, dtb_ref, ng_ref,
                eb_ref, ea_ref, o_ref, s_ref, conv_scr, *, gb, n_chains, c, n_heads, passes):
    @pl.when(pl.program_id(1) == 0)
    def _():
        for b in range(s_ref.shape[0]):
            s_ref[b] = _block_diag(s0_ref[b])
        conv_scr[...] = conv0_ref[...]

    prm = dict(conv_w=convw_ref[...], a_log=alog_ref[...], dt_bias=dtb_ref[...], norm_g=ng_ref[...],
               e_beta=eb_ref[...], e_a=ea_ref[...])
    ins = []
    for ch in range(n_chains):
        b0 = ch * gb
        ins.append((qkv_ref[b0:b0 + gb].reshape(gb * c, qkv_ref.shape[-1]),
                    small_ref[b0:b0 + gb].reshape(gb * c, LANES),
                    cz_ref[b0:b0 + gb].reshape(gb * c, cz_ref.shape[-1]),
                    [conv_scr[b0 + b] for b in range(gb)],
                    [s_ref[b0 + b] for b in range(gb)]))
    outs = _run_interleaved([_gdn_chain(*args, prm, gb=gb, c=c, n_heads=n_heads, passes=passes) for args in ins])
    for ch, (o, s_new, last_rows) in enumerate(outs):
        b0 = ch * gb
        o_ref[b0:b0 + gb] = o.reshape(gb, c, o.shape[-1])
        for b in range(gb):
            s_ref[b0 + b] = s_new[b]
            conv_scr[b0 + b] = last_rows[b]


def _gdn_chain(x, small, z, prev_rows, s_old, prm, *, gb, c, n_heads, passes):
    d = n_heads * HEAD_DIM
    conv_w = prm["conv_w"]
    n_conv = conv_w.shape[0]
    y = x * conv_w[n_conv - 1:n_conv, :]
    for j in range(1, n_conv):
        y = y + _shift_rows(x, j, prev_rows, gb, c) * conv_w[n_conv - 1 - j:n_conv - j, :]
    last_rows = [x[(b + 1) * c - (n_conv - 1):(b + 1) * c, :] for b in range(gb)]
    y = _silu(y)
    q = y[:, 0:d]
    k = y[:, d:2 * d]
    v = y[:, 2 * d:3 * d]
    ones_bd = _head_block_ones(d)
    q = q * lax.rsqrt(_mm_exact_rhs(q * q, ones_bd) + L2_EPS) * (HEAD_DIM ** -0.5)
    k = k * lax.rsqrt(_mm_exact_rhs(k * k, ones_bd) + L2_EPS)

    beta = _sigmoid(_mm_exact_rhs(small, prm["e_beta"]))
    g = -jnp.exp(prm["a_log"]) * _softplus(_mm_exact_rhs(small, prm["e_a"]) + prm["dt_bias"])
    yield
    gc = _cumsum_rows(g, gb, c)
    yield
    gc_last = jnp.concatenate(
        [jnp.broadcast_to(gc[(b + 1) * c - 1:(b + 1) * c, :], (c, d)) for b in range(gb)], axis=0)
    egc = jnp.exp(gc)

    st = functools.partial(_stack_heads, gb=gb, c=c, n_heads=n_heads)
    kb = k * beta
    kbs, ks, qs = st(kb), st(k), st(q)
    n = kbs.shape[0]
    strict, incl = _chunk_masks(n, c)
    gcol = jnp.min(st(gc), axis=1, keepdims=True)
    eye = _iota((n, n), 0) == _iota((n, n), 1)
    grow = jnp.sum(jnp.where(eye, jnp.broadcast_to(gcol, (n, n)), 0.0), axis=0, keepdims=True)
    dmat = jnp.exp(jnp.where(incl, gcol - grow, 0.0))
    g2 = _mm(jnp.concatenate([kbs, qs], axis=0), ks, _NT, passes)
    yield
    m = jnp.where(strict, g2[0:n] * dmat, 0.0)
    qk = jnp.where(incl, g2[n:2 * n] * dmat, 0.0)
    t_inv = yield from _unit_lower_inverse(-m, c, passes)
    rhs = jnp.concatenate([st(v * beta), st(kb * egc)], axis=1)
    sol = _mm(t_inv, rhs, passes=passes)
    yield
    u = sol[:, 0:d]
    w = sol[:, d:2 * d]

    hc = n_heads * c
    qgs = st(q * egc)
    kds = st(k * jnp.exp(gc_last - gc))
    v_new = u - jnp.concatenate(
        [_mm(w[b * hc:(b + 1) * hc], s_old[b], passes=passes) for b in range(gb)], axis=0)
    o_s = jnp.concatenate(
        [_mm(qgs[b * hc:(b + 1) * hc], s_old[b], passes=passes) for b in range(gb)], axis=0)
    yield
    o_s = o_s + _mm(qk, v_new, passes=passes)
    s_new = []
    for b in range(gb):
        sl = slice(b * hc, (b + 1) * hc)
        gl = jnp.exp(gc[(b + 1) * c - 1:(b + 1) * c, :])
        s_new.append(s_old[b] * gl + _mm(kds[sl], v_new[sl], _TN, passes))
    yield

    o = _unstack_heads(o_s, gb, c, n_heads)
    ms = _mm_exact_rhs(o * o, ones_bd) * (1.0 / HEAD_DIM)
    o = o * lax.rsqrt(ms + GDN_NORM_EPS) * prm["norm_g"]
    return (o * _silu(z)).astype(BF16), s_new, last_rows


def _gdn_mix(c_qkv, small, c_z, conv0, s0, lp, gb, n_chains, c, passes):
    bsz, t_len, n_qkv = c_qkv.shape
    d = c_z.shape[-1]
    n_heads = d // HEAD_DIM
    n_conv = lp["conv_w"].shape[0]
    kern = functools.partial(_gdn_kernel, gb=gb, n_chains=n_chains, c=c, n_heads=n_heads, passes=passes)
    gb = gb * n_chains
    vec = lambda w: pl.BlockSpec((1, w), lambda b, s: (0, 0))
    return pl.pallas_call(
        kern,
        grid=(bsz // gb, t_len // c),
        in_specs=[pl.BlockSpec((gb, c, n_qkv), lambda b, s: (b, s, 0)),
                  pl.BlockSpec((gb, c, LANES), lambda b, s: (b, s, 0)),
                  pl.BlockSpec((gb, c, d), lambda b, s: (b, s, 0)),
                  pl.BlockSpec((gb, n_conv - 1, n_qkv), lambda b, s: (b, 0, 0)),
                  pl.BlockSpec((gb, d, HEAD_DIM), lambda b, s: (b, 0, 0)),
                  pl.BlockSpec((n_conv, n_qkv), lambda b, s: (0, 0)),
                  vec(d), vec(d), vec(d),
                  pl.BlockSpec((LANES, d), lambda b, s: (0, 0)),
                  pl.BlockSpec((LANES, d), lambda b, s: (0, 0))],
        out_specs=[pl.BlockSpec((gb, c, d), lambda b, s: (b, s, 0)),
                   pl.BlockSpec((gb, d, d), lambda b, s: (b, 0, 0))],
        out_shape=[jax.ShapeDtypeStruct((bsz, t_len, d), BF16),
                   jax.ShapeDtypeStruct((bsz, d, d), F32)],
        scratch_shapes=[pltpu.VMEM((gb, n_conv - 1, n_qkv), F32)],
        compiler_params=_cparams(("parallel", "arbitrary")),
        name="gdn_mix",
    )(c_qkv, small, c_z, conv0, s0, lp["conv_w"], lp["a_log"], lp["dt_bias"], lp["norm_g"],
      lp["e_beta"], lp["e_a"])


def _log_forget_cumsum(small, bf, carry_scr):
    @pl.when(pl.program_id(1) == 0)
    def _():
        carry_scr[...] = jnp.zeros_like(carry_scr)

    logf = -_softplus(-(small + bf))
    tb = logf.shape[0]
    c = _cumsum_rows(logf, 1, tb) + carry_scr[...]
    carry_scr[...] = c[tb - 1:tb, :]
    return logf, c


def _fox_gate_kernel(small_ref, bf_ref, logf_ref, ccol_ref, crow_ref, carry_scr, *, n_heads):
    logf, c = _log_forget_cumsum(small_ref[...], bf_ref[...], carry_scr)
    logf_ref[...] = logf[:, 0:n_heads]
    ccol_ref[...] = c
    sel = jnp.where(_iota((n_heads, LANES), 0) == _iota((n_heads, LANES), 1), 1.0, 0.0).astype(BF16)
    crow_ref[...] = _mm_exact_lhs(sel, c, _NT)


def _fox_gates(small, bf_pad, n_heads, tb):
    bsz, t_len, _ = small.shape
    kern = functools.partial(_fox_gate_kernel, n_heads=n_heads)
    return pl.pallas_call(
        kern,
        grid=(bsz, t_len // tb),
        in_specs=[pl.BlockSpec((None, tb, LANES), lambda b, s: (b, s, 0)),
                  pl.BlockSpec((1, LANES), lambda b, s: (0, 0))],
        out_specs=[pl.BlockSpec((None, tb, n_heads), lambda b, s: (b, s, 0)),
                   pl.BlockSpec((None, tb, LANES), lambda b, s: (b, s, 0)),
                   pl.BlockSpec((None, n_heads, tb), lambda b, s: (b, 0, s))],
        out_shape=[jax.ShapeDtypeStruct((bsz, t_len, n_heads), F32),
                   jax.ShapeDtypeStruct((bsz, t_len, LANES), F32),
                   jax.ShapeDtypeStruct((bsz, n_heads, t_len), F32)],
        scratch_shapes=[pltpu.VMEM((1, LANES), F32)],
        compiler_params=_cparams(("parallel", "arbitrary")),
        name="fox_gates",
    )(small, bf_pad)


N_AUX = 3


def _proj_prompt_kernel(*refs, widths, n_heads, ln_in, n_alias):
    refs = list(refs)
    x_ref, w_ref, bf_ref, eaux_ref = refs[0:4]
    pos = 4
    if ln_in:
        g_ref, b_ref = refs[pos:pos + 2]
        pos += 2
    pos += n_alias
    (ash_ref, az_ref, bz_ref, cqkv_ref, cz_ref, small_ref, k_ref, v_ref, logf_ref,
     qa_ref, ka_ref, va_ref) = refs[pos:pos + 12]
    pos += 12
    if ln_in:
        xn_ref = refs[pos]
        pos += 1
    carry_scr = refs[pos]

    x = x_ref[...]
    if ln_in:
        mu = jnp.mean(x, axis=-1, keepdims=True)
        xc = x - mu
        var = jnp.mean(xc * xc, axis=-1, keepdims=True)
        x = xc * lax.rsqrt(var + LN_EPS) * g_ref[...] + b_ref[...]
        xn_ref[...] = x
    xb = x.astype(BF16)
    offs = np.concatenate([[0], np.cumsum(widths)])
    seg = lambda i: jnp.dot(xb, w_ref[:, int(offs[i]):int(offs[i + 1])], preferred_element_type=F32)
    ash_ref[...] = seg(0)
    az_ref[...] = seg(1)
    q, k, v = seg(2), seg(3), seg(4)
    bz_ref[...] = seg(5)
    cqkv_ref[...] = seg(6)
    cz_ref[...] = seg(7)
    small = seg(8)
    small_ref[...] = small
    k_ref[...] = k
    v_ref[...] = v

    logf, c = _log_forget_cumsum(small, bf_ref[...], carry_scr)
    logf_ref[...] = logf[:, 0:n_heads]
    parts = jnp.concatenate(_split3(c * LOG2E), axis=1)
    lane = _iota((1, LANES), 1)
    for h in range(n_heads):
        pair, hh = divmod(h, 2)
        own = (lane < HEAD_DIM) if hh == 0 else (lane >= HEAD_DIM)
        a0 = HEAD_DIM if hh == 0 else 0
        aux_q = jnp.where((lane >= a0) & (lane < a0 + N_AUX), 1.0, 0.0)
        aux_k = _dg(parts, eaux_ref[h], _NN)
        aux_v = jnp.where(lane == a0, 1.0, 0.0)
        cols = slice(pair * LANES, (pair + 1) * LANES)
        qa_ref[h] = jnp.where(own, q[:, cols] * (HEAD_DIM ** -0.5 * LOG2E), aux_q).astype(BF16)
        ka_ref[h] = jnp.where(own, k[:, cols], aux_k).astype(BF16)
        va_ref[h] = jnp.where(own, v[:, cols], aux_v).astype(BF16)


def _project_prompt(x3, w_all, layer, widths, bf_pad, e_aux, n_heads, tm, ln_params, stacked):
    bsz, t_len, d = x3.shape
    depth, _, n_cols = w_all.shape
    d_fox = n_heads * HEAD_DIM
    ln_in = ln_params is not None
    n_alias = 0 if stacked is None else len(stacked)
    kern = functools.partial(_proj_prompt_kernel, widths=tuple(widths), n_heads=n_heads, ln_in=ln_in,
                             n_alias=n_alias)
    row = lambda w: pl.BlockSpec((None, tm, w), lambda b, s: (b, s, 0))
    per_layer = lambda w: pl.BlockSpec((None, None, tm, w), lambda b, s: (layer, b, s, 0))
    per_head = pl.BlockSpec((None, n_heads, tm, LANES), lambda b, s: (b, 0, s, 0))
    const = lambda shape: pl.BlockSpec(shape, lambda b, s: (0,) * len(shape))
    in_specs = [row(d), pl.BlockSpec((None, d, n_cols), lambda b, s: (layer, 0, 0), pipeline_mode=pl.Buffered(1)),
                const((1, LANES)), const((n_heads, N_AUX * LANES, LANES))]
    args = [x3, w_all, bf_pad, e_aux]
    if ln_in:
        in_specs += [const((1, d)), const((1, d))]
        args += [p.reshape(1, d) for p in ln_params]
    aliases = {}
    if stacked is not None:
        for j, buf in enumerate(stacked):
            aliases[len(args)] = 6 + j
            in_specs.append(pl.BlockSpec(memory_space=pl.ANY))
            args.append(buf)
    f32 = lambda *shape: jax.ShapeDtypeStruct(shape, F32)
    out_specs = [row(widths[0]), row(widths[1]), row(widths[5]), row(widths[6]), row(widths[7]), row(LANES),
                 per_layer(d_fox), per_layer(d_fox), per_layer(n_heads), per_head, per_head, per_head]
    out_shape = [f32(bsz, t_len, widths[0]), f32(bsz, t_len, widths[1]), f32(bsz, t_len, widths[5]),
                 f32(bsz, t_len, widths[6]), f32(bsz, t_len, widths[7]), f32(bsz, t_len, LANES),
                 f32(depth, bsz, t_len, d_fox), f32(depth, bsz, t_len, d_fox), f32(depth, bsz, t_len, n_heads)]
    out_shape += [jax.ShapeDtypeStruct((bsz, n_heads, t_len, LANES), BF16)] * 3
    if ln_in:
        out_specs.append(row(d))
        out_shape.append(f32(bsz, t_len, d))
    return pl.pallas_call(
        kern,
        grid=(bsz, t_len // tm),
        in_specs=in_specs,
        out_specs=out_specs,
        out_shape=out_shape,
        input_output_aliases=aliases,
        scratch_shapes=[pltpu.VMEM((1, LANES), F32)],
        compiler_params=_cparams(("parallel", "arbitrary")),
        name="proj_prompt",
    )(*args)


def _fox_prompt_kernel(qa_ref, ka_ref, va_ref, bz_ref, o_ref, s_scr, m_scr, acc_scr, *, blk):
    qi = pl.program_id(2)
    m_scr[...] = jnp.full_like(m_scr, NEG_BIG)
    acc_scr[...] = jnp.zeros_like(acc_scr)

    def logits(ki, slot):
        k0 = pl.multiple_of(ki * blk, blk)
        for hh in range(2):
            s_scr[slot, hh] = _dg(qa_ref[hh], ka_ref[hh, pl.ds(k0, blk), :], _NT)

    def consume(ki, slot, masked):
        k0 = pl.multiple_of(ki * blk, blk)
        for hh in range(2):
            s = s_scr[slot, hh]
            if masked:
                s = jnp.where(_iota((blk, blk), 1) <= _iota((blk, blk), 0), s, NEG_BIG)
            m_prev = m_scr[hh]
            m_new = jnp.maximum(m_prev, jnp.max(s, axis=1, keepdims=True))
            alpha = jnp.exp2(m_prev - m_new)
            p = jnp.exp2(s - jnp.tile(m_new, (1, blk // LANES)))
            acc_scr[hh] = alpha * acc_scr[hh] + _dg(p.astype(BF16), va_ref[hh, pl.ds(k0, blk), :], _NN)
            m_scr[hh] = m_new

    def two_blocks(j, carry):
        logits(2 * j + 1, 1)
        consume(2 * j, 0, False)
        logits(2 * j + 2, 0)
        consume(2 * j + 1, 1, False)
        return carry

    logits(0, 0)
    lax.fori_loop(0, qi // 2, two_blocks, 0)

    @pl.when(qi % 2 == 0)
    def _():
        consume(qi, 0, True)

    @pl.when(qi % 2 == 1)
    def _():
        logits(qi, 1)
        consume(qi - 1, 0, False)
        consume(qi, 1, True)

    lane = _iota((1, LANES), 1)
    acc0 = acc_scr[0]
    acc1 = acc_scr[1]
    o = jnp.where(lane < HEAD_DIM, acc0 / acc0[:, HEAD_DIM:HEAD_DIM + 1], acc1 / acc1[:, 0:1])
    o_ref[...] = (o * _silu(bz_ref[...])).astype(BF16)


def _fox_prompt(qa, ka, va, b_z, blk):
    bsz, n_heads, t_len, _ = qa.shape
    d = b_z.shape[-1]
    n_pairs = n_heads // 2
    return pl.pallas_call(
        functools.partial(_fox_prompt_kernel, blk=blk),
        grid=(bsz, n_pairs, t_len // blk),
        in_specs=[pl.BlockSpec((None, 2, blk, LANES), lambda b, p, i: (b, p, i, 0)),
                  pl.BlockSpec((None, 2, t_len, LANES), lambda b, p, i: (b, p, 0, 0)),
                  pl.BlockSpec((None, 2, t_len, LANES), lambda b, p, i: (b, p, 0, 0)),
                  pl.BlockSpec((None, blk, LANES), lambda b, p, i: (b, i, p))],
        out_specs=pl.BlockSpec((None, blk, LANES), lambda b, p, i: (b, i, p)),
        out_shape=jax.ShapeDtypeStruct((bsz, t_len, d), BF16),
        scratch_shapes=[pltpu.VMEM((2, 2, blk, blk), F32),
                        pltpu.VMEM((2, blk, LANES), F32),
                        pltpu.VMEM((2, blk, LANES), F32)],
        compiler_params=_cparams(("parallel", "parallel", "arbitrary")),
        name="fox_prompt",
    )(qa, ka, va, b_z)


def _fox_cached_kernel(q_ref, k_ref, v_ref, ck_ref, cv_ref, clf_ref, ccol_ref, crow_ref, bz_ref, o_ref,
                       *, n_heads, lane_blk):
    t_len, d = q_ref.shape
    past = ck_ref.shape[1]
    n_rows = n_heads * t_len

    triu = jnp.where(_iota((lane_blk, lane_blk), 0) <= _iota((lane_blk, lane_blk), 1), 1.0, 0.0).astype(BF16)
    carry = jnp.zeros((n_heads, 1), F32)
    pieces = []
    for j in range(past // lane_blk):
        cj = _mm_exact_rhs(clf_ref[:, j * lane_blk:(j + 1) * lane_blk], triu) + carry
        carry = cj[:, lane_blk - 1:lane_blk]
        pieces.append(cj)
    c_cache = jnp.concatenate(pieces, axis=1)
    total = carry

    rows = lambda f: jnp.concatenate([f(h) for h in range(n_heads)], axis=0)
    ck_cache = rows(lambda h: jnp.broadcast_to(c_cache[h:h + 1, :], (t_len, past)))
    ck_new = rows(lambda h: jnp.broadcast_to(total[h:h + 1, :] + crow_ref[h:h + 1, :], (t_len, t_len)))
    cq = rows(lambda h: total[h:h + 1, :] + ccol_ref[:, h:h + 1])

    lane_head = _div_pow2(_iota((1, d), 1), HEAD_DIM)
    q = q_ref[...] * (HEAD_DIM ** -0.5)
    q_bd = rows(lambda h: jnp.where(lane_head == h, q, 0.0)).astype(BF16)
    s_c = _dg(q_bd, ck_ref[...].astype(BF16), _NN) + cq - ck_cache
    s_n = _dg(q_bd, k_ref[...].astype(BF16), _NT) + cq - ck_new
    q_pos = _iota((n_rows, t_len), 0) & (t_len - 1)
    s_n = jnp.where(_iota((n_rows, t_len), 1) <= q_pos, s_n, NEG_BIG)
    m = jnp.maximum(jnp.max(s_c, axis=1, keepdims=True), jnp.max(s_n, axis=1, keepdims=True))
    p_c = jnp.exp(s_c - m)
    p_n = jnp.exp(s_n - m)
    l = jnp.sum(p_c, axis=1, keepdims=True) + jnp.sum(p_n, axis=1, keepdims=True)
    o_all = (_dg(p_c.astype(BF16), cv_ref[...].astype(BF16), _NT)
             + _dg(p_n.astype(BF16), v_ref[...].astype(BF16), _NN)) / l
    o = jnp.where(lane_head == 0, o_all[0:t_len], 0.0)
    for h in range(1, n_heads):
        o = o + jnp.where(lane_head == h, o_all[h * t_len:(h + 1) * t_len], 0.0)
    o_ref[...] = (o * _silu(bz_ref[...])).astype(BF16)


def _fox_cached(q, k, v, cache_k, cache_v, cache_logf_t, ccol, crow, b_z, layer):
    bsz, t_len, d = q.shape
    n_heads = d // HEAD_DIM
    past = cache_k.shape[3]
    kern = functools.partial(_fox_cached_kernel, n_heads=n_heads, lane_blk=min(past, 256))
    cur = lambda w: pl.BlockSpec((None, t_len, w), lambda b: (b, 0, 0))
    return pl.pallas_call(
        kern,
        grid=(bsz,),
        in_specs=[cur(d), cur(d), cur(d),
                  pl.BlockSpec((None, None, d, past), lambda b: (layer, b, 0, 0)),
                  pl.BlockSpec((None, None, d, past), lambda b: (layer, b, 0, 0)),
                  pl.BlockSpec((None, None, n_heads, past), lambda b: (layer, b, 0, 0)),
                  cur(LANES),
                  pl.BlockSpec((None, n_heads, t_len), lambda b: (b, 0, 0)),
                  cur(d)],
        out_specs=cur(d),
        out_shape=jax.ShapeDtypeStruct((bsz, t_len, d), BF16),
        compiler_params=_cparams(("parallel",)),
        name="fox_cached",
    )(q, k, v, cache_k, cache_v, cache_logf_t, ccol, crow, b_z)


def _out_kernel(oa_ref, ob_ref, oc_ref, x_ref, w_ref, g_ref, b_ref, y_ref, *, alpha):
    da = oa_ref.shape[-1]
    db = ob_ref.shape[-1]
    h = jnp.dot(oa_ref[...], w_ref[0:da, :], preferred_element_type=F32)
    h = h + jnp.dot(ob_ref[...], w_ref[da:da + db, :], preferred_element_type=F32)
    h = h + jnp.dot(oc_ref[...], w_ref[da + db:, :], preferred_element_type=F32)
    y = alpha * x_ref[...] + h
    mu = jnp.mean(y, axis=-1, keepdims=True)
    yc = y - mu
    var = jnp.mean(yc * yc, axis=-1, keepdims=True)
    y_ref[...] = yc * lax.rsqrt(var + LN_EPS) * g_ref[...] + b_ref[...]


def _out_project(o_a, o_b, o_c, x2, w_all, layer, g, b, alpha, tm):
    n, d = x2.shape
    d_mix = w_all.shape[1]
    row = lambda w: pl.BlockSpec((tm, w), lambda i: (i, 0))
    return pl.pallas_call(
        functools.partial(_out_kernel, alpha=alpha),
        grid=(n // tm,),
        in_specs=[row(o_a.shape[-1]), row(o_b.shape[-1]), row(o_c.shape[-1]), row(d),
                  pl.BlockSpec((None, d_mix, d), lambda i: (layer, 0, 0)),
                  pl.BlockSpec((1, d), lambda i: (0, 0)),
                  pl.BlockSpec((1, d), lambda i: (0, 0))],
        out_specs=row(d),
        out_shape=jax.ShapeDtypeStruct((n, d), F32),
        compiler_params=_cparams(("parallel",)),
        name="proj_out",
    )(o_a, o_b, o_c, x2, w_all, g.reshape(1, d), b.reshape(1, d))


def _diag_blocks(s, n_h):
    bsz = s.shape[0]
    s5 = s.reshape(bsz, n_h, HEAD_DIM, n_h, HEAD_DIM)
    return jnp.stack([s5[:, h, :, h, :] for h in range(n_h)], axis=1)


def _expand_heads(p):
    return jnp.repeat(p.astype(F32), HEAD_DIM).reshape(1, -1)


def _select_matrix(rows, n_cols, first_row):
    e = np.zeros((LANES, n_cols), np.float32)
    for h in range(rows):
        e[first_row + h, h * HEAD_DIM:(h + 1) * HEAD_DIM] = 1.0
    return jnp.asarray(e, BF16)


def _row_tile(n, largest=256):
    for tm in (1024, 512, 256, 128, 64, 32, 16, 8):
        if tm <= largest and n % tm == 0:
            return tm
    raise ValueError(f"row count {n} must be a multiple of 8")


def kernel(x_prompt, x_sample, cache_fox_k, cache_fox_v, cache_fox_logf, state_rwkv_shift, state_rwkv_wkv,
           state_gdn_conv, state_gdn_wkv, ln_in_g, ln_in_b, w_in, rwkv_mu, rwkv_w0, rwkv_w2, rwkv_a0, rwkv_a2,
           rwkv_k_k, rwkv_k_a, rwkv_r_k, rwkv_gn_g, rwkv_gn_b, fox_b_f, gdn_conv_w, gdn_a_log, gdn_dt_bias,
           gdn_norm_g, w_out, ln_post_g, ln_post_b):
    depth, d_model, _ = w_in.shape
    bp, seq, _ = x_prompt.shape
    bs, dec_seq, _ = x_sample.shape
    h_fox = fox_b_f.shape[1]
    h_gdn = gdn_a_log.shape[1]
    h_rwkv = rwkv_r_k.shape[1]
    d_rwkv, d_fox, d_gdn = h_rwkv * HEAD_DIM, h_fox * HEAD_DIM, h_gdn * HEAD_DIM
    rank_w = rwkv_w2.shape[1]
    rank_a = rwkv_a2.shape[1]
    n_shift = 3 * d_rwkv + rank_w + rank_a
    n_conv_cols = 3 * d_gdn
    n_conv = gdn_conv_w.shape[1]
    alpha = (2 * depth) ** 0.25
    assert rank_w + rank_a == LANES and h_fox + 2 * h_gdn <= LANES

    split = [n_shift, d_rwkv, d_fox, d_fox, d_fox, h_fox, d_fox, n_conv_cols, h_gdn, h_gdn, d_gdn]
    offs = np.concatenate([[0], np.cumsum(split)])
    seg = lambda i: w_in[:, :, offs[i]:offs[i + 1]]
    small_w = jnp.concatenate([seg(5), seg(8), seg(9)], axis=-1)
    small_w = jnp.pad(small_w, ((0, 0), (0, 0), (0, LANES - small_w.shape[-1])))
    w_in_p = jnp.concatenate([seg(0), seg(1), seg(2), seg(3), seg(4), seg(6), seg(7), seg(10), small_w],
                             axis=-1).astype(BF16)
    widths = [n_shift, d_rwkv, d_fox, d_fox, d_fox, d_fox, n_conv_cols, d_gdn, LANES]
    w_out_b = w_out.astype(BF16)

    zw = jnp.zeros((depth, rank_w, d_rwkv), F32)
    w2a2 = jnp.concatenate([jnp.concatenate([rwkv_w2, zw], axis=-1),
                            jnp.concatenate([zw, rwkv_a2], axis=-1)], axis=1).astype(BF16)
    bf_pad = jnp.pad(fox_b_f, ((0, 0), (0, LANES - h_fox)))
    e_beta = _select_matrix(h_gdn, d_gdn, h_fox)
    e_a = _select_matrix(h_gdn, d_gdn, h_fox + h_gdn)
    e_aux_np = np.zeros((h_fox, N_AUX * LANES, LANES), np.float32)
    for h in range(h_fox):
        for j in range(N_AUX):
            e_aux_np[h, j * LANES + h, (HEAD_DIM if h % 2 == 0 else 0) + j] = -1.0
    e_aux = jnp.asarray(e_aux_np, BF16)
    cache_logf_t = jnp.swapaxes(cache_fox_logf, 2, 3)
    cache_k_t = jnp.swapaxes(cache_fox_k.reshape(depth, bs, -1, d_fox), 2, 3)
    cache_v_t = jnp.swapaxes(cache_fox_v.reshape(depth, bs, -1, d_fox), 2, 3)

    def layer_params(l):
        row = lambda a: a[l].reshape(1, -1).astype(F32)
        return dict(
            rwkv=dict(mu=row(rwkv_mu), w0=row(rwkv_w0), a0=row(rwkv_a0), w2a2=w2a2[l], k_k=row(rwkv_k_k),
                      k_a=row(rwkv_k_a), r_k=row(rwkv_r_k), gn_g=row(rwkv_gn_g), gn_b=row(rwkv_gn_b)),
            gdn=dict(conv_w=gdn_conv_w[l], a_log=_expand_heads(gdn_a_log[l]),
                     dt_bias=_expand_heads(gdn_dt_bias[l]),
                     norm_g=jnp.tile(gdn_norm_g[l].reshape(1, -1), (1, h_gdn)), e_beta=e_beta, e_a=e_a),
        )

    def mix_and_project(l, x2, bsz, t_len, a_sh, a_z, c_qkv, small, c_z, o_b, prev0, s_rwkv0, conv0, s_gdn0):
        lp = layer_params(l)
        c = min(HEAD_DIM, t_len)
        gb = N_STACK // (h_rwkv * c)
        n_chains = max(1, min(N_CHAINS, bsz // gb))
        o_a, s_rwkv = _rwkv_mix(a_sh, a_z, prev0, s_rwkv0, lp["rwkv"], gb, n_chains, c, CHUNK_PASSES)
        gb = N_STACK // (h_gdn * c)
        o_c, s_gdn = _gdn_mix(c_qkv, small, c_z, conv0, s_gdn0, lp["gdn"], gb, n_chains, c, CHUNK_PASSES)
        f2 = lambda a: a.reshape(bsz * t_len, a.shape[-1])
        x_new = _out_project(f2(o_a), f2(o_b), f2(o_c), x2, w_out_b, l, ln_post_g[l], ln_post_b[l], alpha,
                             _row_tile(bsz * t_len, OUT_ROWS))
        outs = (a_sh[:, -1], _diag_blocks(s_rwkv, h_rwkv), c_qkv[:, t_len - (n_conv - 1):],
                _diag_blocks(s_gdn, h_gdn))
        return x_new, outs

    def prompt_layer(l, x2, stacked):
        ln_params = (ln_in_g, ln_in_b) if l == 0 else None
        res = _project_prompt(x2.reshape(bp, seq, d_model), w_in_p, l, widths, bf_pad[l:l + 1], e_aux, h_fox,
                              min(seq, PROJ_ROWS), ln_params, stacked)
        a_sh, a_z, b_z, c_qkv, c_z, small, k_st, v_st, logf_st, qa, ka, va = res[:12]
        if l == 0:
            x2 = res[12].reshape(bp * seq, d_model)
        o_b = _fox_prompt(qa, ka, va, b_z, min(seq, FOX_BLOCK))
        x_new, outs = mix_and_project(l, x2, bp, seq, a_sh, a_z, c_qkv, small, c_z, o_b, *zeros_p)
        return x_new, outs, (k_st, v_st, logf_st)

    def sample_layer(l, x2):
        tm = _row_tile(bs * dec_seq)
        r3 = lambda a: a.reshape(bs, dec_seq, a.shape[-1])
        a_sh, a_z, b_q, b_k, b_v, b_z, c_qkv, c_z, small = map(r3, _project(x2, w_in_p, l, widths, tm))
        logf, ccol, crow = _fox_gates(small, bf_pad[l:l + 1], h_fox, min(dec_seq, 256))
        o_b = _fox_cached(b_q, b_k, b_v, cache_k_t, cache_v_t, cache_logf_t, ccol, crow, b_z, l)
        x_new, outs = mix_and_project(l, x2, bs, dec_seq, a_sh, a_z, c_qkv, small, c_z, o_b,
                                      state_rwkv_shift[l][:, None, :], state_rwkv_wkv[l].reshape(bs, d_rwkv, HEAD_DIM),
                                      state_gdn_conv[l], state_gdn_wkv[l].reshape(bs, d_gdn, HEAD_DIM))
        heads = lambda a: a.reshape(bs, dec_seq, h_fox, HEAD_DIM)
        return x_new, (heads(b_k), heads(b_v), logf) + outs

    xp = x_prompt.reshape(bp * seq, d_model)
    xs = _layer_norm(x_sample.reshape(bs * dec_seq, d_model), ln_in_g, ln_in_b, _row_tile(bs * dec_seq))
    zeros_p = (jnp.zeros((bp, 1, n_shift), F32), jnp.zeros((bp, d_rwkv, HEAD_DIM), F32),
               jnp.zeros((bp, n_conv - 1, n_conv_cols), F32), jnp.zeros((bp, d_gdn, HEAD_DIM), F32))
    outs_p, outs_s, stacked = [], [], None
    for l in range(depth):
        xp, o, stacked = prompt_layer(l, xp, stacked)
        outs_p.append(o)
        xs, o = sample_layer(l, xs)
        outs_s.append(o)
    k_st, v_st, logf_st = stacked
    stack = lambda outs, i: jnp.stack([o[i] for o in outs])
    return ((xp.reshape(bp, seq, d_model), xs.reshape(bs, dec_seq, d_model),
             k_st.reshape(depth, bp, seq, h_fox, HEAD_DIM), v_st.reshape(depth, bp, seq, h_fox, HEAD_DIM), logf_st)
            + tuple(stack(outs_p, i) for i in range(4)) + tuple(stack(outs_s, i) for i in range(7)))
```

```python
import functools
import math

import jax
import jax.numpy as jnp
import numpy as np
from jax import lax
from jax.experimental import pallas as pl
from jax.experimental.pallas import tpu as pltpu

F32 = jnp.float32
BF16 = jnp.bfloat16

HEAD_DIM = 64
LANES = 128
N_STACK = 256
PROJ_ROWS = 512
OUT_ROWS = 1024
FOX_BLOCK = 512
N_CHAINS = 8
CHUNK_PASSES = 1
LN_EPS = 1e-5
RWKV_GN_EPS = 64e-5
GDN_NORM_EPS = 1e-6
L2_EPS = 1e-6
NEG_BIG = -1e30
LOG2E = math.log2(math.e)
VMEM_LIMIT = 56 * 1024 * 1024


def _sigmoid(x):
    return 1.0 / (1.0 + jnp.exp(-x))


def _softplus(x):
    return jnp.maximum(x, 0.0) + jnp.log(1.0 + jnp.exp(-jnp.abs(x)))


def _silu(x):
    return x * _sigmoid(x)


def _split3(x):
    hi = x.astype(BF16)
    r1 = x - hi.astype(F32)
    mid = r1.astype(BF16)
    lo = (r1 - mid.astype(F32)).astype(BF16)
    return hi, mid, lo


_NN = (((1,), (0,)), ((), ()))
_NT = (((1,), (1,)), ((), ()))
_TN = (((0,), (0,)), ((), ()))


def _dg(a, b, dims):
    return lax.dot_general(a, b, dims, preferred_element_type=F32)


def _mm(a, b, dims=_NN, passes=1):
    if passes == 1:
        return _dg(a.astype(BF16), b.astype(BF16), dims)
    ah, am, _ = _split3(a)
    bh, bm, _ = _split3(b)
    return _dg(ah, bh, dims) + (_dg(ah, bm, dims) + _dg(am, bh, dims))


def _mm_exact_rhs(a, e, dims=_NN):
    m = a.shape[0]
    r = _dg(jnp.concatenate(_split3(a), axis=0), e, dims)
    return r[0:m] + (r[m:2 * m] + r[2 * m:3 * m])


def _mm_exact_lhs(e, b, dims=_NN):
    hi, mid, lo = _split3(b)
    return _dg(e, hi, dims) + (_dg(e, mid, dims) + _dg(e, lo, dims))


def _iota(shape, dim):
    return lax.broadcasted_iota(jnp.int32, shape, dim)


def _div_pow2(x, n):
    assert n & (n - 1) == 0
    return lax.shift_right_logical(x, jnp.int32(int(math.log2(n))))


def _head_block_ones(n):
    r = _div_pow2(_iota((n, n), 0), HEAD_DIM)
    c = _div_pow2(_iota((n, n), 1), HEAD_DIM)
    return jnp.where(r == c, 1.0, 0.0).astype(BF16)


def _block_diag(tall):
    n = tall.shape[0]
    wide = jnp.concatenate([tall] * (n // HEAD_DIM), axis=1)
    same = _div_pow2(_iota((n, n), 0), HEAD_DIM) == _div_pow2(_iota((n, n), 1), HEAD_DIM)
    return jnp.where(same, wide, 0.0)


def _stack_heads(x, gb, c, n_heads):
    width = x.shape[1]
    lane_head = _div_pow2(_iota((1, width), 1), HEAD_DIM)
    parts = []
    for b in range(gb):
        xb = x[b * c:(b + 1) * c, :]
        for h in range(n_heads):
            parts.append(jnp.where(lane_head == h, xb, 0.0))
    return jnp.concatenate(parts, axis=0)


def _ls_masks(gb, c, n_heads):
    shape = (gb * c, gb * n_heads * c)
    row = _iota(shape, 0)
    lane = _iota(shape, 1)
    same_b = _div_pow2(row, c) == _div_pow2(lane, n_heads * c)
    i = row & (c - 1)
    j = lane & (c - 1)
    return same_b & (j < i), same_b & (j <= i), same_b & (j == i)


def _expand_bd(m_ls, gb, c, n_heads):
    lane_h = _div_pow2(_iota((1, m_ls.shape[1]), 1), c) & (n_heads - 1)
    parts = []
    for b in range(gb):
        mb = m_ls[b * c:(b + 1) * c, :]
        for h in range(n_heads):
            parts.append(jnp.where(lane_h == h, mb, 0.0))
    return jnp.concatenate(parts, axis=0)


def _unit_lower_inverse(p, gb, c, n_heads, passes):
    n_rows = p.shape[0]
    expand = functools.partial(_expand_bd, gb=gb, c=c, n_heads=n_heads)
    t = jnp.where(_ls_masks(gb, c, n_heads)[2], 1.0, 0.0) + p
    n_rounds = int(math.log2(c)) - 1
    pk = _mm(p, expand(p), passes=passes)
    yield
    for i in range(n_rounds):
        if i + 1 < n_rounds:
            both = _mm(jnp.concatenate([t, pk], axis=0), expand(pk), passes=passes)
            t = t + both[0:n_rows]
            pk = both[n_rows:2 * n_rows]
        else:
            t = t + _mm(t, expand(pk), passes=passes)
        yield
    return t


def _run_interleaved(chains):
    results = [None] * len(chains)
    active = list(enumerate(chains))
    while active:
        still = []
        for i, ch in active:
            try:
                next(ch)
                still.append((i, ch))
            except StopIteration as stop:
                results[i] = stop.value
        active = still
    return results


def _chunk_masks(n, c):
    row = _iota((n, n), 0)
    col = _iota((n, n), 1)
    same = _div_pow2(row, c) == _div_pow2(col, c)
    strict = jnp.logical_and(same, row > col)
    incl = jnp.logical_and(same, row >= col)
    return strict, incl


def _cumsum_rows(x, gb, c):
    n = gb * c
    strict, incl = _chunk_masks(n, c)
    del strict
    tri = jnp.where(incl, 1.0, 0.0).astype(BF16)
    return _mm_exact_lhs(tri, x)


def _shift_rows(x, j, prev_rows, gb, c):
    n_prev = prev_rows[0].shape[0]
    out = pltpu.roll(x, j, 0)
    rid = _iota((x.shape[0], 1), 0)
    for b in range(gb):
        for t in range(j):
            src = n_prev + t - j
            out = jnp.where(rid == b * c + t, prev_rows[b][src:src + 1, :], out)
    return out


def _cparams(sem):
    return pltpu.CompilerParams(dimension_semantics=sem, vmem_limit_bytes=VMEM_LIMIT)


def _ln_kernel(x_ref, g_ref, b_ref, o_ref):
    x = x_ref[...]
    mu = jnp.mean(x, axis=-1, keepdims=True)
    xc = x - mu
    var = jnp.mean(xc * xc, axis=-1, keepdims=True)
    o_ref[...] = xc * lax.rsqrt(var + LN_EPS) * g_ref[...] + b_ref[...]


def _layer_norm(x2, g, b, tm):
    n, d = x2.shape
    return pl.pallas_call(
        _ln_kernel,
        grid=(n // tm,),
        in_specs=[pl.BlockSpec((tm, d), lambda i: (i, 0)),
                  pl.BlockSpec((1, d), lambda i: (0, 0)),
                  pl.BlockSpec((1, d), lambda i: (0, 0))],
        out_specs=pl.BlockSpec((tm, d), lambda i: (i, 0)),
        out_shape=jax.ShapeDtypeStruct((n, d), F32),
        compiler_params=_cparams(("parallel",)),
        name="ln_in",
    )(x2, g.reshape(1, d), b.reshape(1, d))


def _proj_kernel(x_ref, w_ref, *o_refs):
    xb = x_ref[...].astype(BF16)
    off = 0
    for o_ref in o_refs:
        wdt = o_ref.shape[-1]
        o_ref[...] = jnp.dot(xb, w_ref[:, off:off + wdt], preferred_element_type=F32)
        off += wdt


def _project(x2, w_all, layer, widths, tm):
    n, d = x2.shape
    n_cols = w_all.shape[-1]
    return pl.pallas_call(
        _proj_kernel,
        grid=(n // tm,),
        in_specs=[pl.BlockSpec((tm, d), lambda i: (i, 0)),
                  pl.BlockSpec((None, d, n_cols), lambda i: (layer, 0, 0))],
        out_specs=[pl.BlockSpec((tm, w), lambda i: (i, 0)) for w in widths],
        out_shape=[jax.ShapeDtypeStruct((n, w), F32) for w in widths],
        compiler_params=_cparams(("parallel",)),
        name="proj_in",
    )(x2, w_all)


def _rwkv_kernel(ash_ref, az_ref, prev0_ref, s0_ref, mu_ref, w0_ref, a0_ref, w2a2_ref, kk_ref, ka_ref,
                 rk_ref, gng_ref, gnb_ref, o_ref, s_ref, prev_scr, *, gb, n_chains, c, n_heads, passes):
    @pl.when(pl.program_id(1) == 0)
    def _():
        for b in range(s_ref.shape[0]):
            s_ref[b] = _block_diag(s0_ref[b])
        prev_scr[...] = prev0_ref[...]

    prm = dict(mu=mu_ref[...], w0=w0_ref[...], a0=a0_ref[...], w2a2=w2a2_ref[...], k_k=kk_ref[...],
               k_a=ka_ref[...], r_k=rk_ref[...], gn_g=gng_ref[...], gn_b=gnb_ref[...])
    ins = []
    for ch in range(n_chains):
        b0 = ch * gb
        ins.append((ash_ref[b0:b0 + gb].reshape(gb * c, ash_ref.shape[-1]),
                    az_ref[b0:b0 + gb].reshape(gb * c, az_ref.shape[-1]),
                    [prev_scr[b0 + b] for b in range(gb)],
                    [s_ref[b0 + b] for b in range(gb)]))
    outs = _run_interleaved([_rwkv_chain(*args, prm, gb=gb, c=c, n_heads=n_heads, passes=passes) for args in ins])
    for ch, (o, s_new, last_rows) in enumerate(outs):
        b0 = ch * gb
        o_ref[b0:b0 + gb] = o.reshape(gb, c, o.shape[-1])
        for b in range(gb):
            s_ref[b0 + b] = s_new[b]
            prev_scr[b0 + b] = last_rows[b]


def _rwkv_chain(p, z, prev_rows, s_old, prm, *, gb, c, n_heads, passes):
    d = n_heads * HEAD_DIM
    prev = _shift_rows(p, 1, prev_rows, gb, c)
    last_rows = [p[(b + 1) * c - 1:(b + 1) * c, :] for b in range(gb)]
    xs = p + (prev - p) * prm["mu"]
    r = xs[:, 0:d]
    k = xs[:, d:2 * d]
    v = xs[:, 2 * d:3 * d]
    wa = xs[:, 3 * d:3 * d + LANES]
    lane = _iota((1, LANES), 1)
    wa = jnp.where(lane < HEAD_DIM, jnp.tanh(wa), wa)
    pre = _mm(wa, prm["w2a2"])
    yield
    w_ll = -_softplus(-(prm["w0"] + pre[:, 0:d])) - 0.5
    lw = -jnp.exp(w_ll)
    a = _sigmoid(prm["a0"] + pre[:, d:2 * d])

    ones_bd = _head_block_ones(d)
    kk = k * prm["k_k"]
    kk = kk / jnp.maximum(jnp.sqrt(_mm_exact_rhs(kk * kk, ones_bd)), 1e-12)
    kmod = k * (1.0 + (a - 1.0) * prm["k_a"])
    kka = kk * a

    cum = _cumsum_rows(lw, gb, c)
    yield
    cum_last = jnp.concatenate(
        [jnp.broadcast_to(cum[(b + 1) * c - 1:(b + 1) * c, :], (c, d)) for b in range(gb)], axis=0)
    inv_p = jnp.exp(-cum)
    to_end = jnp.exp(cum_last - cum)
    b_t = kk * jnp.exp(cum - lw)
    a_t = -kka * inv_p
    k_t = kmod * inv_p
    r_t = r * jnp.exp(cum)

    st = functools.partial(_stack_heads, gb=gb, c=c, n_heads=n_heads)
    n_rows = gb * c
    n = n_rows * n_heads
    g = _mm(jnp.concatenate([b_t, r_t], axis=0), jnp.concatenate([st(a_t), st(k_t)], axis=0), _NT, passes)
    yield
    strict, incl, _ = _ls_masks(gb, c, n_heads)
    m_ab = jnp.where(strict, g[0:n_rows, 0:n], 0.0)
    m_bk = jnp.where(strict, g[0:n_rows, n:2 * n], 0.0)
    m_ra = jnp.where(incl, g[n_rows:2 * n_rows, 0:n], 0.0)
    m_rk = jnp.where(incl, g[n_rows:2 * n_rows, n:2 * n], 0.0)

    rows = [slice(b * c, (b + 1) * c) for b in range(gb)]
    br = [_mm(jnp.concatenate([b_t[sl], r_t[sl]], axis=0), s_old[b], _NT, passes) for b, sl in enumerate(rows)]
    bh = jnp.concatenate([x[0:c] for x in br], axis=0)
    rh = jnp.concatenate([x[c:2 * c] for x in br], axis=0)
    mv = _mm(jnp.concatenate([m_bk, m_rk], axis=0), st(v), passes=passes)
    rhs_u = bh + mv[0:n_rows]
    o = rh + mv[n_rows:2 * n_rows]
    t_inv = yield from _unit_lower_inverse(m_ab, gb, c, n_heads, passes)
    u = _mm(t_inv, st(rhs_u), passes=passes)
    yield
    o = o + _mm(m_ra, st(u), passes=passes)

    a_end = -kka * to_end
    k_end = kmod * to_end
    same_head = _div_pow2(_iota((d, d), 0), HEAD_DIM) == _div_pow2(_iota((d, d), 1), HEAD_DIM)
    s_new = []
    for b, sl in enumerate(rows):
        p_end = jnp.exp(cum[(b + 1) * c - 1:(b + 1) * c, :])
        upd = _mm(jnp.concatenate([u[sl], v[sl]], axis=0), jnp.concatenate([a_end[sl], k_end[sl]], axis=0),
                  _TN, passes)
        s_new.append(s_old[b] * p_end + jnp.where(same_head, upd, 0.0))
    yield

    mean = _mm_exact_rhs(o, ones_bd) * (1.0 / HEAD_DIM)
    oc = o - mean
    yield
    var = _mm_exact_rhs(oc * oc, ones_bd) * (1.0 / HEAD_DIM)
    o = oc * lax.rsqrt(var + RWKV_GN_EPS) * prm["gn_g"] + prm["gn_b"]
    bonus = _mm_exact_rhs(r * kmod * prm["r_k"], ones_bd) * v
    return ((o + bonus) * _silu(z)).astype(BF16), s_new, last_rows


def _rwkv_mix(a_sh, a_z, prev0, s0, lp, gb, n_chains, c, passes):
    bsz, t_len, n_shift = a_sh.shape
    d = a_z.shape[-1]
    n_heads = d // HEAD_DIM
    kern = functools.partial(_rwkv_kernel, gb=gb, n_chains=n_chains, c=c, n_heads=n_heads, passes=passes)
    gb = gb * n_chains
    vec = lambda w: pl.BlockSpec((1, w), lambda b, s: (0, 0))
    return pl.pallas_call(
        kern,
        grid=(bsz // gb, t_len // c),
        in_specs=[pl.BlockSpec((gb, c, n_shift), lambda b, s: (b, s, 0)),
                  pl.BlockSpec((gb, c, d), lambda b, s: (b, s, 0)),
                  pl.BlockSpec((gb, 1, n_shift), lambda b, s: (b, 0, 0)),
                  pl.BlockSpec((gb, d, HEAD_DIM), lambda b, s: (b, 0, 0)),
                  vec(n_shift), vec(d), vec(d),
                  pl.BlockSpec((LANES, 2 * d), lambda b, s: (0, 0)),
                  vec(d), vec(d), vec(d), vec(d), vec(d)],
        out_specs=[pl.BlockSpec((gb, c, d), lambda b, s: (b, s, 0)),
                   pl.BlockSpec((gb, d, d), lambda b, s: (b, 0, 0))],
        out_shape=[jax.ShapeDtypeStruct((bsz, t_len, d), BF16),
                   jax.ShapeDtypeStruct((bsz, d, d), F32)],
        scratch_shapes=[pltpu.VMEM((gb, 1, n_shift), F32)],
        compiler_params=_cparams(("parallel", "arbitrary")),
        name="rwkv_mix",
    )(a_sh, a_z, prev0, s0, lp["mu"], lp["w0"], lp["a0"], lp["w2a2"], lp["k_k"], lp["k_a"], lp["r_k"],
      lp["gn_g"], lp["gn_b"])


def _gdn_kernel(qkv_ref, small_ref, cz_ref, conv0_ref, s0_ref, convw_ref, alog_ref, dtb_ref, ng_ref,
                eb_ref, ea_ref, o_ref, s_ref, conv_scr, *, gb, n_chains, c, n_heads, passes):
    @pl.when(pl.program_id(1) == 0)
    def _():
        for b in range(s_ref.shape[0]):
            s_ref[b] = _block_diag(s0_ref[b])
        conv_scr[...] = conv0_ref[...]

    prm = dict(conv_w=convw_ref[...], a_log=alog_ref[...], dt_bias=dtb_ref[...], norm_g=ng_ref[...],
               e_beta=eb_ref[...], e_a=ea_ref[...])
    ins = []
    for ch in range(n_chains):
        b0 = ch * gb
        ins.append((qkv_ref[b0:b0 + gb].reshape(gb * c, qkv_ref.shape[-1]),
                    small_ref[b0:b0 + gb].reshape(gb * c, LANES),
                    cz_ref[b0:b0 + gb].reshape(gb * c, cz_ref.shape[-1]),
                    [conv_scr[b0 + b] for b in range(gb)],
                    [s_ref[b0 + b] for b in range(gb)]))
    outs = _run_interleaved([_gdn_chain(*args, prm, gb=gb, c=c, n_heads=n_heads, passes=passes) for args in ins])
    for ch, (o, s_new, last_rows) in enumerate(outs):
        b0 = ch * gb
        o_ref[b0:b0 + gb] = o.reshape(gb, c, o.shape[-1])
        for b in range(gb):
            s_ref[b0 + b] = s_new[b]
            conv_scr[b0 + b] = last_rows[b]


def _gdn_chain(x, small, z, prev_rows, s_old, prm, *, gb, c, n_heads, passes):
    d = n_heads * HEAD_DIM
    conv_w = prm["conv_w"]
    n_conv = conv_w.shape[0]
    y = x * conv_w[n_conv - 1:n_conv, :]
    for j in range(1, n_conv):
        y = y + _shift_rows(x, j, prev_rows, gb, c) * conv_w[n_conv - 1 - j:n_conv - j, :]
    last_rows = [x[(b + 1) * c - (n_conv - 1):(b + 1) * c, :] for b in range(gb)]
    y = _silu(y)
    q = y[:, 0:d]
    k = y[:, d:2 * d]
    v = y[:, 2 * d:3 * d]
    ones_bd = _head_block_ones(d)
    q = q * lax.rsqrt(_mm_exact_rhs(q * q, ones_bd) + L2_EPS) * (HEAD_DIM ** -0.5)
    k = k * lax.rsqrt(_mm_exact_rhs(k * k, ones_bd) + L2_EPS)

    beta = _sigmoid(_mm_exact_rhs(small, prm["e_beta"]))
    g = -jnp.exp(prm["a_log"]) * _softplus(_mm_exact_rhs(small, prm["e_a"]) + prm["dt_bias"])
    yield
    gc = _cumsum_rows(g, gb, c)
    yield
    gc_last = jnp.concatenate(
        [jnp.broadcast_to(gc[(b + 1) * c - 1:(b + 1) * c, :], (c, d)) for b in range(gb)], axis=0)
    egc = jnp.exp(gc)

    st = functools.partial(_stack_heads, gb=gb, c=c, n_heads=n_heads)
    n_rows = gb * c
    n = n_rows * n_heads
    kb = k * beta
    strict, incl, _ = _ls_masks(gb, c, n_heads)
    gcol = jnp.min(st(gc), axis=1, keepdims=True)
    eye = _iota((n, n), 0) == _iota((n, n), 1)
    grow = jnp.sum(jnp.where(eye, jnp.broadcast_to(gcol, (n, n)), 0.0), axis=0, keepdims=True)
    if gb == 1 and c == HEAD_DIM:
        gc_ls = gc
    else:
        lane_h = _div_pow2(_iota((d, n), 1), c) & (n_heads - 1)
        pick = jnp.where(_iota((d, n), 0) == lane_h * HEAD_DIM, 1.0, 0.0).astype(BF16)
        gc_ls = _mm_exact_rhs(gc, pick)
    dmat = jnp.exp(jnp.where(incl, gc_ls - grow, 0.0))
    g2 = _mm(jnp.concatenate([kb, q], axis=0), st(k), _NT, passes)
    yield
    m = jnp.where(strict, g2[0:n_rows] * dmat, 0.0)
    qk = jnp.where(incl, g2[n_rows:2 * n_rows] * dmat, 0.0)
    t_inv = yield from _unit_lower_inverse(-m, gb, c, n_heads, passes)
    rhs = jnp.concatenate([st(v * beta), st(kb * egc)], axis=1)
    sol = _mm(t_inv, rhs, passes=passes)
    yield
    u = sol[:, 0:d]
    w = sol[:, d:2 * d]

    rows = [slice(b * c, (b + 1) * c) for b in range(gb)]
    qg = q * egc
    kd = k * jnp.exp(gc_last - gc)
    v_new = u - jnp.concatenate([_mm(w[sl], s_old[b], passes=passes) for b, sl in enumerate(rows)], axis=0)
    o = jnp.concatenate([_mm(qg[sl], s_old[b], passes=passes) for b, sl in enumerate(rows)], axis=0)
    yield
    o = o + _mm(qk, st(v_new), passes=passes)
    same_head = _div_pow2(_iota((d, d), 0), HEAD_DIM) == _div_pow2(_iota((d, d), 1), HEAD_DIM)
    s_new = []
    for b, sl in enumerate(rows):
        gl = jnp.exp(gc[(b + 1) * c - 1:(b + 1) * c, :])
        s_new.append(s_old[b] * gl + jnp.where(same_head, _mm(kd[sl], v_new[sl], _TN, passes), 0.0))
    yield

    ms = _mm_exact_rhs(o * o, ones_bd) * (1.0 / HEAD_DIM)
    o = o * lax.rsqrt(ms + GDN_NORM_EPS) * prm["norm_g"]
    return (o * _silu(z)).astype(BF16), s_new, last_rows


def _gdn_mix(c_qkv, small, c_z, conv0, s0, lp, gb, n_chains, c, passes):
    bsz, t_len, n_qkv = c_qkv.shape
    d = c_z.shape[-1]
    n_heads = d // HEAD_DIM
    n_conv = lp["conv_w"].shape[0]
    kern = functools.partial(_gdn_kernel, gb=gb, n_chains=n_chains, c=c, n_heads=n_heads, passes=passes)
    gb = gb * n_chains
    vec = lambda w: pl.BlockSpec((1, w), lambda b, s: (0, 0))
    return pl.pallas_call(
        kern,
        grid=(bsz // gb, t_len // c),
        in_specs=[pl.BlockSpec((gb, c, n_qkv), lambda b, s: (b, s, 0)),
                  pl.BlockSpec((gb, c, LANES), lambda b, s: (b, s, 0)),
                  pl.BlockSpec((gb, c, d), lambda b, s: (b, s, 0)),
                  pl.BlockSpec((gb, n_conv - 1, n_qkv), lambda b, s: (b, 0, 0)),
                  pl.BlockSpec((gb, d, HEAD_DIM), lambda b, s: (b, 0, 0)),
                  pl.BlockSpec((n_conv, n_qkv), lambda b, s: (0, 0)),
                  vec(d), vec(d), vec(d),
                  pl.BlockSpec((LANES, d), lambda b, s: (0, 0)),
                  pl.BlockSpec((LANES, d), lambda b, s: (0, 0))],
        out_specs=[pl.BlockSpec((gb, c, d), lambda b, s: (b, s, 0)),
                   pl.BlockSpec((gb, d, d), lambda b, s: (b, 0, 0))],
        out_shape=[jax.ShapeDtypeStruct((bsz, t_len, d), BF16),
                   jax.ShapeDtypeStruct((bsz, d, d), F32)],
        scratch_shapes=[pltpu.VMEM((gb, n_conv - 1, n_qkv), F32)],
        compiler_params=_cparams(("parallel", "arbitrary")),
        name="gdn_mix",
    )(c_qkv, small, c_z, conv0, s0, lp["conv_w"], lp["a_log"], lp["dt_bias"], lp["norm_g"],
      lp["e_beta"], lp["e_a"])


def _log_forget_cumsum(small, bf, carry_scr):
    @pl.when(pl.program_id(1) == 0)
    def _():
        carry_scr[...] = jnp.zeros_like(carry_scr)

    logf = -_softplus(-(small + bf))
    tb = logf.shape[0]
    c = _cumsum_rows(logf, 1, tb) + carry_scr[...]
    carry_scr[...] = c[tb - 1:tb, :]
    return logf, c


def _fox_gate_kernel(small_ref, bf_ref, logf_ref, ccol_ref, crow_ref, carry_scr, *, n_heads):
    logf, c = _log_forget_cumsum(small_ref[...], bf_ref[...], carry_scr)
    logf_ref[...] = logf[:, 0:n_heads]
    ccol_ref[...] = c
    sel = jnp.where(_iota((n_heads, LANES), 0) == _iota((n_heads, LANES), 1), 1.0, 0.0).astype(BF16)
    crow_ref[...] = _mm_exact_lhs(sel, c, _NT)


def _fox_gates(small, bf_pad, n_heads, tb):
    bsz, t_len, _ = small.shape
    kern = functools.partial(_fox_gate_kernel, n_heads=n_heads)
    return pl.pallas_call(
        kern,
        grid=(bsz, t_len // tb),
        in_specs=[pl.BlockSpec((None, tb, LANES), lambda b, s: (b, s, 0)),
                  pl.BlockSpec((1, LANES), lambda b, s: (0, 0))],
        out_specs=[pl.BlockSpec((None, tb, n_heads), lambda b, s: (b, s, 0)),
                   pl.BlockSpec((None, tb, LANES), lambda b, s: (b, s, 0)),
                   pl.BlockSpec((None, n_heads, tb), lambda b, s: (b, 0, s))],
        out_shape=[jax.ShapeDtypeStruct((bsz, t_len, n_heads), F32),
                   jax.ShapeDtypeStruct((bsz, t_len, LANES), F32),
                   jax.ShapeDtypeStruct((bsz, n_heads, t_len), F32)],
        scratch_shapes=[pltpu.VMEM((1, LANES), F32)],
        compiler_params=_cparams(("parallel", "arbitrary")),
        name="fox_gates",
    )(small, bf_pad)


N_AUX = 3


def _proj_prompt_kernel(*refs, widths, n_heads, ln_in, n_alias):
    refs = list(refs)
    x_ref, w_ref, bf_ref, eaux_ref = refs[0:4]
    pos = 4
    if ln_in:
        g_ref, b_ref = refs[pos:pos + 2]
        pos += 2
    pos += n_alias
    (ash_ref, az_ref, bz_ref, cqkv_ref, cz_ref, small_ref, k_ref, v_ref, logf_ref,
     qa_ref, ka_ref, va_ref) = refs[pos:pos + 12]
    pos += 12
    if ln_in:
        xn_ref = refs[pos]
        pos += 1
    carry_scr = refs[pos]

    x = x_ref[...]
    if ln_in:
        mu = jnp.mean(x, axis=-1, keepdims=True)
        xc = x - mu
        var = jnp.mean(xc * xc, axis=-1, keepdims=True)
        x = xc * lax.rsqrt(var + LN_EPS) * g_ref[...] + b_ref[...]
        xn_ref[...] = x
    xb = x.astype(BF16)
    offs = np.concatenate([[0], np.cumsum(widths)])
    seg = lambda i: jnp.dot(xb, w_ref[:, int(offs[i]):int(offs[i + 1])], preferred_element_type=F32)
    ash_ref[...] = seg(0)
    az_ref[...] = seg(1)
    q, k, v = seg(2), seg(3), seg(4)
    bz_ref[...] = seg(5)
    cqkv_ref[...] = seg(6)
    cz_ref[...] = seg(7)
    small = seg(8)
    small_ref[...] = small
    k_ref[...] = k
    v_ref[...] = v

    logf, c = _log_forget_cumsum(small, bf_ref[...], carry_scr)
    logf_ref[...] = logf[:, 0:n_heads]
    parts = jnp.concatenate(_split3(c * LOG2E), axis=1)
    lane = _iota((1, LANES), 1)
    for h in range(n_heads):
        pair, hh = divmod(h, 2)
        own = (lane < HEAD_DIM) if hh == 0 else (lane >= HEAD_DIM)
        a0 = HEAD_DIM if hh == 0 else 0
        aux_q = jnp.where((lane >= a0) & (lane < a0 + N_AUX), 1.0, 0.0)
        aux_k = _dg(parts, eaux_ref[h], _NN)
        aux_v = jnp.where(lane == a0, 1.0, 0.0)
        cols = slice(pair * LANES, (pair + 1) * LANES)
        qa_ref[h] = jnp.where(own, q[:, cols] * (HEAD_DIM ** -0.5 * LOG2E), aux_q).astype(BF16)
        ka_ref[h] = jnp.where(own, k[:, cols], aux_k).astype(BF16)
        va_ref[h] = jnp.where(own, v[:, cols], aux_v).astype(BF16)


def _project_prompt(x3, w_all, layer, widths, bf_pad, e_aux, n_heads, tm, ln_params, stacked):
    bsz, t_len, d = x3.shape
    depth, _, n_cols = w_all.shape
    d_fox = n_heads * HEAD_DIM
    ln_in = ln_params is not None
    n_alias = 0 if stacked is None else len(stacked)
    kern = functools.partial(_proj_prompt_kernel, widths=tuple(widths), n_heads=n_heads, ln_in=ln_in,
                             n_alias=n_alias)
    row = lambda w: pl.BlockSpec((None, tm, w), lambda b, s: (b, s, 0))
    per_layer = lambda w: pl.BlockSpec((None, None, tm, w), lambda b, s: (layer, b, s, 0))
    per_head = pl.BlockSpec((None, n_heads, tm, LANES), lambda b, s: (b, 0, s, 0))
    const = lambda shape: pl.BlockSpec(shape, lambda b, s: (0,) * len(shape))
    in_specs = [row(d), pl.BlockSpec((None, d, n_cols), lambda b, s: (layer, 0, 0), pipeline_mode=pl.Buffered(1)),
                const((1, LANES)), const((n_heads, N_AUX * LANES, LANES))]
    args = [x3, w_all, bf_pad, e_aux]
    if ln_in:
        in_specs += [const((1, d)), const((1, d))]
        args += [p.reshape(1, d) for p in ln_params]
    aliases = {}
    if stacked is not None:
        for j, buf in enumerate(stacked):
            aliases[len(args)] = 6 + j
            in_specs.append(pl.BlockSpec(memory_space=pl.ANY))
            args.append(buf)
    f32 = lambda *shape: jax.ShapeDtypeStruct(shape, F32)
    out_specs = [row(widths[0]), row(widths[1]), row(widths[5]), row(widths[6]), row(widths[7]), row(LANES),
                 per_layer(d_fox), per_layer(d_fox), per_layer(n_heads), per_head, per_head, per_head]
    out_shape = [f32(bsz, t_len, widths[0]), f32(bsz, t_len, widths[1]), f32(bsz, t_len, widths[5]),
                 f32(bsz, t_len, widths[6]), f32(bsz, t_len, widths[7]), f32(bsz, t_len, LANES),
                 f32(depth, bsz, t_len, d_fox), f32(depth, bsz, t_len, d_fox), f32(depth, bsz, t_len, n_heads)]
    out_shape += [jax.ShapeDtypeStruct((bsz, n_heads, t_len, LANES), BF16)] * 3
    if ln_in:
        out_specs.append(row(d))
        out_shape.append(f32(bsz, t_len, d))
    return pl.pallas_call(
        kern,
        grid=(bsz, t_len // tm),
        in_specs=in_specs,
        out_specs=out_specs,
        out_shape=out_shape,
        input_output_aliases=aliases,
        scratch_shapes=[pltpu.VMEM((1, LANES), F32)],
        compiler_params=_cparams(("parallel", "arbitrary")),
        name="proj_prompt",
    )(*args)


def _fox_prompt_kernel(qa_ref, ka_ref, va_ref, bz_ref, o_ref, s_scr, m_scr, acc_scr, *, blk):
    qi = pl.program_id(2)
    m_scr[...] = jnp.full_like(m_scr, NEG_BIG)
    acc_scr[...] = jnp.zeros_like(acc_scr)

    def logits(ki, slot):
        k0 = pl.multiple_of(ki * blk, blk)
        for hh in range(2):
            s_scr[slot, hh] = _dg(qa_ref[hh], ka_ref[hh, pl.ds(k0, blk), :], _NT)

    def consume(ki, slot, masked):
        k0 = pl.multiple_of(ki * blk, blk)
        for hh in range(2):
            s = s_scr[slot, hh]
            if masked:
                s = jnp.where(_iota((blk, blk), 1) <= _iota((blk, blk), 0), s, NEG_BIG)
            m_prev = m_scr[hh]
            m_new = jnp.maximum(m_prev, jnp.max(s, axis=1, keepdims=True))
            alpha = jnp.exp2(m_prev - m_new)
            p = jnp.exp2(s - jnp.tile(m_new, (1, blk // LANES)))
            acc_scr[hh] = alpha * acc_scr[hh] + _dg(p.astype(BF16), va_ref[hh, pl.ds(k0, blk), :], _NN)
            m_scr[hh] = m_new

    def two_blocks(j, carry):
        logits(2 * j + 1, 1)
        consume(2 * j, 0, False)
        logits(2 * j + 2, 0)
        consume(2 * j + 1, 1, False)
        return carry

    logits(0, 0)
    lax.fori_loop(0, qi // 2, two_blocks, 0)

    @pl.when(qi % 2 == 0)
    def _():
        consume(qi, 0, True)

    @pl.when(qi % 2 == 1)
    def _():
        logits(qi, 1)
        consume(qi - 1, 0, False)
        consume(qi, 1, True)

    lane = _iota((1, LANES), 1)
    acc0 = acc_scr[0]
    acc1 = acc_scr[1]
    o = jnp.where(lane < HEAD_DIM, acc0 / acc0[:, HEAD_DIM:HEAD_DIM + 1], acc1 / acc1[:, 0:1])
    o_ref[...] = (o * _silu(bz_ref[...])).astype(BF16)


def _fox_prompt(qa, ka, va, b_z, blk):
    bsz, n_heads, t_len, _ = qa.shape
    d = b_z.shape[-1]
    n_pairs = n_heads // 2
    return pl.pallas_call(
        functools.partial(_fox_prompt_kernel, blk=blk),
        grid=(bsz, n_pairs, t_len // blk),
        in_specs=[pl.BlockSpec((None, 2, blk, LANES), lambda b, p, i: (b, p, i, 0)),
                  pl.BlockSpec((None, 2, t_len, LANES), lambda b, p, i: (b, p, 0, 0)),
                  pl.BlockSpec((None, 2, t_len, LANES), lambda b, p, i: (b, p, 0, 0)),
                  pl.BlockSpec((None, blk, LANES), lambda b, p, i: (b, i, p))],
        out_specs=pl.BlockSpec((None, blk, LANES), lambda b, p, i: (b, i, p)),
        out_shape=jax.ShapeDtypeStruct((bsz, t_len, d), BF16),
        scratch_shapes=[pltpu.VMEM((2, 2, blk, blk), F32),
                        pltpu.VMEM((2, blk, LANES), F32),
                        pltpu.VMEM((2, blk, LANES), F32)],
        compiler_params=_cparams(("parallel", "parallel", "arbitrary")),
        name="fox_prompt",
    )(qa, ka, va, b_z)


def _fox_cached_kernel(q_ref, k_ref, v_ref, ck_ref, cv_ref, clf_ref, ccol_ref, crow_ref, bz_ref, o_ref,
                       *, n_heads, lane_blk):
    t_len, d = q_ref.shape
    past = ck_ref.shape[1]
    n_rows = n_heads * t_len

    triu = jnp.where(_iota((lane_blk, lane_blk), 0) <= _iota((lane_blk, lane_blk), 1), 1.0, 0.0).astype(BF16)
    carry = jnp.zeros((n_heads, 1), F32)
    pieces = []
    for j in range(past // lane_blk):
        cj = _mm_exact_rhs(clf_ref[:, j * lane_blk:(j + 1) * lane_blk], triu) + carry
        carry = cj[:, lane_blk - 1:lane_blk]
        pieces.append(cj)
    c_cache = jnp.concatenate(pieces, axis=1)
    total = carry

    rows = lambda f: jnp.concatenate([f(h) for h in range(n_heads)], axis=0)
    ck_cache = rows(lambda h: jnp.broadcast_to(c_cache[h:h + 1, :], (t_len, past)))
    ck_new = rows(lambda h: jnp.broadcast_to(total[h:h + 1, :] + crow_ref[h:h + 1, :], (t_len, t_len)))
    cq = rows(lambda h: total[h:h + 1, :] + ccol_ref[:, h:h + 1])

    lane_head = _div_pow2(_iota((1, d), 1), HEAD_DIM)
    q = q_ref[...] * (HEAD_DIM ** -0.5)
    q_bd = rows(lambda h: jnp.where(lane_head == h, q, 0.0)).astype(BF16)
    s_c = _dg(q_bd, ck_ref[...].astype(BF16), _NN) + cq - ck_cache
    s_n = _dg(q_bd, k_ref[...].astype(BF16), _NT) + cq - ck_new
    q_pos = _iota((n_rows, t_len), 0) & (t_len - 1)
    s_n = jnp.where(_iota((n_rows, t_len), 1) <= q_pos, s_n, NEG_BIG)
    m = jnp.maximum(jnp.max(s_c, axis=1, keepdims=True), jnp.max(s_n, axis=1, keepdims=True))
    p_c = jnp.exp(s_c - m)
    p_n = jnp.exp(s_n - m)
    l = jnp.sum(p_c, axis=1, keepdims=True) + jnp.sum(p_n, axis=1, keepdims=True)
    o_all = (_dg(p_c.astype(BF16), cv_ref[...].astype(BF16), _NT)
             + _dg(p_n.astype(BF16), v_ref[...].astype(BF16), _NN)) / l
    o = jnp.where(lane_head == 0, o_all[0:t_len], 0.0)
    for h in range(1, n_heads):
        o = o + jnp.where(lane_head == h, o_all[h * t_len:(h + 1) * t_len], 0.0)
    o_ref[...] = (o * _silu(bz_ref[...])).astype(BF16)


def _fox_cached(q, k, v, cache_k, cache_v, cache_logf_t, ccol, crow, b_z, layer):
    bsz, t_len, d = q.shape
    n_heads = d // HEAD_DIM
    past = cache_k.shape[3]
    kern = functools.partial(_fox_cached_kernel, n_heads=n_heads, lane_blk=min(past, 256))
    cur = lambda w: pl.BlockSpec((None, t_len, w), lambda b: (b, 0, 0))
    return pl.pallas_call(
        kern,
        grid=(bsz,),
        in_specs=[cur(d), cur(d), cur(d),
                  pl.BlockSpec((None, None, d, past), lambda b: (layer, b, 0, 0)),
                  pl.BlockSpec((None, None, d, past), lambda b: (layer, b, 0, 0)),
                  pl.BlockSpec((None, None, n_heads, past), lambda b: (layer, b, 0, 0)),
                  cur(LANES),
                  pl.BlockSpec((None, n_heads, t_len), lambda b: (b, 0, 0)),
                  cur(d)],
        out_specs=cur(d),
        out_shape=jax.ShapeDtypeStruct((bsz, t_len, d), BF16),
        compiler_params=_cparams(("parallel",)),
        name="fox_cached",
    )(q, k, v, cache_k, cache_v, cache_logf_t, ccol, crow, b_z)


def _out_kernel(oa_ref, ob_ref, oc_ref, x_ref, w_ref, g_ref, b_ref, y_ref, *, alpha):
    da = oa_ref.shape[-1]
    db = ob_ref.shape[-1]
    h = jnp.dot(oa_ref[...], w_ref[0:da, :], preferred_element_type=F32)
    h = h + jnp.dot(ob_ref[...], w_ref[da:da + db, :], preferred_element_type=F32)
    h = h + jnp.dot(oc_ref[...], w_ref[da + db:, :], preferred_element_type=F32)
    y = alpha * x_ref[...] + h
    mu = jnp.mean(y, axis=-1, keepdims=True)
    yc = y - mu
    var = jnp.mean(yc * yc, axis=-1, keepdims=True)
    y_ref[...] = yc * lax.rsqrt(var + LN_EPS) * g_ref[...] + b_ref[...]


def _out_project(o_a, o_b, o_c, x2, w_all, layer, g, b, alpha, tm):
    n, d = x2.shape
    d_mix = w_all.shape[1]
    row = lambda w: pl.BlockSpec((tm, w), lambda i: (i, 0))
    return pl.pallas_call(
        functools.partial(_out_kernel, alpha=alpha),
        grid=(n // tm,),
        in_specs=[row(o_a.shape[-1]), row(o_b.shape[-1]), row(o_c.shape[-1]), row(d),
                  pl.BlockSpec((None, d_mix, d), lambda i: (layer, 0, 0)),
                  pl.BlockSpec((1, d), lambda i: (0, 0)),
                  pl.BlockSpec((1, d), lambda i: (0, 0))],
        out_specs=row(d),
        out_shape=jax.ShapeDtypeStruct((n, d), F32),
        compiler_params=_cparams(("parallel",)),
        name="proj_out",
    )(o_a, o_b, o_c, x2, w_all, g.reshape(1, d), b.reshape(1, d))


def _diag_blocks(s, n_h):
    bsz = s.shape[0]
    s5 = s.reshape(bsz, n_h, HEAD_DIM, n_h, HEAD_DIM)
    return jnp.stack([s5[:, h, :, h, :] for h in range(n_h)], axis=1)


def _expand_heads(p):
    return jnp.repeat(p.astype(F32), HEAD_DIM).reshape(1, -1)


def _select_matrix(rows, n_cols, first_row):
    e = np.zeros((LANES, n_cols), np.float32)
    for h in range(rows):
        e[first_row + h, h * HEAD_DIM:(h + 1) * HEAD_DIM] = 1.0
    return jnp.asarray(e, BF16)


def _row_tile(n, largest=256):
    for tm in (1024, 512, 256, 128, 64, 32, 16, 8):
        if tm <= largest and n % tm == 0:
            return tm
    raise ValueError(f"row count {n} must be a multiple of 8")


def kernel(x_prompt, x_sample, cache_fox_k, cache_fox_v, cache_fox_logf, state_rwkv_shift, state_rwkv_wkv,
           state_gdn_conv, state_gdn_wkv, ln_in_g, ln_in_b, w_in, rwkv_mu, rwkv_w0, rwkv_w2, rwkv_a0, rwkv_a2,
           rwkv_k_k, rwkv_k_a, rwkv_r_k, rwkv_gn_g, rwkv_gn_b, fox_b_f, gdn_conv_w, gdn_a_log, gdn_dt_bias,
           gdn_norm_g, w_out, ln_post_g, ln_post_b):
    depth, d_model, _ = w_in.shape
    bp, seq, _ = x_prompt.shape
    bs, dec_seq, _ = x_sample.shape
    h_fox = fox_b_f.shape[1]
    h_gdn = gdn_a_log.shape[1]
    h_rwkv = rwkv_r_k.shape[1]
    d_rwkv, d_fox, d_gdn = h_rwkv * HEAD_DIM, h_fox * HEAD_DIM, h_gdn * HEAD_DIM
    rank_w = rwkv_w2.shape[1]
    rank_a = rwkv_a2.shape[1]
    n_shift = 3 * d_rwkv + rank_w + rank_a
    n_conv_cols = 3 * d_gdn
    n_conv = gdn_conv_w.shape[1]
    alpha = (2 * depth) ** 0.25
    assert rank_w + rank_a == LANES and h_fox + 2 * h_gdn <= LANES

    split = [n_shift, d_rwkv, d_fox, d_fox, d_fox, h_fox, d_fox, n_conv_cols, h_gdn, h_gdn, d_gdn]
    offs = np.concatenate([[0], np.cumsum(split)])
    seg = lambda i: w_in[:, :, offs[i]:offs[i + 1]]
    small_w = jnp.concatenate([seg(5), seg(8), seg(9)], axis=-1)
    small_w = jnp.pad(small_w, ((0, 0), (0, 0), (0, LANES - small_w.shape[-1])))
    w_in_p = jnp.concatenate([seg(0), seg(1), seg(2), seg(3), seg(4), seg(6), seg(7), seg(10), small_w],
                             axis=-1).astype(BF16)
    widths = [n_shift, d_rwkv, d_fox, d_fox, d_fox, d_fox, n_conv_cols, d_gdn, LANES]
    w_out_b = w_out.astype(BF16)

    zw = jnp.zeros((depth, rank_w, d_rwkv), F32)
    w2a2 = jnp.concatenate([jnp.concatenate([rwkv_w2, zw], axis=-1),
                            jnp.concatenate([zw, rwkv_a2], axis=-1)], axis=1).astype(BF16)
    bf_pad = jnp.pad(fox_b_f, ((0, 0), (0, LANES - h_fox)))
    e_beta = _select_matrix(h_gdn, d_gdn, h_fox)
    e_a = _select_matrix(h_gdn, d_gdn, h_fox + h_gdn)
    e_aux_np = np.zeros((h_fox, N_AUX * LANES, LANES), np.float32)
    for h in range(h_fox):
        for j in range(N_AUX):
            e_aux_np[h, j * LANES + h, (HEAD_DIM if h % 2 == 0 else 0) + j] = -1.0
    e_aux = jnp.asarray(e_aux_np, BF16)
    cache_logf_t = jnp.swapaxes(cache_fox_logf, 2, 3)
    cache_k_t = jnp.swapaxes(cache_fox_k.reshape(depth, bs, -1, d_fox), 2, 3)
    cache_v_t = jnp.swapaxes(cache_fox_v.reshape(depth, bs, -1, d_fox), 2, 3)

    def layer_params(l):
        row = lambda a: a[l].reshape(1, -1).astype(F32)
        return dict(
            rwkv=dict(mu=row(rwkv_mu), w0=row(rwkv_w0), a0=row(rwkv_a0), w2a2=w2a2[l], k_k=row(rwkv_k_k),
                      k_a=row(rwkv_k_a), r_k=row(rwkv_r_k), gn_g=row(rwkv_gn_g), gn_b=row(rwkv_gn_b)),
            gdn=dict(conv_w=gdn_conv_w[l], a_log=_expand_heads(gdn_a_log[l]),
                     dt_bias=_expand_heads(gdn_dt_bias[l]),
                     norm_g=jnp.tile(gdn_norm_g[l].reshape(1, -1), (1, h_gdn)), e_beta=e_beta, e_a=e_a),
        )

    def mix_and_project(l, x2, bsz, t_len, a_sh, a_z, c_qkv, small, c_z, o_b, prev0, s_rwkv0, conv0, s_gdn0):
        lp = layer_params(l)
        c = min(HEAD_DIM, t_len)
        gb = N_STACK // (h_rwkv * c)
        n_chains = max(1, min(N_CHAINS, bsz // gb))
        o_a, s_rwkv = _rwkv_mix(a_sh, a_z, prev0, s_rwkv0, lp["rwkv"], gb, n_chains, c, CHUNK_PASSES)
        gb = N_STACK // (h_gdn * c)
        o_c, s_gdn = _gdn_mix(c_qkv, small, c_z, conv0, s_gdn0, lp["gdn"], gb, n_chains, c, CHUNK_PASSES)
        f2 = lambda a: a.reshape(bsz * t_len, a.shape[-1])
        x_new = _out_project(f2(o_a), f2(o_b), f2(o_c), x2, w_out_b, l, ln_post_g[l], ln_post_b[l], alpha,
                             _row_tile(bsz * t_len, OUT_ROWS))
        outs = (a_sh[:, -1], _diag_blocks(s_rwkv, h_rwkv), c_qkv[:, t_len - (n_conv - 1):],
                _diag_blocks(s_gdn, h_gdn))
        return x_new, outs

    def prompt_layer(l, x2, stacked):
        ln_params = (ln_in_g, ln_in_b) if l == 0 else None
        res = _project_prompt(x2.reshape(bp, seq, d_model), w_in_p, l, widths, bf_pad[l:l + 1], e_aux, h_fox,
                              min(seq, PROJ_ROWS), ln_params, stacked)
        a_sh, a_z, b_z, c_qkv, c_z, small, k_st, v_st, logf_st, qa, ka, va = res[:12]
        if l == 0:
            x2 = res[12].reshape(bp * seq, d_model)
        o_b = _fox_prompt(qa, ka, va, b_z, min(seq, FOX_BLOCK))
        x_new, outs = mix_and_project(l, x2, bp, seq, a_sh, a_z, c_qkv, small, c_z, o_b, *zeros_p)
        return x_new, outs, (k_st, v_st, logf_st)

    def sample_layer(l, x2):
        tm = _row_tile(bs * dec_seq)
        r3 = lambda a: a.reshape(bs, dec_seq, a.shape[-1])
        a_sh, a_z, b_q, b_k, b_v, b_z, c_qkv, c_z, small = map(r3, _project(x2, w_in_p, l, widths, tm))
        logf, ccol, crow = _fox_gates(small, bf_pad[l:l + 1], h_fox, min(dec_seq, 256))
        o_b = _fox_cached(b_q, b_k, b_v, cache_k_t, cache_v_t, cache_logf_t, ccol, crow, b_z, l)
        x_new, outs = mix_and_project(l, x2, bs, dec_seq, a_sh, a_z, c_qkv, small, c_z, o_b,
                                      state_rwkv_shift[l][:, None, :], state_rwkv_wkv[l].reshape(bs, d_rwkv, HEAD_DIM),
                                      state_gdn_conv[l], state_gdn_wkv[l].reshape(bs, d_gdn, HEAD_DIM))
        heads = lambda a: a.reshape(bs, dec_seq, h_fox, HEAD_DIM)
        return x_new, (heads(b_k), heads(b_v), logf) + outs

    xp = x_prompt.reshape(bp * seq, d_model)
    xs = _layer_norm(x_sample.reshape(bs * dec_seq, d_model), ln_in_g, ln_in_b, _row_tile(bs * dec_seq))
    zeros_p = (jnp.zeros((bp, 1, n_shift), F32), jnp.zeros((bp, d_rwkv, HEAD_DIM), F32),
               jnp.zeros((bp, n_conv - 1, n_conv_cols), F32), jnp.zeros((bp, d_gdn, HEAD_DIM), F32))
    outs_p, outs_s, stacked = [], [], None
    for l in range(depth):
        xp, o, stacked = prompt_layer(l, xp, stacked)
        outs_p.append(o)
        xs, o = sample_layer(l, xs)
        outs_s.append(o)
    k_st, v_st, logf_st = stacked
    stack = lambda outs, i: jnp.stack([o[i] for o in outs])
    return ((xp.reshape(bp, seq, d_model), xs.reshape(bs, dec_seq, d_model),
             k_st.reshape(depth, bp, seq, h_fox, HEAD_DIM), v_st.reshape(depth, bp, seq, h_fox, HEAD_DIM), logf_st)
            + tuple(stack(outs_p, i) for i in range(4)) + tuple(stack(outs_s, i) for i in range(7)))
```

```python
import functools
import math

import jax
import jax.numpy as jnp
import numpy as np
from jax import lax
from jax.experimental import pallas as pl
from jax.experimental.pallas import tpu as pltpu

F32 = jnp.float32
BF16 = jnp.bfloat16

HEAD_DIM = 64
LANES = 128
N_STACK = 256
PROJ_ROWS = 512
OUT_ROWS = 1024
FOX_BLOCK = 512
FOX_HEADS = 4
N_CHAINS = 8
CHUNK_PASSES = 1
LN_EPS = 1e-5
RWKV_GN_EPS = 64e-5
GDN_NORM_EPS = 1e-6
L2_EPS = 1e-6
NEG_BIG = -1e30
LOG2E = math.log2(math.e)
VMEM_LIMIT = 56 * 1024 * 1024


def _sigmoid(x):
    return 1.0 / (1.0 + jnp.exp(-x))


def _softplus(x):
    return jnp.maximum(x, 0.0) + jnp.log(1.0 + jnp.exp(-jnp.abs(x)))


def _silu(x):
    return x * _sigmoid(x)


def _split3(x):
    hi = x.astype(BF16)
    r1 = x - hi.astype(F32)
    mid = r1.astype(BF16)
    lo = (r1 - mid.astype(F32)).astype(BF16)
    return hi, mid, lo


_NN = (((1,), (0,)), ((), ()))
_NT = (((1,), (1,)), ((), ()))
_TN = (((0,), (0,)), ((), ()))


def _dg(a, b, dims):
    return lax.dot_general(a, b, dims, preferred_element_type=F32)


def _mm(a, b, dims=_NN, passes=1):
    if passes == 1:
        return _dg(a.astype(BF16), b.astype(BF16), dims)
    ah, am, _ = _split3(a)
    bh, bm, _ = _split3(b)
    return _dg(ah, bh, dims) + (_dg(ah, bm, dims) + _dg(am, bh, dims))


def _mm_exact_rhs(a, e, dims=_NN):
    m = a.shape[0]
    r = _dg(jnp.concatenate(_split3(a), axis=0), e, dims)
    return r[0:m] + (r[m:2 * m] + r[2 * m:3 * m])


def _mm_exact_lhs(e, b, dims=_NN):
    hi, mid, lo = _split3(b)
    return _dg(e, hi, dims) + (_dg(e, mid, dims) + _dg(e, lo, dims))


def _iota(shape, dim):
    return lax.broadcasted_iota(jnp.int32, shape, dim)


def _div_pow2(x, n):
    assert n & (n - 1) == 0
    return lax.shift_right_logical(x, jnp.int32(int(math.log2(n))))


def _head_block_ones(n):
    r = _div_pow2(_iota((n, n), 0), HEAD_DIM)
    c = _div_pow2(_iota((n, n), 1), HEAD_DIM)
    return jnp.where(r == c, 1.0, 0.0).astype(BF16)


def _block_diag(tall):
    n = tall.shape[0]
    wide = jnp.concatenate([tall] * (n // HEAD_DIM), axis=1)
    same = _div_pow2(_iota((n, n), 0), HEAD_DIM) == _div_pow2(_iota((n, n), 1), HEAD_DIM)
    return jnp.where(same, wide, 0.0)


def _stack_heads(x, gb, c, n_heads):
    width = x.shape[1]
    lane_head = _div_pow2(_iota((1, width), 1), HEAD_DIM)
    parts = []
    for b in range(gb):
        xb = x[b * c:(b + 1) * c, :]
        for h in range(n_heads):
            parts.append(jnp.where(lane_head == h, xb, 0.0))
    return jnp.concatenate(parts, axis=0)


def _ls_masks(gb, c, n_heads):
    shape = (gb * c, gb * n_heads * c)
    row = _iota(shape, 0)
    lane = _iota(shape, 1)
    same_b = _div_pow2(row, c) == _div_pow2(lane, n_heads * c)
    i = row & (c - 1)
    j = lane & (c - 1)
    return same_b & (j < i), same_b & (j <= i), same_b & (j == i)


def _expand_bd(m_ls, gb, c, n_heads):
    lane_h = _div_pow2(_iota((1, m_ls.shape[1]), 1), c) & (n_heads - 1)
    parts = []
    for b in range(gb):
        mb = m_ls[b * c:(b + 1) * c, :]
        for h in range(n_heads):
            parts.append(jnp.where(lane_h == h, mb, 0.0))
    return jnp.concatenate(parts, axis=0)


def _unit_lower_inverse(p, gb, c, n_heads, passes):
    n_rows = p.shape[0]
    expand = functools.partial(_expand_bd, gb=gb, c=c, n_heads=n_heads)
    t = jnp.where(_ls_masks(gb, c, n_heads)[2], 1.0, 0.0) + p
    n_rounds = int(math.log2(c)) - 1
    pk = _mm(p, expand(p), passes=passes)
    yield
    for i in range(n_rounds):
        if i + 1 < n_rounds:
            both = _mm(jnp.concatenate([t, pk], axis=0), expand(pk), passes=passes)
            t = t + both[0:n_rows]
            pk = both[n_rows:2 * n_rows]
        else:
            t = t + _mm(t, expand(pk), passes=passes)
        yield
    return t


def _run_interleaved(chains):
    results = [None] * len(chains)
    active = list(enumerate(chains))
    while active:
        still = []
        for i, ch in active:
            try:
                next(ch)
                still.append((i, ch))
            except StopIteration as stop:
                results[i] = stop.value
        active = still
    return results


def _chunk_masks(n, c):
    row = _iota((n, n), 0)
    col = _iota((n, n), 1)
    same = _div_pow2(row, c) == _div_pow2(col, c)
    strict = jnp.logical_and(same, row > col)
    incl = jnp.logical_and(same, row >= col)
    return strict, incl


def _cumsum_rows(x, gb, c):
    n = gb * c
    strict, incl = _chunk_masks(n, c)
    del strict
    tri = jnp.where(incl, 1.0, 0.0).astype(BF16)
    return _mm_exact_lhs(tri, x)


def _shift_rows(x, j, prev_rows, gb, c):
    n_prev = prev_rows[0].shape[0]
    out = pltpu.roll(x, j, 0)
    rid = _iota((x.shape[0], 1), 0)
    for b in range(gb):
        for t in range(j):
            src = n_prev + t - j
            out = jnp.where(rid == b * c + t, prev_rows[b][src:src + 1, :], out)
    return out


def _cparams(sem):
    return pltpu.CompilerParams(dimension_semantics=sem, vmem_limit_bytes=VMEM_LIMIT)


def _ln_kernel(x_ref, g_ref, b_ref, o_ref):
    x = x_ref[...]
    mu = jnp.mean(x, axis=-1, keepdims=True)
    xc = x - mu
    var = jnp.mean(xc * xc, axis=-1, keepdims=True)
    o_ref[...] = xc * lax.rsqrt(var + LN_EPS) * g_ref[...] + b_ref[...]


def _layer_norm(x2, g, b, tm):
    n, d = x2.shape
    return pl.pallas_call(
        _ln_kernel,
        grid=(n // tm,),
        in_specs=[pl.BlockSpec((tm, d), lambda i: (i, 0)),
                  pl.BlockSpec((1, d), lambda i: (0, 0)),
                  pl.BlockSpec((1, d), lambda i: (0, 0))],
        out_specs=pl.BlockSpec((tm, d), lambda i: (i, 0)),
        out_shape=jax.ShapeDtypeStruct((n, d), F32),
        compiler_params=_cparams(("parallel",)),
        name="ln_in",
    )(x2, g.reshape(1, d), b.reshape(1, d))


def _proj_kernel(x_ref, w_ref, *o_refs):
    xb = x_ref[...].astype(BF16)
    off = 0
    for o_ref in o_refs:
        wdt = o_ref.shape[-1]
        o_ref[...] = jnp.dot(xb, w_ref[:, off:off + wdt], preferred_element_type=F32)
        off += wdt


def _project(x2, w_all, layer, widths, tm):
    n, d = x2.shape
    n_cols = w_all.shape[-1]
    return pl.pallas_call(
        _proj_kernel,
        grid=(n // tm,),
        in_specs=[pl.BlockSpec((tm, d), lambda i: (i, 0)),
                  pl.BlockSpec((None, d, n_cols), lambda i: (layer, 0, 0))],
        out_specs=[pl.BlockSpec((tm, w), lambda i: (i, 0)) for w in widths],
        out_shape=[jax.ShapeDtypeStruct((n, w), F32) for w in widths],
        compiler_params=_cparams(("parallel",)),
        name="proj_in",
    )(x2, w_all)


def _rwkv_kernel(ash_ref, az_ref, prev0_ref, s0_ref, mu_ref, w0_ref, a0_ref, w2a2_ref, kk_ref, ka_ref,
                 rk_ref, gng_ref, gnb_ref, o_ref, s_ref, prev_scr, *, gb, n_chains, c, n_heads, passes):
    @pl.when(pl.program_id(1) == 0)
    def _():
        for b in range(s_ref.shape[0]):
            s_ref[b] = _block_diag(s0_ref[b])
        prev_scr[...] = prev0_ref[...]

    prm = dict(mu=mu_ref[...], w0=w0_ref[...], a0=a0_ref[...], w2a2=w2a2_ref[...], k_k=kk_ref[...],
               k_a=ka_ref[...], r_k=rk_ref[...], gn_g=gng_ref[...], gn_b=gnb_ref[...])
    ins = []
    for ch in range(n_chains):
        b0 = ch * gb
        ins.append((ash_ref[b0:b0 + gb].reshape(gb * c, ash_ref.shape[-1]),
                    az_ref[b0:b0 + gb].reshape(gb * c, az_ref.shape[-1]),
                    [prev_scr[b0 + b] for b in range(gb)],
                    [s_ref[b0 + b] for b in range(gb)]))
    outs = _run_interleaved([_rwkv_chain(*args, prm, gb=gb, c=c, n_heads=n_heads, passes=passes) for args in ins])
    for ch, (o, s_new, last_rows) in enumerate(outs):
        b0 = ch * gb
        o_ref[b0:b0 + gb] = o.reshape(gb, c, o.shape[-1])
        for b in range(gb):
            s_ref[b0 + b] = s_new[b]
            prev_scr[b0 + b] = last_rows[b]


def _rwkv_chain(p, z, prev_rows, s_old, prm, *, gb, c, n_heads, passes):
    d = n_heads * HEAD_DIM
    prev = _shift_rows(p, 1, prev_rows, gb, c)
    last_rows = [p[(b + 1) * c - 1:(b + 1) * c, :] for b in range(gb)]
    xs = p + (prev - p) * prm["mu"]
    r = xs[:, 0:d]
    k = xs[:, d:2 * d]
    v = xs[:, 2 * d:3 * d]
    wa = xs[:, 3 * d:3 * d + LANES]
    lane = _iota((1, LANES), 1)
    wa = jnp.where(lane < HEAD_DIM, jnp.tanh(wa), wa)
    pre = _mm(wa, prm["w2a2"])
    yield
    w_ll = -_softplus(-(prm["w0"] + pre[:, 0:d])) - 0.5
    lw = -jnp.exp(w_ll)
    a = _sigmoid(prm["a0"] + pre[:, d:2 * d])

    ones_bd = _head_block_ones(d)
    kk = k * prm["k_k"]
    kk = kk / jnp.maximum(jnp.sqrt(_mm_exact_rhs(kk * kk, ones_bd)), 1e-12)
    kmod = k * (1.0 + (a - 1.0) * prm["k_a"])
    kka = kk * a

    cum = _cumsum_rows(lw, gb, c)
    yield
    cum_last = jnp.concatenate(
        [jnp.broadcast_to(cum[(b + 1) * c - 1:(b + 1) * c, :], (c, d)) for b in range(gb)], axis=0)
    inv_p = jnp.exp(-cum)
    to_end = jnp.exp(cum_last - cum)
    b_t = kk * jnp.exp(cum - lw)
    a_t = -kka * inv_p
    k_t = kmod * inv_p
    r_t = r * jnp.exp(cum)

    st = functools.partial(_stack_heads, gb=gb, c=c, n_heads=n_heads)
    n_rows = gb * c
    n = n_rows * n_heads
    g = _mm(jnp.concatenate([b_t, r_t], axis=0), jnp.concatenate([st(a_t), st(k_t)], axis=0), _NT, passes)
    yield
    strict, incl, _ = _ls_masks(gb, c, n_heads)
    m_ab = jnp.where(strict, g[0:n_rows, 0:n], 0.0)
    m_bk = jnp.where(strict, g[0:n_rows, n:2 * n], 0.0)
    m_ra = jnp.where(incl, g[n_rows:2 * n_rows, 0:n], 0.0)
    m_rk = jnp.where(incl, g[n_rows:2 * n_rows, n:2 * n], 0.0)

    rows = [slice(b * c, (b + 1) * c) for b in range(gb)]
    br = [_mm(jnp.concatenate([b_t[sl], r_t[sl]], axis=0), s_old[b], _NT, passes) for b, sl in enumerate(rows)]
    bh = jnp.concatenate([x[0:c] for x in br], axis=0)
    rh = jnp.concatenate([x[c:2 * c] for x in br], axis=0)
    mv = _mm(jnp.concatenate([m_bk, m_rk], axis=0), st(v), passes=passes)
    rhs_u = bh + mv[0:n_rows]
    o = rh + mv[n_rows:2 * n_rows]
    t_inv = yield from _unit_lower_inverse(m_ab, gb, c, n_heads, passes)
    u = _mm(t_inv, st(rhs_u), passes=passes)
    yield
    o = o + _mm(m_ra, st(u), passes=passes)

    a_end = -kka * to_end
    k_end = kmod * to_end
    same_head = _div_pow2(_iota((d, d), 0), HEAD_DIM) == _div_pow2(_iota((d, d), 1), HEAD_DIM)
    s_new = []
    for b, sl in enumerate(rows):
        p_end = jnp.exp(cum[(b + 1) * c - 1:(b + 1) * c, :])
        upd = _mm(jnp.concatenate([u[sl], v[sl]], axis=0), jnp.concatenate([a_end[sl], k_end[sl]], axis=0),
                  _TN, passes)
        s_new.append(s_old[b] * p_end + jnp.where(same_head, upd, 0.0))
    yield

    mean = _mm_exact_rhs(o, ones_bd) * (1.0 / HEAD_DIM)
    oc = o - mean
    yield
    var = _mm_exact_rhs(oc * oc, ones_bd) * (1.0 / HEAD_DIM)
    o = oc * lax.rsqrt(var + RWKV_GN_EPS) * prm["gn_g"] + prm["gn_b"]
    bonus = _mm_exact_rhs(r * kmod * prm["r_k"], ones_bd) * v
    return ((o + bonus) * _silu(z)).astype(BF16), s_new, last_rows


def _rwkv_mix(a_sh, a_z, prev0, s0, lp, gb, n_chains, c, passes):
    bsz, t_len, n_shift = a_sh.shape
    d = a_z.shape[-1]
    n_heads = d // HEAD_DIM
    kern = functools.partial(_rwkv_kernel, gb=gb, n_chains=n_chains, c=c, n_heads=n_heads, passes=passes)
    gb = gb * n_chains
    vec = lambda w: pl.BlockSpec((1, w), lambda b, s: (0, 0))
    return pl.pallas_call(
        kern,
        grid=(bsz // gb, t_len // c),
        in_specs=[pl.BlockSpec((gb, c, n_shift), lambda b, s: (b, s, 0)),
                  pl.BlockSpec((gb, c, d), lambda b, s: (b, s, 0)),
                  pl.BlockSpec((gb, 1, n_shift), lambda b, s: (b, 0, 0)),
                  pl.BlockSpec((gb, d, HEAD_DIM), lambda b, s: (b, 0, 0)),
                  vec(n_shift), vec(d), vec(d),
                  pl.BlockSpec((LANES, 2 * d), lambda b, s: (0, 0)),
                  vec(d), vec(d), vec(d), vec(d), vec(d)],
        out_specs=[pl.BlockSpec((gb, c, d), lambda b, s: (b, s, 0)),
                   pl.BlockSpec((gb, d, d), lambda b, s: (b, 0, 0))],
        out_shape=[jax.ShapeDtypeStruct((bsz, t_len, d), BF16),
                   jax.ShapeDtypeStruct((bsz, d, d), F32)],
        scratch_shapes=[pltpu.VMEM((gb, 1, n_shift), F32)],
        compiler_params=_cparams(("parallel", "arbitrary")),
        name="rwkv_mix",
    )(a_sh, a_z, prev0, s0, lp["mu"], lp["w0"], lp["a0"], lp["w2a2"], lp["k_k"], lp["k_a"], lp["r_k"],
      lp["gn_g"], lp["gn_b"])


def _gdn_kernel(qkv_ref, small_ref, cz_ref, conv0_ref, s0_ref, convw_ref, alog_ref, dtb_ref, ng_ref,
                eb_ref, ea_ref, o_ref, s_ref, conv_scr, *, gb, n_chains, c, n_heads, passes):
    @pl.when(pl.program_id(1) == 0)
    def _():
        for b in range(s_ref.shape[0]):
            s_ref[b] = _block_diag(s0_ref[b])
        conv_scr[...] = conv0_ref[...]

    prm = dict(conv_w=convw_ref[...], a_log=alog_ref[...], dt_bias=dtb_ref[...], norm_g=ng_ref[...],
               e_beta=eb_ref[...], e_a=ea_ref[...])
    ins = []
    for ch in range(n_chains):
        b0 = ch * gb
        ins.append((qkv_ref[b0:b0 + gb].reshape(gb * c, qkv_ref.shape[-1]),
                    small_ref[b0:b0 + gb].reshape(gb * c, LANES),
                    cz_ref[b0:b0 + gb].reshape(gb * c, cz_ref.shape[-1]),
                    [conv_scr[b0 + b] for b in range(gb)],
                    [s_ref[b0 + b] for b in range(gb)]))
    outs = _run_interleaved([_gdn_chain(*args, prm, gb=gb, c=c, n_heads=n_heads, passes=passes) for args in ins])
    for ch, (o, s_new, last_rows) in enumerate(outs):
        b0 = ch * gb
        o_ref[b0:b0 + gb] = o.reshape(gb, c, o.shape[-1])
        for b in range(gb):
            s_ref[b0 + b] = s_new[b]
            conv_scr[b0 + b] = last_rows[b]


def _gdn_chain(x, small, z, prev_rows, s_old, prm, *, gb, c, n_heads, passes):
    d = n_heads * HEAD_DIM
    conv_w = prm["conv_w"]
    n_conv = conv_w.shape[0]
    y = x * conv_w[n_conv - 1:n_conv, :]
    for j in range(1, n_conv):
        y = y + _shift_rows(x, j, prev_rows, gb, c) * conv_w[n_conv - 1 - j:n_conv - j, :]
    last_rows = [x[(b + 1) * c - (n_conv - 1):(b + 1) * c, :] for b in range(gb)]
    y = _silu(y)
    q = y[:, 0:d]
    k = y[:, d:2 * d]
    v = y[:, 2 * d:3 * d]
    ones_bd = _head_block_ones(d)
    q = q * lax.rsqrt(_mm_exact_rhs(q * q, ones_bd) + L2_EPS) * (HEAD_DIM ** -0.5)
    k = k * lax.rsqrt(_mm_exact_rhs(k * k, ones_bd) + L2_EPS)

    beta = _sigmoid(_mm_exact_rhs(small, prm["e_beta"]))
    g = -jnp.exp(prm["a_log"]) * _softplus(_mm_exact_rhs(small, prm["e_a"]) + prm["dt_bias"])
    yield
    gc = _cumsum_rows(g, gb, c)
    yield
    gc_last = jnp.concatenate(
        [jnp.broadcast_to(gc[(b + 1) * c - 1:(b + 1) * c, :], (c, d)) for b in range(gb)], axis=0)
    egc = jnp.exp(gc)

    st = functools.partial(_stack_heads, gb=gb, c=c, n_heads=n_heads)
    n_rows = gb * c
    n = n_rows * n_heads
    kb = k * beta
    strict, incl, _ = _ls_masks(gb, c, n_heads)
    gcol = jnp.min(st(gc), axis=1, keepdims=True)
    eye = _iota((n, n), 0) == _iota((n, n), 1)
    grow = jnp.sum(jnp.where(eye, jnp.broadcast_to(gcol, (n, n)), 0.0), axis=0, keepdims=True)
    if gb == 1 and c == HEAD_DIM:
        gc_ls = gc
    else:
        lane_h = _div_pow2(_iota((d, n), 1), c) & (n_heads - 1)
        pick = jnp.where(_iota((d, n), 0) == lane_h * HEAD_DIM, 1.0, 0.0).astype(BF16)
        gc_ls = _mm_exact_rhs(gc, pick)
    dmat = jnp.exp(jnp.where(incl, gc_ls - grow, 0.0))
    g2 = _mm(jnp.concatenate([kb, q], axis=0), st(k), _NT, passes)
    yield
    m = jnp.where(strict, g2[0:n_rows] * dmat, 0.0)
    qk = jnp.where(incl, g2[n_rows:2 * n_rows] * dmat, 0.0)
    t_inv = yield from _unit_lower_inverse(-m, gb, c, n_heads, passes)
    rhs = jnp.concatenate([st(v * beta), st(kb * egc)], axis=1)
    sol = _mm(t_inv, rhs, passes=passes)
    yield
    u = sol[:, 0:d]
    w = sol[:, d:2 * d]

    rows = [slice(b * c, (b + 1) * c) for b in range(gb)]
    qg = q * egc
    kd = k * jnp.exp(gc_last - gc)
    v_new = u - jnp.concatenate([_mm(w[sl], s_old[b], passes=passes) for b, sl in enumerate(rows)], axis=0)
    o = jnp.concatenate([_mm(qg[sl], s_old[b], passes=passes) for b, sl in enumerate(rows)], axis=0)
    yield
    o = o + _mm(qk, st(v_new), passes=passes)
    same_head = _div_pow2(_iota((d, d), 0), HEAD_DIM) == _div_pow2(_iota((d, d), 1), HEAD_DIM)
    s_new = []
    for b, sl in enumerate(rows):
        gl = jnp.exp(gc[(b + 1) * c - 1:(b + 1) * c, :])
        s_new.append(s_old[b] * gl + jnp.where(same_head, _mm(kd[sl], v_new[sl], _TN, passes), 0.0))
    yield

    ms = _mm_exact_rhs(o * o, ones_bd) * (1.0 / HEAD_DIM)
    o = o * lax.rsqrt(ms + GDN_NORM_EPS) * prm["norm_g"]
    return (o * _silu(z)).astype(BF16), s_new, last_rows


def _gdn_mix(c_qkv, small, c_z, conv0, s0, lp, gb, n_chains, c, passes):
    bsz, t_len, n_qkv = c_qkv.shape
    d = c_z.shape[-1]
    n_heads = d // HEAD_DIM
    n_conv = lp["conv_w"].shape[0]
    kern = functools.partial(_gdn_kernel, gb=gb, n_chains=n_chains, c=c, n_heads=n_heads, passes=passes)
    gb = gb * n_chains
    vec = lambda w: pl.BlockSpec((1, w), lambda b, s: (0, 0))
    return pl.pallas_call(
        kern,
        grid=(bsz // gb, t_len // c),
        in_specs=[pl.BlockSpec((gb, c, n_qkv), lambda b, s: (b, s, 0)),
                  pl.BlockSpec((gb, c, LANES), lambda b, s: (b, s, 0)),
                  pl.BlockSpec((gb, c, d), lambda b, s: (b, s, 0)),
                  pl.BlockSpec((gb, n_conv - 1, n_qkv), lambda b, s: (b, 0, 0)),
                  pl.BlockSpec((gb, d, HEAD_DIM), lambda b, s: (b, 0, 0)),
                  pl.BlockSpec((n_conv, n_qkv), lambda b, s: (0, 0)),
                  vec(d), vec(d), vec(d),
                  pl.BlockSpec((LANES, d), lambda b, s: (0, 0)),
                  pl.BlockSpec((LANES, d), lambda b, s: (0, 0))],
        out_specs=[pl.BlockSpec((gb, c, d), lambda b, s: (b, s, 0)),
                   pl.BlockSpec((gb, d, d), lambda b, s: (b, 0, 0))],
        out_shape=[jax.ShapeDtypeStruct((bsz, t_len, d), BF16),
                   jax.ShapeDtypeStruct((bsz, d, d), F32)],
        scratch_shapes=[pltpu.VMEM((gb, n_conv - 1, n_qkv), F32)],
        compiler_params=_cparams(("parallel", "arbitrary")),
        name="gdn_mix",
    )(c_qkv, small, c_z, conv0, s0, lp["conv_w"], lp["a_log"], lp["dt_bias"], lp["norm_g"],
      lp["e_beta"], lp["e_a"])


def _log_forget_cumsum(small, bf, carry_scr):
    @pl.when(pl.program_id(1) == 0)
    def _():
        carry_scr[...] = jnp.zeros_like(carry_scr)

    logf = -_softplus(-(small + bf))
    tb = logf.shape[0]
    c = _cumsum_rows(logf, 1, tb) + carry_scr[...]
    carry_scr[...] = c[tb - 1:tb, :]
    return logf, c


def _fox_gate_kernel(small_ref, bf_ref, logf_ref, ccol_ref, crow_ref, carry_scr, *, n_heads):
    logf, c = _log_forget_cumsum(small_ref[...], bf_ref[...], carry_scr)
    logf_ref[...] = logf[:, 0:n_heads]
    ccol_ref[...] = c
    sel = jnp.where(_iota((n_heads, LANES), 0) == _iota((n_heads, LANES), 1), 1.0, 0.0).astype(BF16)
    crow_ref[...] = _mm_exact_lhs(sel, c, _NT)


def _fox_gates(small, bf_pad, n_heads, tb):
    bsz, t_len, _ = small.shape
    kern = functools.partial(_fox_gate_kernel, n_heads=n_heads)
    return pl.pallas_call(
        kern,
        grid=(bsz, t_len // tb),
        in_specs=[pl.BlockSpec((None, tb, LANES), lambda b, s: (b, s, 0)),
                  pl.BlockSpec((1, LANES), lambda b, s: (0, 0))],
        out_specs=[pl.BlockSpec((None, tb, n_heads), lambda b, s: (b, s, 0)),
                   pl.BlockSpec((None, tb, LANES), lambda b, s: (b, s, 0)),
                   pl.BlockSpec((None, n_heads, tb), lambda b, s: (b, 0, s))],
        out_shape=[jax.ShapeDtypeStruct((bsz, t_len, n_heads), F32),
                   jax.ShapeDtypeStruct((bsz, t_len, LANES), F32),
                   jax.ShapeDtypeStruct((bsz, n_heads, t_len), F32)],
        scratch_shapes=[pltpu.VMEM((1, LANES), F32)],
        compiler_params=_cparams(("parallel", "arbitrary")),
        name="fox_gates",
    )(small, bf_pad)


N_AUX = 3


def _proj_prompt_kernel(*refs, widths, n_heads, ln_in, n_alias):
    refs = list(refs)
    x_ref, w_ref, bf_ref, eaux_ref = refs[0:4]
    pos = 4
    if ln_in:
        g_ref, b_ref = refs[pos:pos + 2]
        pos += 2
    pos += n_alias
    (ash_ref, az_ref, bz_ref, cqkv_ref, cz_ref, small_ref, k_ref, v_ref, logf_ref,
     qa_ref, ka_ref, va_ref) = refs[pos:pos + 12]
    pos += 12
    if ln_in:
        xn_ref = refs[pos]
        pos += 1
    carry_scr = refs[pos]

    x = x_ref[...]
    if ln_in:
        mu = jnp.mean(x, axis=-1, keepdims=True)
        xc = x - mu
        var = jnp.mean(xc * xc, axis=-1, keepdims=True)
        x = xc * lax.rsqrt(var + LN_EPS) * g_ref[...] + b_ref[...]
        xn_ref[...] = x
    xb = x.astype(BF16)
    offs = np.concatenate([[0], np.cumsum(widths)])
    seg = lambda i: jnp.dot(xb, w_ref[:, int(offs[i]):int(offs[i + 1])], preferred_element_type=F32)
    ash_ref[...] = seg(0)
    az_ref[...] = seg(1)
    q, k, v = seg(2), seg(3), seg(4)
    bz_ref[...] = seg(5)
    cqkv_ref[...] = seg(6)
    cz_ref[...] = seg(7)
    small = seg(8)
    small_ref[...] = small
    k_ref[...] = k
    v_ref[...] = v

    logf, c = _log_forget_cumsum(small, bf_ref[...], carry_scr)
    logf_ref[...] = logf[:, 0:n_heads]
    parts = jnp.concatenate(_split3(c * LOG2E), axis=1)
    lane = _iota((1, LANES), 1)
    for h in range(n_heads):
        pair, hh = divmod(h, 2)
        own = (lane < HEAD_DIM) if hh == 0 else (lane >= HEAD_DIM)
        a0 = HEAD_DIM if hh == 0 else 0
        aux_q = jnp.where((lane >= a0) & (lane < a0 + N_AUX), 1.0, 0.0)
        aux_k = _dg(parts, eaux_ref[h], _NN)
        aux_v = jnp.where(lane == a0, 1.0, 0.0)
        cols = slice(pair * LANES, (pair + 1) * LANES)
        qa_ref[h] = jnp.where(own, q[:, cols] * (HEAD_DIM ** -0.5 * LOG2E), aux_q).astype(BF16)
        ka_ref[h] = jnp.where(own, k[:, cols], aux_k).astype(BF16)
        va_ref[h] = jnp.where(own, v[:, cols], aux_v).astype(BF16)


def _project_prompt(x3, w_all, layer, widths, bf_pad, e_aux, n_heads, tm, ln_params, stacked):
    bsz, t_len, d = x3.shape
    depth, _, n_cols = w_all.shape
    d_fox = n_heads * HEAD_DIM
    ln_in = ln_params is not None
    n_alias = 0 if stacked is None else len(stacked)
    kern = functools.partial(_proj_prompt_kernel, widths=tuple(widths), n_heads=n_heads, ln_in=ln_in,
                             n_alias=n_alias)
    row = lambda w: pl.BlockSpec((None, tm, w), lambda b, s: (b, s, 0))
    per_layer = lambda w: pl.BlockSpec((None, None, tm, w), lambda b, s: (layer, b, s, 0))
    per_head = pl.BlockSpec((None, n_heads, tm, LANES), lambda b, s: (b, 0, s, 0))
    const = lambda shape: pl.BlockSpec(shape, lambda b, s: (0,) * len(shape))
    in_specs = [row(d), pl.BlockSpec((None, d, n_cols), lambda b, s: (layer, 0, 0), pipeline_mode=pl.Buffered(1)),
                const((1, LANES)), const((n_heads, N_AUX * LANES, LANES))]
    args = [x3, w_all, bf_pad, e_aux]
    if ln_in:
        in_specs += [const((1, d)), const((1, d))]
        args += [p.reshape(1, d) for p in ln_params]
    aliases = {}
    if stacked is not None:
        for j, buf in enumerate(stacked):
            aliases[len(args)] = 6 + j
            in_specs.append(pl.BlockSpec(memory_space=pl.ANY))
            args.append(buf)
    f32 = lambda *shape: jax.ShapeDtypeStruct(shape, F32)
    out_specs = [row(widths[0]), row(widths[1]), row(widths[5]), row(widths[6]), row(widths[7]), row(LANES),
                 per_layer(d_fox), per_layer(d_fox), per_layer(n_heads), per_head, per_head, per_head]
    out_shape = [f32(bsz, t_len, widths[0]), f32(bsz, t_len, widths[1]), f32(bsz, t_len, widths[5]),
                 f32(bsz, t_len, widths[6]), f32(bsz, t_len, widths[7]), f32(bsz, t_len, LANES),
                 f32(depth, bsz, t_len, d_fox), f32(depth, bsz, t_len, d_fox), f32(depth, bsz, t_len, n_heads)]
    out_shape += [jax.ShapeDtypeStruct((bsz, n_heads, t_len, LANES), BF16)] * 3
    if ln_in:
        out_specs.append(row(d))
        out_shape.append(f32(bsz, t_len, d))
    return pl.pallas_call(
        kern,
        grid=(bsz, t_len // tm),
        in_specs=in_specs,
        out_specs=out_specs,
        out_shape=out_shape,
        input_output_aliases=aliases,
        scratch_shapes=[pltpu.VMEM((1, LANES), F32)],
        compiler_params=_cparams(("parallel", "arbitrary")),
        name="proj_prompt",
    )(*args)


def _fox_prompt_kernel(qa_ref, ka_ref, va_ref, bz_ref, o_ref, s_scr, m_scr, acc_scr, *, blk, n_h):
    qi = pl.program_id(2)
    m_scr[...] = jnp.full_like(m_scr, NEG_BIG)
    acc_scr[...] = jnp.zeros_like(acc_scr)

    def logits(ki, slot):
        k0 = pl.multiple_of(ki * blk, blk)
        for h in range(n_h):
            s_scr[slot, h] = _dg(qa_ref[h], ka_ref[h, pl.ds(k0, blk), :], _NT)

    def consume(ki, slot, masked):
        k0 = pl.multiple_of(ki * blk, blk)
        for h in range(n_h):
            s = s_scr[slot, h]
            if masked:
                s = jnp.where(_iota((blk, blk), 1) <= _iota((blk, blk), 0), s, NEG_BIG)
            m_prev = m_scr[h]
            m_new = jnp.maximum(m_prev, jnp.max(s, axis=1, keepdims=True))
            alpha = jnp.exp2(m_prev - m_new)
            p = jnp.exp2(s - jnp.tile(m_new, (1, blk // LANES)))
            acc_scr[h] = alpha * acc_scr[h] + _dg(p.astype(BF16), va_ref[h, pl.ds(k0, blk), :], _NN)
            m_scr[h] = m_new

    def two_blocks(j, carry):
        logits(2 * j + 1, 0)
        consume(2 * j, 1, False)
        logits(jnp.minimum(2 * j + 2, qi - 1), 1)
        consume(2 * j + 1, 0, False)
        return carry

    logits(qi, 0)

    @pl.when(qi == 0)
    def _():
        consume(qi, 0, True)

    @pl.when(qi > 0)
    def _():
        logits(0, 1)
        consume(qi, 0, True)
        lax.fori_loop(0, qi // 2, two_blocks, 0)

        @pl.when(qi % 2 == 1)
        def _():
            consume(qi - 1, 1, False)

    lane = _iota((1, LANES), 1)
    outs = []
    for pair in range(n_h // 2):
        acc0 = acc_scr[2 * pair]
        acc1 = acc_scr[2 * pair + 1]
        outs.append(jnp.where(lane < HEAD_DIM, acc0 / acc0[:, HEAD_DIM:HEAD_DIM + 1], acc1 / acc1[:, 0:1]))
    o = jnp.concatenate(outs, axis=1) if len(outs) > 1 else outs[0]
    o_ref[...] = (o * _silu(bz_ref[...])).astype(BF16)


def _fox_prompt(qa, ka, va, b_z, blk):
    bsz, n_heads, t_len, _ = qa.shape
    d = b_z.shape[-1]
    n_h = min(n_heads, FOX_HEADS)
    w = (n_h // 2) * LANES
    return pl.pallas_call(
        functools.partial(_fox_prompt_kernel, blk=blk, n_h=n_h),
        grid=(bsz, n_heads // n_h, t_len // blk),
        in_specs=[pl.BlockSpec((None, n_h, blk, LANES), lambda b, p, i: (b, p, i, 0)),
                  pl.BlockSpec((None, n_h, t_len, LANES), lambda b, p, i: (b, p, 0, 0)),
                  pl.BlockSpec((None, n_h, t_len, LANES), lambda b, p, i: (b, p, 0, 0)),
                  pl.BlockSpec((None, blk, w), lambda b, p, i: (b, i, p))],
        out_specs=pl.BlockSpec((None, blk, w), lambda b, p, i: (b, i, p)),
        out_shape=jax.ShapeDtypeStruct((bsz, t_len, d), BF16),
        scratch_shapes=[pltpu.VMEM((2, n_h, blk, blk), F32),
                        pltpu.VMEM((n_h, blk, LANES), F32),
                        pltpu.VMEM((n_h, blk, LANES), F32)],
        compiler_params=_cparams(("parallel", "parallel", "arbitrary")),
        name="fox_prompt",
    )(qa, ka, va, b_z)


def _fox_cached_kernel(q_ref, k_ref, v_ref, ck_ref, cv_ref, clf_ref, ccol_ref, crow_ref, bz_ref, o_ref,
                       *, n_heads, lane_blk):
    t_len, d = q_ref.shape
    past = ck_ref.shape[1]
    n_rows = n_heads * t_len

    triu = jnp.where(_iota((lane_blk, lane_blk), 0) <= _iota((lane_blk, lane_blk), 1), 1.0, 0.0).astype(BF16)
    carry = jnp.zeros((n_heads, 1), F32)
    pieces = []
    for j in range(past // lane_blk):
        cj = _mm_exact_rhs(clf_ref[:, j * lane_blk:(j + 1) * lane_blk], triu) + carry
        carry = cj[:, lane_blk - 1:lane_blk]
        pieces.append(cj)
    c_cache = jnp.concatenate(pieces, axis=1)
    total = carry

    rows = lambda f: jnp.concatenate([f(h) for h in range(n_heads)], axis=0)
    ck_cache = rows(lambda h: jnp.broadcast_to(c_cache[h:h + 1, :], (t_len, past)))
    ck_new = rows(lambda h: jnp.broadcast_to(total[h:h + 1, :] + crow_ref[h:h + 1, :], (t_len, t_len)))
    cq = rows(lambda h: total[h:h + 1, :] + ccol_ref[:, h:h + 1])

    lane_head = _div_pow2(_iota((1, d), 1), HEAD_DIM)
    q = q_ref[...] * (HEAD_DIM ** -0.5)
    q_bd = rows(lambda h: jnp.where(lane_head == h, q, 0.0)).astype(BF16)
    s_c = _dg(q_bd, ck_ref[...].astype(BF16), _NN) + cq - ck_cache
    s_n = _dg(q_bd, k_ref[...].astype(BF16), _NT) + cq - ck_new
    q_pos = _iota((n_rows, t_len), 0) & (t_len - 1)
    s_n = jnp.where(_iota((n_rows, t_len), 1) <= q_pos, s_n, NEG_BIG)
    m = jnp.maximum(jnp.max(s_c, axis=1, keepdims=True), jnp.max(s_n, axis=1, keepdims=True))
    p_c = jnp.exp(s_c - m)
    p_n = jnp.exp(s_n - m)
    l = jnp.sum(p_c, axis=1, keepdims=True) + jnp.sum(p_n, axis=1, keepdims=True)
    o_all = (_dg(p_c.astype(BF16), cv_ref[...].astype(BF16), _NT)
             + _dg(p_n.astype(BF16), v_ref[...].astype(BF16), _NN)) / l
    o = jnp.where(lane_head == 0, o_all[0:t_len], 0.0)
    for h in range(1, n_heads):
        o = o + jnp.where(lane_head == h, o_all[h * t_len:(h + 1) * t_len], 0.0)
    o_ref[...] = (o * _silu(bz_ref[...])).astype(BF16)


def _fox_cached(q, k, v, cache_k, cache_v, cache_logf_t, ccol, crow, b_z, layer):
    bsz, t_len, d = q.shape
    n_heads = d // HEAD_DIM
    past = cache_k.shape[3]
    kern = functools.partial(_fox_cached_kernel, n_heads=n_heads, lane_blk=min(past, 256))
    cur = lambda w: pl.BlockSpec((None, t_len, w), lambda b: (b, 0, 0))
    return pl.pallas_call(
        kern,
        grid=(bsz,),
        in_specs=[cur(d), cur(d), cur(d),
                  pl.BlockSpec((None, None, d, past), lambda b: (layer, b, 0, 0)),
                  pl.BlockSpec((None, None, d, past), lambda b: (layer, b, 0, 0)),
                  pl.BlockSpec((None, None, n_heads, past), lambda b: (layer, b, 0, 0)),
                  cur(LANES),
                  pl.BlockSpec((None, n_heads, t_len), lambda b: (b, 0, 0)),
                  cur(d)],
        out_specs=cur(d),
        out_shape=jax.ShapeDtypeStruct((bsz, t_len, d), BF16),
        compiler_params=_cparams(("parallel",)),
        name="fox_cached",
    )(q, k, v, cache_k, cache_v, cache_logf_t, ccol, crow, b_z)


def _out_kernel(oa_ref, ob_ref, oc_ref, x_ref, w_ref, g_ref, b_ref, y_ref, *, alpha):
    da = oa_ref.shape[-1]
    db = ob_ref.shape[-1]
    h = jnp.dot(oa_ref[...], w_ref[0:da, :], preferred_element_type=F32)
    h = h + jnp.dot(ob_ref[...], w_ref[da:da + db, :], preferred_element_type=F32)
    h = h + jnp.dot(oc_ref[...], w_ref[da + db:, :], preferred_element_type=F32)
    y = alpha * x_ref[...] + h
    mu = jnp.mean(y, axis=-1, keepdims=True)
    yc = y - mu
    var = jnp.mean(yc * yc, axis=-1, keepdims=True)
    y_ref[...] = yc * lax.rsqrt(var + LN_EPS) * g_ref[...] + b_ref[...]


def _out_project(o_a, o_b, o_c, x2, w_all, layer, g, b, alpha, tm):
    n, d = x2.shape
    d_mix = w_all.shape[1]
    row = lambda w: pl.BlockSpec((tm, w), lambda i: (i, 0))
    return pl.pallas_call(
        functools.partial(_out_kernel, alpha=alpha),
        grid=(n // tm,),
        in_specs=[row(o_a.shape[-1]), row(o_b.shape[-1]), row(o_c.shape[-1]), row(d),
                  pl.BlockSpec((None, d_mix, d), lambda i: (layer, 0, 0)),
                  pl.BlockSpec((1, d), lambda i: (0, 0)),
                  pl.BlockSpec((1, d), lambda i: (0, 0))],
        out_specs=row(d),
        out_shape=jax.ShapeDtypeStruct((n, d), F32),
        compiler_params=_cparams(("parallel",)),
        name="proj_out",
    )(o_a, o_b, o_c, x2, w_all, g.reshape(1, d), b.reshape(1, d))


def _diag_blocks(s, n_h):
    bsz = s.shape[0]
    s5 = s.reshape(bsz, n_h, HEAD_DIM, n_h, HEAD_DIM)
    return jnp.stack([s5[:, h, :, h, :] for h in range(n_h)], axis=1)


def _expand_heads(p):
    return jnp.repeat(p.astype(F32), HEAD_DIM).reshape(1, -1)


def _select_matrix(rows, n_cols, first_row):
    e = np.zeros((LANES, n_cols), np.float32)
    for h in range(rows):
        e[first_row + h, h * HEAD_DIM:(h + 1) * HEAD_DIM] = 1.0
    return jnp.asarray(e, BF16)


def _row_tile(n, largest=256):
    for tm in (1024, 512, 256, 128, 64, 32, 16, 8):
        if tm <= largest and n % tm == 0:
            return tm
    raise ValueError(f"row count {n} must be a multiple of 8")


def kernel(x_prompt, x_sample, cache_fox_k, cache_fox_v, cache_fox_logf, state_rwkv_shift, state_rwkv_wkv,
           state_gdn_conv, state_gdn_wkv, ln_in_g, ln_in_b, w_in, rwkv_mu, rwkv_w0, rwkv_w2, rwkv_a0, rwkv_a2,
           rwkv_k_k, rwkv_k_a, rwkv_r_k, rwkv_gn_g, rwkv_gn_b, fox_b_f, gdn_conv_w, gdn_a_log, gdn_dt_bias,
           gdn_norm_g, w_out, ln_post_g, ln_post_b):
    depth, d_model, _ = w_in.shape
    bp, seq, _ = x_prompt.shape
    bs, dec_seq, _ = x_sample.shape
    h_fox = fox_b_f.shape[1]
    h_gdn = gdn_a_log.shape[1]
    h_rwkv = rwkv_r_k.shape[1]
    d_rwkv, d_fox, d_gdn = h_rwkv * HEAD_DIM, h_fox * HEAD_DIM, h_gdn * HEAD_DIM
    rank_w = rwkv_w2.shape[1]
    rank_a = rwkv_a2.shape[1]
    n_shift = 3 * d_rwkv + rank_w + rank_a
    n_conv_cols = 3 * d_gdn
    n_conv = gdn_conv_w.shape[1]
    alpha = (2 * depth) ** 0.25
    assert rank_w + rank_a == LANES and h_fox + 2 * h_gdn <= LANES

    split = [n_shift, d_rwkv, d_fox, d_fox, d_fox, h_fox, d_fox, n_conv_cols, h_gdn, h_gdn, d_gdn]
    offs = np.concatenate([[0], np.cumsum(split)])
    seg = lambda i: w_in[:, :, offs[i]:offs[i + 1]]
    small_w = jnp.concatenate([seg(5), seg(8), seg(9)], axis=-1)
    small_w = jnp.pad(small_w, ((0, 0), (0, 0), (0, LANES - small_w.shape[-1])))
    w_in_p = jnp.concatenate([seg(0), seg(1), seg(2), seg(3), seg(4), seg(6), seg(7), seg(10), small_w],
                             axis=-1).astype(BF16)
    widths = [n_shift, d_rwkv, d_fox, d_fox, d_fox, d_fox, n_conv_cols, d_gdn, LANES]
    w_out_b = w_out.astype(BF16)

    zw = jnp.zeros((depth, rank_w, d_rwkv), F32)
    w2a2 = jnp.concatenate([jnp.concatenate([rwkv_w2, zw], axis=-1),
                            jnp.concatenate([zw, rwkv_a2], axis=-1)], axis=1).astype(BF16)
    bf_pad = jnp.pad(fox_b_f, ((0, 0), (0, LANES - h_fox)))
    e_beta = _select_matrix(h_gdn, d_gdn, h_fox)
    e_a = _select_matrix(h_gdn, d_gdn, h_fox + h_gdn)
    e_aux_np = np.zeros((h_fox, N_AUX * LANES, LANES), np.float32)
    for h in range(h_fox):
        for j in range(N_AUX):
            e_aux_np[h, j * LANES + h, (HEAD_DIM if h % 2 == 0 else 0) + j] = -1.0
    e_aux = jnp.asarray(e_aux_np, BF16)
    cache_logf_t = jnp.swapaxes(cache_fox_logf, 2, 3)
    cache_k_t = jnp.swapaxes(cache_fox_k.reshape(depth, bs, -1, d_fox), 2, 3)
    cache_v_t = jnp.swapaxes(cache_fox_v.reshape(depth, bs, -1, d_fox), 2, 3)

    def layer_params(l):
        row = lambda a: a[l].reshape(1, -1).astype(F32)
        return dict(
            rwkv=dict(mu=row(rwkv_mu), w0=row(rwkv_w0), a0=row(rwkv_a0), w2a2=w2a2[l], k_k=row(rwkv_k_k),
                      k_a=row(rwkv_k_a), r_k=row(rwkv_r_k), gn_g=row(rwkv_gn_g), gn_b=row(rwkv_gn_b)),
            gdn=dict(conv_w=gdn_conv_w[l], a_log=_expand_heads(gdn_a_log[l]),
                     dt_bias=_expand_heads(gdn_dt_bias[l]),
                     norm_g=jnp.tile(gdn_norm_g[l].reshape(1, -1), (1, h_gdn)), e_beta=e_beta, e_a=e_a),
        )

    def mix_and_project(l, x2, bsz, t_len, a_sh, a_z, c_qkv, small, c_z, o_b, prev0, s_rwkv0, conv0, s_gdn0):
        lp = layer_params(l)
        c = min(HEAD_DIM, t_len)
        gb = N_STACK // (h_rwkv * c)
        n_chains = max(1, min(N_CHAINS, bsz // gb))
        o_a, s_rwkv = _rwkv_mix(a_sh, a_z, prev0, s_rwkv0, lp["rwkv"], gb, n_chains, c, CHUNK_PASSES)
        gb = N_STACK // (h_gdn * c)
        o_c, s_gdn = _gdn_mix(c_qkv, small, c_z, conv0, s_gdn0, lp["gdn"], gb, n_chains, c, CHUNK_PASSES)
        f2 = lambda a: a.reshape(bsz * t_len, a.shape[-1])
        x_new = _out_project(f2(o_a), f2(o_b), f2(o_c), x2, w_out_b, l, ln_post_g[l], ln_post_b[l], alpha,
                             _row_tile(bsz * t_len, OUT_ROWS))
        outs = (a_sh[:, -1], _diag_blocks(s_rwkv, h_rwkv), c_qkv[:, t_len - (n_conv - 1):],
                _diag_blocks(s_gdn, h_gdn))
        return x_new, outs

    def prompt_layer(l, x2, stacked):
        ln_params = (ln_in_g, ln_in_b) if l == 0 else None
        res = _project_prompt(x2.reshape(bp, seq, d_model), w_in_p, l, widths, bf_pad[l:l + 1], e_aux, h_fox,
                              min(seq, PROJ_ROWS), ln_params, stacked)
        a_sh, a_z, b_z, c_qkv, c_z, small, k_st, v_st, logf_st, qa, ka, va = res[:12]
        if l == 0:
            x2 = res[12].reshape(bp * seq, d_model)
        o_b = _fox_prompt(qa, ka, va, b_z, min(seq, FOX_BLOCK))
        x_new, outs = mix_and_project(l, x2, bp, seq, a_sh, a_z, c_qkv, small, c_z, o_b, *zeros_p)
        return x_new, outs, (k_st, v_st, logf_st)

    def sample_layer(l, x2):
        tm = _row_tile(bs * dec_seq)
        r3 = lambda a: a.reshape(bs, dec_seq, a.shape[-1])
        a_sh, a_z, b_q, b_k, b_v, b_z, c_qkv, c_z, small = map(r3, _project(x2, w_in_p, l, widths, tm))
        logf, ccol, crow = _fox_gates(small, bf_pad[l:l + 1], h_fox, min(dec_seq, 256))
        o_b = _fox_cached(b_q, b_k, b_v, cache_k_t, cache_v_t, cache_logf_t, ccol, crow, b_z, l)
        x_new, outs = mix_and_project(l, x2, bs, dec_seq, a_sh, a_z, c_qkv, small, c_z, o_b,
                                      state_rwkv_shift[l][:, None, :], state_rwkv_wkv[l].reshape(bs, d_rwkv, HEAD_DIM),
                                      state_gdn_conv[l], state_gdn_wkv[l].reshape(bs, d_gdn, HEAD_DIM))
        heads = lambda a: a.reshape(bs, dec_seq, h_fox, HEAD_DIM)
        return x_new, (heads(b_k), heads(b_v), logf) + outs

    xp = x_prompt.reshape(bp * seq, d_model)
    xs = _layer_norm(x_sample.reshape(bs * dec_seq, d_model), ln_in_g, ln_in_b, _row_tile(bs * dec_seq))
    zeros_p = (jnp.zeros((bp, 1, n_shift), F32), jnp.zeros((bp, d_rwkv, HEAD_DIM), F32),
               jnp.zeros((bp, n_conv - 1, n_conv_cols), F32), jnp.zeros((bp, d_gdn, HEAD_DIM), F32))
    outs_p, outs_s, stacked = [], [], None
    for l in range(depth):
        xp, o, stacked = prompt_layer(l, xp, stacked)
        outs_p.append(o)
        xs, o = sample_layer(l, xs)
        outs_s.append(o)
    k_st, v_st, logf_st = stacked
    stack = lambda outs, i: jnp.stack([o[i] for o in outs])
    return ((xp.reshape(bp, seq, d_model), xs.reshape(bs, dec_seq, d_model),
             k_st.reshape(depth, bp, seq, h_fox, HEAD_DIM), v_st.reshape(depth, bp, seq, h_fox, HEAD_DIM), logf_st)
            + tuple(stack(outs_p, i) for i in range(4)) + tuple(stack(outs_s, i) for i in range(7)))
```

```python
import functools
import math

import jax
import jax.numpy as jnp
import numpy as np
from jax import lax
from jax.experimental import pallas as pl
from jax.experimental.pallas import tpu as pltpu

F32 = jnp.float32
BF16 = jnp.bfloat16

HEAD_DIM = 64
LANES = 128
N_STACK = 256
PROJ_ROWS = 512
OUT_ROWS = 1024
FOX_BLOCK = 512
FOX_HEADS = 4
N_CHAINS = 8
CHUNK_PASSES = 1
LN_EPS = 1e-5
RWKV_GN_EPS = 64e-5
GDN_NORM_EPS = 1e-6
L2_EPS = 1e-6
NEG_BIG = -1e30
LOG2E = math.log2(math.e)
VMEM_LIMIT = 56 * 1024 * 1024


def _sigmoid(x):
    return 1.0 / (1.0 + jnp.exp(-x))


def _softplus(x):
    return jnp.maximum(x, 0.0) + jnp.log(1.0 + jnp.exp(-jnp.abs(x)))


def _silu(x):
    return x * _sigmoid(x)


def _split3(x):
    hi = x.astype(BF16)
    r1 = x - hi.astype(F32)
    mid = r1.astype(BF16)
    lo = (r1 - mid.astype(F32)).astype(BF16)
    return hi, mid, lo


_NN = (((1,), (0,)), ((), ()))
_NT = (((1,), (1,)), ((), ()))
_TN = (((0,), (0,)), ((), ()))


def _dg(a, b, dims):
    return lax.dot_general(a, b, dims, preferred_element_type=F32)


def _mm(a, b, dims=_NN, passes=1):
    if passes == 1:
        return _dg(a.astype(BF16), b.astype(BF16), dims)
    ah, am, _ = _split3(a)
    bh, bm, _ = _split3(b)
    return _dg(ah, bh, dims) + (_dg(ah, bm, dims) + _dg(am, bh, dims))


def _mm_exact_rhs(a, e, dims=_NN):
    m = a.shape[0]
    r = _dg(jnp.concatenate(_split3(a), axis=0), e, dims)
    return r[0:m] + (r[m:2 * m] + r[2 * m:3 * m])


def _mm_exact_lhs(e, b, dims=_NN):
    hi, mid, lo = _split3(b)
    return _dg(e, hi, dims) + (_dg(e, mid, dims) + _dg(e, lo, dims))


def _iota(shape, dim):
    return lax.broadcasted_iota(jnp.int32, shape, dim)


def _div_pow2(x, n):
    assert n & (n - 1) == 0
    return lax.shift_right_logical(x, jnp.int32(int(math.log2(n))))


def _head_block_ones(n):
    r = _div_pow2(_iota((n, n), 0), HEAD_DIM)
    c = _div_pow2(_iota((n, n), 1), HEAD_DIM)
    return jnp.where(r == c, 1.0, 0.0).astype(BF16)


def _block_diag(tall):
    n = tall.shape[0]
    wide = jnp.concatenate([tall] * (n // HEAD_DIM), axis=1)
    same = _div_pow2(_iota((n, n), 0), HEAD_DIM) == _div_pow2(_iota((n, n), 1), HEAD_DIM)
    return jnp.where(same, wide, 0.0)


def _diag_tall(s_bd):
    n = s_bd.shape[0]
    return jnp.concatenate([s_bd[h * HEAD_DIM:(h + 1) * HEAD_DIM, h * HEAD_DIM:(h + 1) * HEAD_DIM]
                            for h in range(n // HEAD_DIM)], axis=0)


def _stack_heads(x, gb, c, n_heads):
    width = x.shape[1]
    lane_head = _div_pow2(_iota((1, width), 1), HEAD_DIM)
    parts = []
    for b in range(gb):
        xb = x[b * c:(b + 1) * c, :]
        for h in range(n_heads):
            parts.append(jnp.where(lane_head == h, xb, 0.0))
    return jnp.concatenate(parts, axis=0)


def _ls_masks(gb, c, n_heads):
    shape = (gb * c, gb * n_heads * c)
    row = _iota(shape, 0)
    lane = _iota(shape, 1)
    same_b = _div_pow2(row, c) == _div_pow2(lane, n_heads * c)
    i = row & (c - 1)
    j = lane & (c - 1)
    return same_b & (j < i), same_b & (j <= i), same_b & (j == i)


def _expand_bd(m_ls, gb, c, n_heads):
    lane_h = _div_pow2(_iota((1, m_ls.shape[1]), 1), c) & (n_heads - 1)
    parts = []
    for b in range(gb):
        mb = m_ls[b * c:(b + 1) * c, :]
        for h in range(n_heads):
            parts.append(jnp.where(lane_h == h, mb, 0.0))
    return jnp.concatenate(parts, axis=0)


def _unit_lower_inverse(p, gb, c, n_heads, passes):
    n_rows = p.shape[0]
    expand = functools.partial(_expand_bd, gb=gb, c=c, n_heads=n_heads)
    t = jnp.where(_ls_masks(gb, c, n_heads)[2], 1.0, 0.0) + p
    n_rounds = int(math.log2(c)) - 1
    pk = _mm(p, expand(p), passes=passes)
    yield
    for i in range(n_rounds):
        if i + 1 < n_rounds:
            both = _mm(jnp.concatenate([t, pk], axis=0), expand(pk), passes=passes)
            t = t + both[0:n_rows]
            pk = both[n_rows:2 * n_rows]
        else:
            t = t + _mm(t, expand(pk), passes=passes)
        yield
    return t


def _run_interleaved(chains):
    results = [None] * len(chains)
    active = list(enumerate(chains))
    while active:
        still = []
        for i, ch in active:
            try:
                next(ch)
                still.append((i, ch))
            except StopIteration as stop:
                results[i] = stop.value
        active = still
    return results


def _chunk_masks(n, c):
    row = _iota((n, n), 0)
    col = _iota((n, n), 1)
    same = _div_pow2(row, c) == _div_pow2(col, c)
    strict = jnp.logical_and(same, row > col)
    incl = jnp.logical_and(same, row >= col)
    return strict, incl


def _cumsum_rows(x, gb, c):
    n = gb * c
    strict, incl = _chunk_masks(n, c)
    del strict
    tri = jnp.where(incl, 1.0, 0.0).astype(BF16)
    return _mm_exact_lhs(tri, x)


def _shift_rows(x, j, prev_rows, gb, c):
    n_prev = prev_rows[0].shape[0]
    out = pltpu.roll(x, j, 0)
    rid = _iota((x.shape[0], 1), 0)
    for b in range(gb):
        for t in range(j):
            src = n_prev + t - j
            out = jnp.where(rid == b * c + t, prev_rows[b][src:src + 1, :], out)
    return out


def _cparams(sem):
    return pltpu.CompilerParams(dimension_semantics=sem, vmem_limit_bytes=VMEM_LIMIT)


def _ln_kernel(x_ref, g_ref, b_ref, o_ref):
    x = x_ref[...]
    mu = jnp.mean(x, axis=-1, keepdims=True)
    xc = x - mu
    var = jnp.mean(xc * xc, axis=-1, keepdims=True)
    o_ref[...] = xc * lax.rsqrt(var + LN_EPS) * g_ref[...] + b_ref[...]


def _layer_norm(x2, g, b, tm):
    n, d = x2.shape
    return pl.pallas_call(
        _ln_kernel,
        grid=(n // tm,),
        in_specs=[pl.BlockSpec((tm, d), lambda i: (i, 0)),
                  pl.BlockSpec((1, d), lambda i: (0, 0)),
                  pl.BlockSpec((1, d), lambda i: (0, 0))],
        out_specs=pl.BlockSpec((tm, d), lambda i: (i, 0)),
        out_shape=jax.ShapeDtypeStruct((n, d), F32),
        compiler_params=_cparams(("parallel",)),
        name="ln_in",
    )(x2, g.reshape(1, d), b.reshape(1, d))


def _proj_kernel(x_ref, w_ref, *o_refs):
    xb = x_ref[...].astype(BF16)
    off = 0
    for o_ref in o_refs:
        wdt = o_ref.shape[-1]
        o_ref[...] = jnp.dot(xb, w_ref[:, off:off + wdt], preferred_element_type=F32)
        off += wdt


def _project(x2, w_all, layer, widths, tm):
    n, d = x2.shape
    n_cols = w_all.shape[-1]
    return pl.pallas_call(
        _proj_kernel,
        grid=(n // tm,),
        in_specs=[pl.BlockSpec((tm, d), lambda i: (i, 0)),
                  pl.BlockSpec((None, d, n_cols), lambda i: (layer, 0, 0))],
        out_specs=[pl.BlockSpec((tm, w), lambda i: (i, 0)) for w in widths],
        out_shape=[jax.ShapeDtypeStruct((n, w), F32) for w in widths],
        compiler_params=_cparams(("parallel",)),
        name="proj_in",
    )(x2, w_all)


def _rwkv_kernel(ash_ref, az_ref, prev0_ref, s0_ref, mu_ref, w0_ref, a0_ref, w2a2_ref, kk_ref, ka_ref,
                 rk_ref, gng_ref, gnb_ref, *rest, gb, n_chains, c, n_heads, passes):
    o_ref, st_ref, s_ref, prev_scr = rest[-4:]

    @pl.when(pl.program_id(1) == 0)
    def _():
        for b in range(s_ref.shape[0]):
            s_ref[b] = _block_diag(s0_ref[b])
        prev_scr[...] = prev0_ref[...]

    prm = dict(mu=mu_ref[...], w0=w0_ref[...], a0=a0_ref[...], w2a2=w2a2_ref[...], k_k=kk_ref[...],
               k_a=ka_ref[...], r_k=rk_ref[...], gn_g=gng_ref[...], gn_b=gnb_ref[...])
    ins = []
    for ch in range(n_chains):
        b0 = ch * gb
        ins.append((ash_ref[b0:b0 + gb].reshape(gb * c, ash_ref.shape[-1]),
                    az_ref[b0:b0 + gb].reshape(gb * c, az_ref.shape[-1]),
                    [prev_scr[b0 + b] for b in range(gb)],
                    [s_ref[b0 + b] for b in range(gb)]))
    outs = _run_interleaved([_rwkv_chain(*args, prm, gb=gb, c=c, n_heads=n_heads, passes=passes) for args in ins])
    for ch, (o, s_new, last_rows) in enumerate(outs):
        b0 = ch * gb
        o_ref[b0:b0 + gb] = o.reshape(gb, c, o.shape[-1])
        for b in range(gb):
            s_ref[b0 + b] = s_new[b]
            prev_scr[b0 + b] = last_rows[b]

    @pl.when(pl.program_id(1) == pl.num_programs(1) - 1)
    def _():
        for b in range(s_ref.shape[0]):
            st_ref[b] = _diag_tall(s_ref[b])


def _rwkv_chain(p, z, prev_rows, s_old, prm, *, gb, c, n_heads, passes):
    d = n_heads * HEAD_DIM
    prev = _shift_rows(p, 1, prev_rows, gb, c)
    last_rows = [p[(b + 1) * c - 1:(b + 1) * c, :] for b in range(gb)]
    xs = p + (prev - p) * prm["mu"]
    r = xs[:, 0:d]
    k = xs[:, d:2 * d]
    v = xs[:, 2 * d:3 * d]
    wa = xs[:, 3 * d:3 * d + LANES]
    lane = _iota((1, LANES), 1)
    wa = jnp.where(lane < HEAD_DIM, jnp.tanh(wa), wa)
    pre = _mm(wa, prm["w2a2"])
    yield
    w_ll = -_softplus(-(prm["w0"] + pre[:, 0:d])) - 0.5
    lw = -jnp.exp(w_ll)
    a = _sigmoid(prm["a0"] + pre[:, d:2 * d])

    ones_bd = _head_block_ones(d)
    kk = k * prm["k_k"]
    kk = kk / jnp.maximum(jnp.sqrt(_mm_exact_rhs(kk * kk, ones_bd)), 1e-12)
    kmod = k * (1.0 + (a - 1.0) * prm["k_a"])
    kka = kk * a

    cum = _cumsum_rows(lw, gb, c)
    yield
    cum_last = jnp.concatenate(
        [jnp.broadcast_to(cum[(b + 1) * c - 1:(b + 1) * c, :], (c, d)) for b in range(gb)], axis=0)
    inv_p = jnp.exp(-cum)
    to_end = jnp.exp(cum_last - cum)
    b_t = kk * jnp.exp(cum - lw)
    a_t = -kka * inv_p
    k_t = kmod * inv_p
    r_t = r * jnp.exp(cum)

    st = functools.partial(_stack_heads, gb=gb, c=c, n_heads=n_heads)
    n_rows = gb * c
    n = n_rows * n_heads
    g = _mm(jnp.concatenate([b_t, r_t], axis=0), jnp.concatenate([st(a_t), st(k_t)], axis=0), _NT, passes)
    yield
    strict, incl, _ = _ls_masks(gb, c, n_heads)
    m_ab = jnp.where(strict, g[0:n_rows, 0:n], 0.0)
    m_bk = jnp.where(strict, g[0:n_rows, n:2 * n], 0.0)
    m_ra = jnp.where(incl, g[n_rows:2 * n_rows, 0:n], 0.0)
    m_rk = jnp.where(incl, g[n_rows:2 * n_rows, n:2 * n], 0.0)

    rows = [slice(b * c, (b + 1) * c) for b in range(gb)]
    br = [_mm(jnp.concatenate([b_t[sl], r_t[sl]], axis=0), s_old[b], _NT, passes) for b, sl in enumerate(rows)]
    bh = jnp.concatenate([x[0:c] for x in br], axis=0)
    rh = jnp.concatenate([x[c:2 * c] for x in br], axis=0)
    mv = _mm(jnp.concatenate([m_bk, m_rk], axis=0), st(v), passes=passes)
    rhs_u = bh + mv[0:n_rows]
    o = rh + mv[n_rows:2 * n_rows]
    t_inv = yield from _unit_lower_inverse(m_ab, gb, c, n_heads, passes)
    u = _mm(t_inv, st(rhs_u), passes=passes)
    yield
    o = o + _mm(m_ra, st(u), passes=passes)

    a_end = -kka * to_end
    k_end = kmod * to_end
    same_head = _div_pow2(_iota((d, d), 0), HEAD_DIM) == _div_pow2(_iota((d, d), 1), HEAD_DIM)
    s_new = []
    for b, sl in enumerate(rows):
        p_end = jnp.exp(cum[(b + 1) * c - 1:(b + 1) * c, :])
        upd = _mm(jnp.concatenate([u[sl], v[sl]], axis=0), jnp.concatenate([a_end[sl], k_end[sl]], axis=0),
                  _TN, passes)
        s_new.append(s_old[b] * p_end + jnp.where(same_head, upd, 0.0))
    yield

    mean = _mm_exact_rhs(o, ones_bd) * (1.0 / HEAD_DIM)
    oc = o - mean
    yield
    var = _mm_exact_rhs(oc * oc, ones_bd) * (1.0 / HEAD_DIM)
    o = oc * lax.rsqrt(var + RWKV_GN_EPS) * prm["gn_g"] + prm["gn_b"]
    bonus = _mm_exact_rhs(r * kmod * prm["r_k"], ones_bd) * v
    return ((o + bonus) * _silu(z)).astype(BF16), s_new, last_rows


def _state_specs(s0, s0_layer, stacked, layer, depth, bsz, gb, d):
    in_spec = pl.BlockSpec((None, gb, d, HEAD_DIM), lambda b, s: (s0_layer, b, 0, 0))
    out_spec = pl.BlockSpec((None, gb, d, HEAD_DIM), lambda b, s: (layer, b, 0, 0))
    out_shape = jax.ShapeDtypeStruct((depth, bsz, d, HEAD_DIM), F32)
    extra_specs = [] if stacked is None else [pl.BlockSpec(memory_space=pl.ANY)]
    extra_args = [] if stacked is None else [stacked]
    return in_spec, out_spec, out_shape, extra_specs, extra_args


def _rwkv_mix(a_sh, a_z, prev0, s0, s0_layer, stacked, layer, depth, lp, gb, n_chains, c, passes):
    bsz, t_len, n_shift = a_sh.shape
    d = a_z.shape[-1]
    n_heads = d // HEAD_DIM
    kern = functools.partial(_rwkv_kernel, gb=gb, n_chains=n_chains, c=c, n_heads=n_heads, passes=passes)
    gb = gb * n_chains
    vec = lambda w: pl.BlockSpec((1, w), lambda b, s: (0, 0))
    s_in, s_out, s_shape, extra_specs, extra_args = _state_specs(s0, s0_layer, stacked, layer, depth, bsz, gb, d)
    args = [a_sh, a_z, prev0, s0, lp["mu"], lp["w0"], lp["a0"], lp["w2a2"], lp["k_k"], lp["k_a"], lp["r_k"],
            lp["gn_g"], lp["gn_b"]]
    return pl.pallas_call(
        kern,
        grid=(bsz // gb, t_len // c),
        in_specs=[pl.BlockSpec((gb, c, n_shift), lambda b, s: (b, s, 0)),
                  pl.BlockSpec((gb, c, d), lambda b, s: (b, s, 0)),
                  pl.BlockSpec((gb, 1, n_shift), lambda b, s: (b, 0, 0)),
                  s_in,
                  vec(n_shift), vec(d), vec(d),
                  pl.BlockSpec((LANES, 2 * d), lambda b, s: (0, 0)),
                  vec(d), vec(d), vec(d), vec(d), vec(d)] + extra_specs,
        out_specs=[pl.BlockSpec((gb, c, d), lambda b, s: (b, s, 0)), s_out],
        out_shape=[jax.ShapeDtypeStruct((bsz, t_len, d), BF16), s_shape],
        input_output_aliases={len(args): 1} if extra_args else {},
        scratch_shapes=[pltpu.VMEM((gb, d, d), F32), pltpu.VMEM((gb, 1, n_shift), F32)],
        compiler_params=_cparams(("parallel", "arbitrary")),
        name="rwkv_mix",
    )(*args, *extra_args)


def _gdn_kernel(qkv_ref, small_ref, cz_ref, conv0_ref, s0_ref, convw_ref, alog_ref, dtb_ref, ng_ref,
                eb_ref, ea_ref, *rest, gb, n_chains, c, n_heads, passes):
    o_ref, st_ref, s_ref, conv_scr = rest[-4:]

    @pl.when(pl.program_id(1) == 0)
    def _():
        for b in range(s_ref.shape[0]):
            s_ref[b] = _block_diag(s0_ref[b])
        conv_scr[...] = conv0_ref[...]

    prm = dict(conv_w=convw_ref[...], a_log=alog_ref[...], dt_bias=dtb_ref[...], norm_g=ng_ref[...],
               e_beta=eb_ref[...], e_a=ea_ref[...])
    ins = []
    for ch in range(n_chains):
        b0 = ch * gb
        ins.append((qkv_ref[b0:b0 + gb].reshape(gb * c, qkv_ref.shape[-1]),
                    small_ref[b0:b0 + gb].reshape(gb * c, LANES),
                    cz_ref[b0:b0 + gb].reshape(gb * c, cz_ref.shape[-1]),
                    [conv_scr[b0 + b] for b in range(gb)],
                    [s_ref[b0 + b] for b in range(gb)]))
    outs = _run_interleaved([_gdn_chain(*args, prm, gb=gb, c=c, n_heads=n_heads, passes=passes) for args in ins])
    for ch, (o, s_new, last_rows) in enumerate(outs):
        b0 = ch * gb
        o_ref[b0:b0 + gb] = o.reshape(gb, c, o.shape[-1])
        for b in range(gb):
            s_ref[b0 + b] = s_new[b]
            conv_scr[b0 + b] = last_rows[b]

    @pl.when(pl.program_id(1) == pl.num_programs(1) - 1)
    def _():
        for b in range(s_ref.shape[0]):
            st_ref[b] = _diag_tall(s_ref[b])


def _gdn_chain(x, small, z, prev_rows, s_old, prm, *, gb, c, n_heads, passes):
    d = n_heads * HEAD_DIM
    conv_w = prm["conv_w"]
    n_conv = conv_w.shape[0]
    y = x * conv_w[n_conv - 1:n_conv, :]
    for j in range(1, n_conv):
        y = y + _shift_rows(x, j, prev_rows, gb, c) * conv_w[n_conv - 1 - j:n_conv - j, :]
    last_rows = [x[(b + 1) * c - (n_conv - 1):(b + 1) * c, :] for b in range(gb)]
    y = _silu(y)
    q = y[:, 0:d]
    k = y[:, d:2 * d]
    v = y[:, 2 * d:3 * d]
    ones_bd = _head_block_ones(d)
    q = q * lax.rsqrt(_mm_exact_rhs(q * q, ones_bd) + L2_EPS) * (HEAD_DIM ** -0.5)
    k = k * lax.rsqrt(_mm_exact_rhs(k * k, ones_bd) + L2_EPS)

    beta = _sigmoid(_mm_exact_rhs(small, prm["e_beta"]))
    g = -jnp.exp(prm["a_log"]) * _softplus(_mm_exact_rhs(small, prm["e_a"]) + prm["dt_bias"])
    yield
    gc = _cumsum_rows(g, gb, c)
    yield
    gc_last = jnp.concatenate(
        [jnp.broadcast_to(gc[(b + 1) * c - 1:(b + 1) * c, :], (c, d)) for b in range(gb)], axis=0)
    egc = jnp.exp(gc)

    st = functools.partial(_stack_heads, gb=gb, c=c, n_heads=n_heads)
    n_rows = gb * c
    n = n_rows * n_heads
    kb = k * beta
    strict, incl, _ = _ls_masks(gb, c, n_heads)
    gcol = jnp.min(st(gc), axis=1, keepdims=True)
    eye = _iota((n, n), 0) == _iota((n, n), 1)
    grow = jnp.sum(jnp.where(eye, jnp.broadcast_to(gcol, (n, n)), 0.0), axis=0, keepdims=True)
    if gb == 1 and c == HEAD_DIM:
        gc_ls = gc
    else:
        lane_h = _div_pow2(_iota((d, n), 1), c) & (n_heads - 1)
        pick = jnp.where(_iota((d, n), 0) == lane_h * HEAD_DIM, 1.0, 0.0).astype(BF16)
        gc_ls = _mm_exact_rhs(gc, pick)
    dmat = jnp.exp(jnp.where(incl, gc_ls - grow, 0.0))
    g2 = _mm(jnp.concatenate([kb, q], axis=0), st(k), _NT, passes)
    yield
    m = jnp.where(strict, g2[0:n_rows] * dmat, 0.0)
    qk = jnp.where(incl, g2[n_rows:2 * n_rows] * dmat, 0.0)
    t_inv = yield from _unit_lower_inverse(-m, gb, c, n_heads, passes)
    rhs = jnp.concatenate([st(v * beta), st(kb * egc)], axis=1)
    sol = _mm(t_inv, rhs, passes=passes)
    yield
    u = sol[:, 0:d]
    w = sol[:, d:2 * d]

    rows = [slice(b * c, (b + 1) * c) for b in range(gb)]
    qg = q * egc
    kd = k * jnp.exp(gc_last - gc)
    v_new = u - jnp.concatenate([_mm(w[sl], s_old[b], passes=passes) for b, sl in enumerate(rows)], axis=0)
    o = jnp.concatenate([_mm(qg[sl], s_old[b], passes=passes) for b, sl in enumerate(rows)], axis=0)
    yield
    o = o + _mm(qk, st(v_new), passes=passes)
    same_head = _div_pow2(_iota((d, d), 0), HEAD_DIM) == _div_pow2(_iota((d, d), 1), HEAD_DIM)
    s_new = []
    for b, sl in enumerate(rows):
        gl = jnp.exp(gc[(b + 1) * c - 1:(b + 1) * c, :])
        s_new.append(s_old[b] * gl + jnp.where(same_head, _mm(kd[sl], v_new[sl], _TN, passes), 0.0))
    yield

    ms = _mm_exact_rhs(o * o, ones_bd) * (1.0 / HEAD_DIM)
    o = o * lax.rsqrt(ms + GDN_NORM_EPS) * prm["norm_g"]
    return (o * _silu(z)).astype(BF16), s_new, last_rows


def _gdn_mix(c_qkv, small, c_z, conv0, s0, s0_layer, stacked, layer, depth, lp, gb, n_chains, c, passes):
    bsz, t_len, n_qkv = c_qkv.shape
    d = c_z.shape[-1]
    n_heads = d // HEAD_DIM
    n_conv = lp["conv_w"].shape[0]
    kern = functools.partial(_gdn_kernel, gb=gb, n_chains=n_chains, c=c, n_heads=n_heads, passes=passes)
    gb = gb * n_chains
    vec = lambda w: pl.BlockSpec((1, w), lambda b, s: (0, 0))
    s_in, s_out, s_shape, extra_specs, extra_args = _state_specs(s0, s0_layer, stacked, layer, depth, bsz, gb, d)
    args = [c_qkv, small, c_z, conv0, s0, lp["conv_w"], lp["a_log"], lp["dt_bias"], lp["norm_g"],
            lp["e_beta"], lp["e_a"]]
    return pl.pallas_call(
        kern,
        grid=(bsz // gb, t_len // c),
        in_specs=[pl.BlockSpec((gb, c, n_qkv), lambda b, s: (b, s, 0)),
                  pl.BlockSpec((gb, c, LANES), lambda b, s: (b, s, 0)),
                  pl.BlockSpec((gb, c, d), lambda b, s: (b, s, 0)),
                  pl.BlockSpec((gb, n_conv - 1, n_qkv), lambda b, s: (b, 0, 0)),
                  s_in,
                  pl.BlockSpec((n_conv, n_qkv), lambda b, s: (0, 0)),
                  vec(d), vec(d), vec(d),
                  pl.BlockSpec((LANES, d), lambda b, s: (0, 0)),
                  pl.BlockSpec((LANES, d), lambda b, s: (0, 0))] + extra_specs,
        out_specs=[pl.BlockSpec((gb, c, d), lambda b, s: (b, s, 0)), s_out],
        out_shape=[jax.ShapeDtypeStruct((bsz, t_len, d), BF16), s_shape],
        input_output_aliases={len(args): 1} if extra_args else {},
        scratch_shapes=[pltpu.VMEM((gb, d, d), F32), pltpu.VMEM((gb, n_conv - 1, n_qkv), F32)],
        compiler_params=_cparams(("parallel", "arbitrary")),
        name="gdn_mix",
    )(*args, *extra_args)


def _log_forget_cumsum(small, bf, carry_scr):
    @pl.when(pl.program_id(1) == 0)
    def _():
        carry_scr[...] = jnp.zeros_like(carry_scr)

    logf = -_softplus(-(small + bf))
    tb = logf.shape[0]
    c = _cumsum_rows(logf, 1, tb) + carry_scr[...]
    carry_scr[...] = c[tb - 1:tb, :]
    return logf, c


def _fox_gate_kernel(small_ref, bf_ref, logf_ref, ccol_ref, crow_ref, carry_scr, *, n_heads):
    logf, c = _log_forget_cumsum(small_ref[...], bf_ref[...], carry_scr)
    logf_ref[...] = logf[:, 0:n_heads]
    ccol_ref[...] = c
    sel = jnp.where(_iota((n_heads, LANES), 0) == _iota((n_heads, LANES), 1), 1.0, 0.0).astype(BF16)
    crow_ref[...] = _mm_exact_lhs(sel, c, _NT)


def _fox_gates(small, bf_pad, n_heads, tb):
    bsz, t_len, _ = small.shape
    kern = functools.partial(_fox_gate_kernel, n_heads=n_heads)
    return pl.pallas_call(
        kern,
        grid=(bsz, t_len // tb),
        in_specs=[pl.BlockSpec((None, tb, LANES), lambda b, s: (b, s, 0)),
                  pl.BlockSpec((1, LANES), lambda b, s: (0, 0))],
        out_specs=[pl.BlockSpec((None, tb, n_heads), lambda b, s: (b, s, 0)),
                   pl.BlockSpec((None, tb, LANES), lambda b, s: (b, s, 0)),
                   pl.BlockSpec((None, n_heads, tb), lambda b, s: (b, 0, s))],
        out_shape=[jax.ShapeDtypeStruct((bsz, t_len, n_heads), F32),
                   jax.ShapeDtypeStruct((bsz, t_len, LANES), F32),
                   jax.ShapeDtypeStruct((bsz, n_heads, t_len), F32)],
        scratch_shapes=[pltpu.VMEM((1, LANES), F32)],
        compiler_params=_cparams(("parallel", "arbitrary")),
        name="fox_gates",
    )(small, bf_pad)


N_AUX = 3


def _proj_prompt_kernel(*refs, widths, n_heads, ln_in, n_alias):
    refs = list(refs)
    x_ref, w_ref, bf_ref, eaux_ref = refs[0:4]
    pos = 4
    if ln_in:
        g_ref, b_ref = refs[pos:pos + 2]
        pos += 2
    pos += n_alias
    (ash_ref, az_ref, bz_ref, cqkv_ref, cz_ref, small_ref, k_ref, v_ref, logf_ref,
     qa_ref, ka_ref, va_ref) = refs[pos:pos + 12]
    pos += 12
    if ln_in:
        xn_ref = refs[pos]
        pos += 1
    carry_scr = refs[pos]

    x = x_ref[...]
    if ln_in:
        mu = jnp.mean(x, axis=-1, keepdims=True)
        xc = x - mu
        var = jnp.mean(xc * xc, axis=-1, keepdims=True)
        x = xc * lax.rsqrt(var + LN_EPS) * g_ref[...] + b_ref[...]
        xn_ref[...] = x
    xb = x.astype(BF16)
    offs = np.concatenate([[0], np.cumsum(widths)])
    seg = lambda i: jnp.dot(xb, w_ref[:, int(offs[i]):int(offs[i + 1])], preferred_element_type=F32)
    ash_ref[...] = seg(0)
    az_ref[...] = seg(1)
    q, k, v = seg(2), seg(3), seg(4)
    bz_ref[...] = seg(5)
    cqkv_ref[...] = seg(6)
    cz_ref[...] = seg(7)
    small = seg(8)
    small_ref[...] = small
    k_ref[...] = k
    v_ref[...] = v

    logf, c = _log_forget_cumsum(small, bf_ref[...], carry_scr)
    logf_ref[...] = logf[:, 0:n_heads]
    parts = jnp.concatenate(_split3(c * LOG2E), axis=1)
    lane = _iota((1, LANES), 1)
    for h in range(n_heads):
        pair, hh = divmod(h, 2)
        own = (lane < HEAD_DIM) if hh == 0 else (lane >= HEAD_DIM)
        a0 = HEAD_DIM if hh == 0 else 0
        aux_q = jnp.where((lane >= a0) & (lane < a0 + N_AUX), 1.0, 0.0)
        aux_k = _dg(parts, eaux_ref[h], _NN)
        aux_v = jnp.where(lane == a0, 1.0, 0.0)
        cols = slice(pair * LANES, (pair + 1) * LANES)
        qa_ref[h] = jnp.where(own, q[:, cols] * (HEAD_DIM ** -0.5 * LOG2E), aux_q).astype(BF16)
        ka_ref[h] = jnp.where(own, k[:, cols], aux_k).astype(BF16)
        va_ref[h] = jnp.where(own, v[:, cols], aux_v).astype(BF16)


def _project_prompt(x3, w_all, layer, widths, bf_pad, e_aux, n_heads, tm, ln_params, stacked):
    bsz, t_len, d = x3.shape
    depth, _, n_cols = w_all.shape
    d_fox = n_heads * HEAD_DIM
    ln_in = ln_params is not None
    n_alias = 0 if stacked is None else len(stacked)
    kern = functools.partial(_proj_prompt_kernel, widths=tuple(widths), n_heads=n_heads, ln_in=ln_in,
                             n_alias=n_alias)
    row = lambda w: pl.BlockSpec((None, tm, w), lambda b, s: (b, s, 0))
    per_layer = lambda w: pl.BlockSpec((None, None, tm, w), lambda b, s: (layer, b, s, 0))
    per_head = pl.BlockSpec((None, n_heads, tm, LANES), lambda b, s: (b, 0, s, 0))
    const = lambda shape: pl.BlockSpec(shape, lambda b, s: (0,) * len(shape))
    in_specs = [row(d), pl.BlockSpec((None, d, n_cols), lambda b, s: (layer, 0, 0), pipeline_mode=pl.Buffered(1)),
                const((1, LANES)), const((n_heads, N_AUX * LANES, LANES))]
    args = [x3, w_all, bf_pad, e_aux]
    if ln_in:
        in_specs += [const((1, d)), const((1, d))]
        args += [p.reshape(1, d) for p in ln_params]
    aliases = {}
    if stacked is not None:
        for j, buf in enumerate(stacked):
            aliases[len(args)] = 6 + j
            in_specs.append(pl.BlockSpec(memory_space=pl.ANY))
            args.append(buf)
    f32 = lambda *shape: jax.ShapeDtypeStruct(shape, F32)
    out_specs = [row(widths[0]), row(widths[1]), row(widths[5]), row(widths[6]), row(widths[7]), row(LANES),
                 per_layer(d_fox), per_layer(d_fox), per_layer(n_heads), per_head, per_head, per_head]
    out_shape = [f32(bsz, t_len, widths[0]), f32(bsz, t_len, widths[1]), f32(bsz, t_len, widths[5]),
                 f32(bsz, t_len, widths[6]), f32(bsz, t_len, widths[7]), f32(bsz, t_len, LANES),
                 f32(depth, bsz, t_len, d_fox), f32(depth, bsz, t_len, d_fox), f32(depth, bsz, t_len, n_heads)]
    out_shape += [jax.ShapeDtypeStruct((bsz, n_heads, t_len, LANES), BF16)] * 3
    if ln_in:
        out_specs.append(row(d))
        out_shape.append(f32(bsz, t_len, d))
    return pl.pallas_call(
        kern,
        grid=(bsz, t_len // tm),
        in_specs=in_specs,
        out_specs=out_specs,
        out_shape=out_shape,
        input_output_aliases=aliases,
        scratch_shapes=[pltpu.VMEM((1, LANES), F32)],
        compiler_params=_cparams(("parallel", "arbitrary")),
        name="proj_prompt",
    )(*args)


def _fox_prompt_kernel(qa_ref, ka_ref, va_ref, bz_ref, o_ref, s_scr, m_scr, acc_scr, *, blk, n_h):
    qi = pl.program_id(2)
    m_scr[...] = jnp.full_like(m_scr, NEG_BIG)
    acc_scr[...] = jnp.zeros_like(acc_scr)

    def logits(ki, slot):
        k0 = pl.multiple_of(ki * blk, blk)
        for h in range(n_h):
            s_scr[slot, h] = _dg(qa_ref[h], ka_ref[h, pl.ds(k0, blk), :], _NT)

    def consume(ki, slot, masked):
        k0 = pl.multiple_of(ki * blk, blk)
        for h in range(n_h):
            s = s_scr[slot, h]
            if masked:
                s = jnp.where(_iota((blk, blk), 1) <= _iota((blk, blk), 0), s, NEG_BIG)
            m_prev = m_scr[h]
            m_new = jnp.maximum(m_prev, jnp.max(s, axis=1, keepdims=True))
            alpha = jnp.exp2(m_prev - m_new)
            p = jnp.exp2(s - jnp.tile(m_new, (1, blk // LANES)))
            acc_scr[h] = alpha * acc_scr[h] + _dg(p.astype(BF16), va_ref[h, pl.ds(k0, blk), :], _NN)
            m_scr[h] = m_new

    def two_blocks(j, carry):
        logits(2 * j + 1, 0)
        consume(2 * j, 1, False)
        logits(jnp.minimum(2 * j + 2, qi - 1), 1)
        consume(2 * j + 1, 0, False)
        return carry

    logits(qi, 0)

    @pl.when(qi == 0)
    def _():
        consume(qi, 0, True)

    @pl.when(qi > 0)
    def _():
        logits(0, 1)
        consume(qi, 0, True)
        lax.fori_loop(0, qi // 2, two_blocks, 0)

        @pl.when(qi % 2 == 1)
        def _():
            consume(qi - 1, 1, False)

    lane = _iota((1, LANES), 1)
    outs = []
    for pair in range(n_h // 2):
        acc0 = acc_scr[2 * pair]
        acc1 = acc_scr[2 * pair + 1]
        outs.append(jnp.where(lane < HEAD_DIM, acc0 / acc0[:, HEAD_DIM:HEAD_DIM + 1], acc1 / acc1[:, 0:1]))
    o = jnp.concatenate(outs, axis=1) if len(outs) > 1 else outs[0]
    o_ref[...] = (o * _silu(bz_ref[...])).astype(BF16)


def _fox_prompt(qa, ka, va, b_z, blk):
    bsz, n_heads, t_len, _ = qa.shape
    d = b_z.shape[-1]
    n_h = min(n_heads, FOX_HEADS)
    w = (n_h // 2) * LANES
    return pl.pallas_call(
        functools.partial(_fox_prompt_kernel, blk=blk, n_h=n_h),
        grid=(bsz, n_heads // n_h, t_len // blk),
        in_specs=[pl.BlockSpec((None, n_h, blk, LANES), lambda b, p, i: (b, p, i, 0)),
                  pl.BlockSpec((None, n_h, t_len, LANES), lambda b, p, i: (b, p, 0, 0)),
                  pl.BlockSpec((None, n_h, t_len, LANES), lambda b, p, i: (b, p, 0, 0)),
                  pl.BlockSpec((None, blk, w), lambda b, p, i: (b, i, p))],
        out_specs=pl.BlockSpec((None, blk, w), lambda b, p, i: (b, i, p)),
        out_shape=jax.ShapeDtypeStruct((bsz, t_len, d), BF16),
        scratch_shapes=[pltpu.VMEM((2, n_h, blk, blk), F32),
                        pltpu.VMEM((n_h, blk, LANES), F32),
                        pltpu.VMEM((n_h, blk, LANES), F32)],
        compiler_params=_cparams(("parallel", "parallel", "arbitrary")),
        name="fox_prompt",
    )(qa, ka, va, b_z)


def _fox_cached_kernel(q_ref, k_ref, v_ref, ck_ref, cv_ref, clf_ref, ccol_ref, crow_ref, bz_ref, o_ref,
                       *, n_heads, lane_blk):
    t_len, d = q_ref.shape
    past = ck_ref.shape[1]
    n_rows = n_heads * t_len

    triu = jnp.where(_iota((lane_blk, lane_blk), 0) <= _iota((lane_blk, lane_blk), 1), 1.0, 0.0).astype(BF16)
    carry = jnp.zeros((n_heads, 1), F32)
    pieces = []
    for j in range(past // lane_blk):
        cj = _mm_exact_rhs(clf_ref[:, j * lane_blk:(j + 1) * lane_blk], triu) + carry
        carry = cj[:, lane_blk - 1:lane_blk]
        pieces.append(cj)
    c_cache = jnp.concatenate(pieces, axis=1)
    total = carry

    rows = lambda f: jnp.concatenate([f(h) for h in range(n_heads)], axis=0)
    ck_cache = rows(lambda h: jnp.broadcast_to(c_cache[h:h + 1, :], (t_len, past)))
    ck_new = rows(lambda h: jnp.broadcast_to(total[h:h + 1, :] + crow_ref[h:h + 1, :], (t_len, t_len)))
    cq = rows(lambda h: total[h:h + 1, :] + ccol_ref[:, h:h + 1])

    lane_head = _div_pow2(_iota((1, d), 1), HEAD_DIM)
    q = q_ref[...] * (HEAD_DIM ** -0.5)
    q_bd = rows(lambda h: jnp.where(lane_head == h, q, 0.0)).astype(BF16)
    s_c = _dg(q_bd, ck_ref[...].astype(BF16), _NN) + cq - ck_cache
    s_n = _dg(q_bd, k_ref[...].astype(BF16), _NT) + cq - ck_new
    q_pos = _iota((n_rows, t_len), 0) & (t_len - 1)
    s_n = jnp.where(_iota((n_rows, t_len), 1) <= q_pos, s_n, NEG_BIG)
    m = jnp.maximum(jnp.max(s_c, axis=1, keepdims=True), jnp.max(s_n, axis=1, keepdims=True))
    p_c = jnp.exp(s_c - m)
    p_n = jnp.exp(s_n - m)
    l = jnp.sum(p_c, axis=1, keepdims=True) + jnp.sum(p_n, axis=1, keepdims=True)
    o_all = (_dg(p_c.astype(BF16), cv_ref[...].astype(BF16), _NT)
             + _dg(p_n.astype(BF16), v_ref[...].astype(BF16), _NN)) / l
    o = jnp.where(lane_head == 0, o_all[0:t_len], 0.0)
    for h in range(1, n_heads):
        o = o + jnp.where(lane_head == h, o_all[h * t_len:(h + 1) * t_len], 0.0)
    o_ref[...] = (o * _silu(bz_ref[...])).astype(BF16)


def _fox_cached(q, k, v, cache_k, cache_v, cache_logf_t, ccol, crow, b_z, layer):
    bsz, t_len, d = q.shape
    n_heads = d // HEAD_DIM
    past = cache_k.shape[3]
    kern = functools.partial(_fox_cached_kernel, n_heads=n_heads, lane_blk=min(past, 256))
    cur = lambda w: pl.BlockSpec((None, t_len, w), lambda b: (b, 0, 0))
    return pl.pallas_call(
        kern,
        grid=(bsz,),
        in_specs=[cur(d), cur(d), cur(d),
                  pl.BlockSpec((None, None, d, past), lambda b: (layer, b, 0, 0)),
                  pl.BlockSpec((None, None, d, past), lambda b: (layer, b, 0, 0)),
                  pl.BlockSpec((None, None, n_heads, past), lambda b: (layer, b, 0, 0)),
                  cur(LANES),
                  pl.BlockSpec((None, n_heads, t_len), lambda b: (b, 0, 0)),
                  cur(d)],
        out_specs=cur(d),
        out_shape=jax.ShapeDtypeStruct((bsz, t_len, d), BF16),
        compiler_params=_cparams(("parallel",)),
        name="fox_cached",
    )(q, k, v, cache_k, cache_v, cache_logf_t, ccol, crow, b_z)


def _out_kernel(oa_ref, ob_ref, oc_ref, x_ref, w_ref, g_ref, b_ref, y_ref, *, alpha):
    da = oa_ref.shape[-1]
    db = ob_ref.shape[-1]
    h = jnp.dot(oa_ref[...], w_ref[0:da, :], preferred_element_type=F32)
    h = h + jnp.dot(ob_ref[...], w_ref[da:da + db, :], preferred_element_type=F32)
    h = h + jnp.dot(oc_ref[...], w_ref[da + db:, :], preferred_element_type=F32)
    y = alpha * x_ref[...] + h
    mu = jnp.mean(y, axis=-1, keepdims=True)
    yc = y - mu
    var = jnp.mean(yc * yc, axis=-1, keepdims=True)
    y_ref[...] = yc * lax.rsqrt(var + LN_EPS) * g_ref[...] + b_ref[...]


def _out_project(o_a, o_b, o_c, x2, w_all, layer, g, b, alpha, tm):
    n, d = x2.shape
    d_mix = w_all.shape[1]
    row = lambda w: pl.BlockSpec((tm, w), lambda i: (i, 0))
    return pl.pallas_call(
        functools.partial(_out_kernel, alpha=alpha),
        grid=(n // tm,),
        in_specs=[row(o_a.shape[-1]), row(o_b.shape[-1]), row(o_c.shape[-1]), row(d),
                  pl.BlockSpec((None, d_mix, d), lambda i: (layer, 0, 0)),
                  pl.BlockSpec((1, d), lambda i: (0, 0)),
                  pl.BlockSpec((1, d), lambda i: (0, 0))],
        out_specs=row(d),
        out_shape=jax.ShapeDtypeStruct((n, d), F32),
        compiler_params=_cparams(("parallel",)),
        name="proj_out",
    )(o_a, o_b, o_c, x2, w_all, g.reshape(1, d), b.reshape(1, d))


def _expand_heads(p):
    return jnp.repeat(p.astype(F32), HEAD_DIM).reshape(1, -1)


def _select_matrix(rows, n_cols, first_row):
    e = np.zeros((LANES, n_cols), np.float32)
    for h in range(rows):
        e[first_row + h, h * HEAD_DIM:(h + 1) * HEAD_DIM] = 1.0
    return jnp.asarray(e, BF16)


def _row_tile(n, largest=256):
    for tm in (1024, 512, 256, 128, 64, 32, 16, 8):
        if tm <= largest and n % tm == 0:
            return tm
    raise ValueError(f"row count {n} must be a multiple of 8")


def kernel(x_prompt, x_sample, cache_fox_k, cache_fox_v, cache_fox_logf, state_rwkv_shift, state_rwkv_wkv,
           state_gdn_conv, state_gdn_wkv, ln_in_g, ln_in_b, w_in, rwkv_mu, rwkv_w0, rwkv_w2, rwkv_a0, rwkv_a2,
           rwkv_k_k, rwkv_k_a, rwkv_r_k, rwkv_gn_g, rwkv_gn_b, fox_b_f, gdn_conv_w, gdn_a_log, gdn_dt_bias,
           gdn_norm_g, w_out, ln_post_g, ln_post_b):
    depth, d_model, _ = w_in.shape
    bp, seq, _ = x_prompt.shape
    bs, dec_seq, _ = x_sample.shape
    h_fox = fox_b_f.shape[1]
    h_gdn = gdn_a_log.shape[1]
    h_rwkv = rwkv_r_k.shape[1]
    d_rwkv, d_fox, d_gdn = h_rwkv * HEAD_DIM, h_fox * HEAD_DIM, h_gdn * HEAD_DIM
    rank_w = rwkv_w2.shape[1]
    rank_a = rwkv_a2.shape[1]
    n_shift = 3 * d_rwkv + rank_w + rank_a
    n_conv_cols = 3 * d_gdn
    n_conv = gdn_conv_w.shape[1]
    alpha = (2 * depth) ** 0.25
    assert rank_w + rank_a == LANES and h_fox + 2 * h_gdn <= LANES

    split = [n_shift, d_rwkv, d_fox, d_fox, d_fox, h_fox, d_fox, n_conv_cols, h_gdn, h_gdn, d_gdn]
    offs = np.concatenate([[0], np.cumsum(split)])
    seg = lambda i: w_in[:, :, offs[i]:offs[i + 1]]
    small_w = jnp.concatenate([seg(5), seg(8), seg(9)], axis=-1)
    small_w = jnp.pad(small_w, ((0, 0), (0, 0), (0, LANES - small_w.shape[-1])))
    w_in_p = jnp.concatenate([seg(0), seg(1), seg(2), seg(3), seg(4), seg(6), seg(7), seg(10), small_w],
                             axis=-1).astype(BF16)
    widths = [n_shift, d_rwkv, d_fox, d_fox, d_fox, d_fox, n_conv_cols, d_gdn, LANES]
    w_out_b = w_out.astype(BF16)

    zw = jnp.zeros((depth, rank_w, d_rwkv), F32)
    w2a2 = jnp.concatenate([jnp.concatenate([rwkv_w2, zw], axis=-1),
                            jnp.concatenate([zw, rwkv_a2], axis=-1)], axis=1).astype(BF16)
    bf_pad = jnp.pad(fox_b_f, ((0, 0), (0, LANES - h_fox)))
    e_beta = _select_matrix(h_gdn, d_gdn, h_fox)
    e_a = _select_matrix(h_gdn, d_gdn, h_fox + h_gdn)
    e_aux_np = np.zeros((h_fox, N_AUX * LANES, LANES), np.float32)
    for h in range(h_fox):
        for j in range(N_AUX):
            e_aux_np[h, j * LANES + h, (HEAD_DIM if h % 2 == 0 else 0) + j] = -1.0
    e_aux = jnp.asarray(e_aux_np, BF16)
    cache_logf_t = jnp.swapaxes(cache_fox_logf, 2, 3)
    cache_k_t = jnp.swapaxes(cache_fox_k.reshape(depth, bs, -1, d_fox), 2, 3)
    cache_v_t = jnp.swapaxes(cache_fox_v.reshape(depth, bs, -1, d_fox), 2, 3)

    def layer_params(l):
        row = lambda a: a[l].reshape(1, -1).astype(F32)
        return dict(
            rwkv=dict(mu=row(rwkv_mu), w0=row(rwkv_w0), a0=row(rwkv_a0), w2a2=w2a2[l], k_k=row(rwkv_k_k),
                      k_a=row(rwkv_k_a), r_k=row(rwkv_r_k), gn_g=row(rwkv_gn_g), gn_b=row(rwkv_gn_b)),
            gdn=dict(conv_w=gdn_conv_w[l], a_log=_expand_heads(gdn_a_log[l]),
                     dt_bias=_expand_heads(gdn_dt_bias[l]),
                     norm_g=jnp.tile(gdn_norm_g[l].reshape(1, -1), (1, h_gdn)), e_beta=e_beta, e_a=e_a),
        )

    def mix_and_project(l, x2, bsz, t_len, a_sh, a_z, c_qkv, small, c_z, o_b, prev0, conv0, s_rwkv0, s_gdn0,
                        s0_layer, states):
        lp = layer_params(l)
        c = min(HEAD_DIM, t_len)
        st_rwkv, st_gdn = (None, None) if states is None else states
        gb = N_STACK // (h_rwkv * c)
        n_chains = max(1, min(N_CHAINS, bsz // gb))
        o_a, st_rwkv = _rwkv_mix(a_sh, a_z, prev0, s_rwkv0, s0_layer, st_rwkv, l, depth, lp["rwkv"], gb, n_chains,
                                 c, CHUNK_PASSES)
        gb = N_STACK // (h_gdn * c)
        o_c, st_gdn = _gdn_mix(c_qkv, small, c_z, conv0, s_gdn0, s0_layer, st_gdn, l, depth, lp["gdn"], gb,
                               n_chains, c, CHUNK_PASSES)
        f2 = lambda a: a.reshape(bsz * t_len, a.shape[-1])
        x_new = _out_project(f2(o_a), f2(o_b), f2(o_c), x2, w_out_b, l, ln_post_g[l], ln_post_b[l], alpha,
                             _row_tile(bsz * t_len, OUT_ROWS))
        return x_new, (a_sh[:, -1], c_qkv[:, t_len - (n_conv - 1):]), (st_rwkv, st_gdn)

    def prompt_layer(l, x2, stacked, states):
        ln_params = (ln_in_g, ln_in_b) if l == 0 else None
        res = _project_prompt(x2.reshape(bp, seq, d_model), w_in_p, l, widths, bf_pad[l:l + 1], e_aux, h_fox,
                              min(seq, PROJ_ROWS), ln_params, stacked)
        a_sh, a_z, b_z, c_qkv, c_z, small, k_st, v_st, logf_st, qa, ka, va = res[:12]
        if l == 0:
            x2 = res[12].reshape(bp * seq, d_model)
        o_b = _fox_prompt(qa, ka, va, b_z, min(seq, FOX_BLOCK))
        x_new, outs, states = mix_and_project(l, x2, bp, seq, a_sh, a_z, c_qkv, small, c_z, o_b, *zeros_p, 0, states)
        return x_new, outs, (k_st, v_st, logf_st), states

    def sample_layer(l, x2, states):
        tm = _row_tile(bs * dec_seq)
        r3 = lambda a: a.reshape(bs, dec_seq, a.shape[-1])
        a_sh, a_z, b_q, b_k, b_v, b_z, c_qkv, c_z, small = map(r3, _project(x2, w_in_p, l, widths, tm))
        logf, ccol, crow = _fox_gates(small, bf_pad[l:l + 1], h_fox, min(dec_seq, 256))
        o_b = _fox_cached(b_q, b_k, b_v, cache_k_t, cache_v_t, cache_logf_t, ccol, crow, b_z, l)
        x_new, outs, states = mix_and_project(
            l, x2, bs, dec_seq, a_sh, a_z, c_qkv, small, c_z, o_b, state_rwkv_shift[l][:, None, :],
            state_gdn_conv[l], state_rwkv_wkv.reshape(depth, bs, d_rwkv, HEAD_DIM),
            state_gdn_wkv.reshape(depth, bs, d_gdn, HEAD_DIM), l, states)
        heads = lambda a: a.reshape(bs, dec_seq, h_fox, HEAD_DIM)
        return x_new, (heads(b_k), heads(b_v), logf) + outs, states

    xp = x_prompt.reshape(bp * seq, d_model)
    xs = _layer_norm(x_sample.reshape(bs * dec_seq, d_model), ln_in_g, ln_in_b, _row_tile(bs * dec_seq))
    zeros_p = (jnp.zeros((bp, 1, n_shift), F32), jnp.zeros((bp, n_conv - 1, n_conv_cols), F32),
               jnp.zeros((1, bp, d_rwkv, HEAD_DIM), F32), jnp.zeros((1, bp, d_gdn, HEAD_DIM), F32))
    outs_p, outs_s, stacked, states_p, states_s = [], [], None, None, None
    for l in range(depth):
        xp, o, stacked, states_p = prompt_layer(l, xp, stacked, states_p)
        outs_p.append(o)
        xs, o, states_s = sample_layer(l, xs, states_s)
        outs_s.append(o)
    k_st, v_st, logf_st = stacked
    stack = lambda outs, i: jnp.stack([o[i] for o in outs])
    blocks = lambda st, bsz: st.reshape(depth, bsz, st.shape[2] // HEAD_DIM, HEAD_DIM, HEAD_DIM)
    return (xp.reshape(bp, seq, d_model), xs.reshape(bs, dec_seq, d_model),
            k_st.reshape(depth, bp, seq, h_fox, HEAD_DIM), v_st.reshape(depth, bp, seq, h_fox, HEAD_DIM), logf_st,
            stack(outs_p, 0), blocks(states_p[0], bp), stack(outs_p, 1), blocks(states_p[1], bp),
            stack(outs_s, 0), stack(outs_s, 1), stack(outs_s, 2),
            stack(outs_s, 3), blocks(states_s[0], bs), stack(outs_s, 4), blocks(states_s[1], bs))
```

```python
import functools
import math

import jax
import jax.numpy as jnp
import numpy as np
from jax import lax
from jax.experimental import pallas as pl
from jax.experimental.pallas import tpu as pltpu

F32 = jnp.float32
BF16 = jnp.bfloat16

HEAD_DIM = 64
LANES = 128
N_STACK = 256
PROJ_ROWS = 512
OUT_ROWS = 1024
FOX_BLOCK = 512
FOX_HEADS = 4
N_CHAINS = 8
CHUNK_PASSES = 1
LN_EPS = 1e-5
RWKV_GN_EPS = 64e-5
GDN_NORM_EPS = 1e-6
L2_EPS = 1e-6
NEG_BIG = -1e30
LOG2E = math.log2(math.e)
VMEM_LIMIT = 56 * 1024 * 1024


def _sigmoid(x):
    return 1.0 / (1.0 + jnp.exp(-x))


def _softplus(x):
    return jnp.maximum(x, 0.0) + jnp.log(1.0 + jnp.exp(-jnp.abs(x)))


def _silu(x):
    return x * _sigmoid(x)


def _split3(x):
    hi = x.astype(BF16)
    r1 = x - hi.astype(F32)
    mid = r1.astype(BF16)
    lo = (r1 - mid.astype(F32)).astype(BF16)
    return hi, mid, lo


_NN = (((1,), (0,)), ((), ()))
_NT = (((1,), (1,)), ((), ()))
_TN = (((0,), (0,)), ((), ()))


def _dg(a, b, dims):
    return lax.dot_general(a, b, dims, preferred_element_type=F32)


def _mm(a, b, dims=_NN, passes=1):
    if passes == 1:
        return _dg(a.astype(BF16), b.astype(BF16), dims)
    ah, am, _ = _split3(a)
    bh, bm, _ = _split3(b)
    return _dg(ah, bh, dims) + (_dg(ah, bm, dims) + _dg(am, bh, dims))


def _mm_exact_rhs(a, e, dims=_NN):
    m = a.shape[0]
    r = _dg(jnp.concatenate(_split3(a), axis=0), e, dims)
    return r[0:m] + (r[m:2 * m] + r[2 * m:3 * m])


def _mm_exact_lhs(e, b, dims=_NN):
    hi, mid, lo = _split3(b)
    return _dg(e, hi, dims) + (_dg(e, mid, dims) + _dg(e, lo, dims))


def _iota(shape, dim):
    return lax.broadcasted_iota(jnp.int32, shape, dim)


def _div_pow2(x, n):
    assert n & (n - 1) == 0
    return lax.shift_right_logical(x, jnp.int32(int(math.log2(n))))


def _head_block_ones(n):
    r = _div_pow2(_iota((n, n), 0), HEAD_DIM)
    c = _div_pow2(_iota((n, n), 1), HEAD_DIM)
    return jnp.where(r == c, 1.0, 0.0).astype(BF16)


def _block_diag(tall):
    n = tall.shape[0]
    wide = jnp.concatenate([tall] * (n // HEAD_DIM), axis=1)
    same = _div_pow2(_iota((n, n), 0), HEAD_DIM) == _div_pow2(_iota((n, n), 1), HEAD_DIM)
    return jnp.where(same, wide, 0.0)


def _diag_tall(s_bd):
    n = s_bd.shape[0]
    return jnp.concatenate([s_bd[h * HEAD_DIM:(h + 1) * HEAD_DIM, h * HEAD_DIM:(h + 1) * HEAD_DIM]
                            for h in range(n // HEAD_DIM)], axis=0)


def _stack_heads(x, gb, c, n_heads):
    width = x.shape[1]
    lane_head = _div_pow2(_iota((1, width), 1), HEAD_DIM)
    parts = []
    for b in range(gb):
        xb = x[b * c:(b + 1) * c, :]
        for h in range(n_heads):
            parts.append(jnp.where(lane_head == h, xb, 0.0))
    return jnp.concatenate(parts, axis=0)


def _ls_masks(gb, c, n_heads):
    shape = (gb * c, gb * n_heads * c)
    row = _iota(shape, 0)
    lane = _iota(shape, 1)
    same_b = _div_pow2(row, c) == _div_pow2(lane, n_heads * c)
    i = row & (c - 1)
    j = lane & (c - 1)
    return same_b & (j < i), same_b & (j <= i), same_b & (j == i)


def _expand_bd(m_ls, gb, c, n_heads):
    lane_h = _div_pow2(_iota((1, m_ls.shape[1]), 1), c) & (n_heads - 1)
    parts = []
    for b in range(gb):
        mb = m_ls[b * c:(b + 1) * c, :]
        for h in range(n_heads):
            parts.append(jnp.where(lane_h == h, mb, 0.0))
    return jnp.concatenate(parts, axis=0)


def _unit_lower_inverse(p, gb, c, n_heads, passes):
    n_rows = p.shape[0]
    expand = functools.partial(_expand_bd, gb=gb, c=c, n_heads=n_heads)
    t = jnp.where(_ls_masks(gb, c, n_heads)[2], 1.0, 0.0) + p
    n_rounds = int(math.log2(c)) - 1
    pk = _mm(p, expand(p), passes=passes)
    yield
    for i in range(n_rounds):
        if i + 1 < n_rounds:
            both = _mm(jnp.concatenate([t, pk], axis=0), expand(pk), passes=passes)
            t = t + both[0:n_rows]
            pk = both[n_rows:2 * n_rows]
        else:
            t = t + _mm(t, expand(pk), passes=passes)
        yield
    return t


def _run_interleaved(chains):
    results = [None] * len(chains)
    active = list(enumerate(chains))
    while active:
        still = []
        for i, ch in active:
            try:
                next(ch)
                still.append((i, ch))
            except StopIteration as stop:
                results[i] = stop.value
        active = still
    return results


def _chunk_masks(n, c):
    row = _iota((n, n), 0)
    col = _iota((n, n), 1)
    same = _div_pow2(row, c) == _div_pow2(col, c)
    strict = jnp.logical_and(same, row > col)
    incl = jnp.logical_and(same, row >= col)
    return strict, incl


def _cumsum_rows(x, gb, c):
    n = gb * c
    strict, incl = _chunk_masks(n, c)
    del strict
    tri = jnp.where(incl, 1.0, 0.0).astype(BF16)
    return _mm_exact_lhs(tri, x)


def _shift_rows(x, j, prev_rows, gb, c):
    n_prev = prev_rows[0].shape[0]
    out = pltpu.roll(x, j, 0)
    rid = _iota((x.shape[0], 1), 0)
    for b in range(gb):
        for t in range(j):
            src = n_prev + t - j
            out = jnp.where(rid == b * c + t, prev_rows[b][src:src + 1, :], out)
    return out


def _cparams(sem):
    return pltpu.CompilerParams(dimension_semantics=sem, vmem_limit_bytes=VMEM_LIMIT)


def _ln_kernel(x_ref, g_ref, b_ref, o_ref):
    x = x_ref[...]
    mu = jnp.mean(x, axis=-1, keepdims=True)
    xc = x - mu
    var = jnp.mean(xc * xc, axis=-1, keepdims=True)
    o_ref[...] = xc * lax.rsqrt(var + LN_EPS) * g_ref[...] + b_ref[...]


def _layer_norm(x2, g, b, tm):
    n, d = x2.shape
    return pl.pallas_call(
        _ln_kernel,
        grid=(n // tm,),
        in_specs=[pl.BlockSpec((tm, d), lambda i: (i, 0)),
                  pl.BlockSpec((1, d), lambda i: (0, 0)),
                  pl.BlockSpec((1, d), lambda i: (0, 0))],
        out_specs=pl.BlockSpec((tm, d), lambda i: (i, 0)),
        out_shape=jax.ShapeDtypeStruct((n, d), F32),
        compiler_params=_cparams(("parallel",)),
        name="ln_in",
    )(x2, g.reshape(1, d), b.reshape(1, d))


def _proj_kernel(x_ref, w_ref, *o_refs):
    xb = x_ref[...].astype(BF16)
    off = 0
    for o_ref in o_refs:
        wdt = o_ref.shape[-1]
        o_ref[...] = jnp.dot(xb, w_ref[:, off:off + wdt], preferred_element_type=F32)
        off += wdt


def _project(x2, w_all, layer, widths, tm):
    n, d = x2.shape
    n_cols = w_all.shape[-1]
    return pl.pallas_call(
        _proj_kernel,
        grid=(n // tm,),
        in_specs=[pl.BlockSpec((tm, d), lambda i: (i, 0)),
                  pl.BlockSpec((None, d, n_cols), lambda i: (layer, 0, 0))],
        out_specs=[pl.BlockSpec((tm, w), lambda i: (i, 0)) for w in widths],
        out_shape=[jax.ShapeDtypeStruct((n, w), F32) for w in widths],
        compiler_params=_cparams(("parallel",)),
        name="proj_in",
    )(x2, w_all)


def _rwkv_kernel(ash_ref, az_ref, prev0_ref, s0_ref, mu_ref, w0_ref, a0_ref, w2a2_ref, kk_ref, ka_ref,
                 rk_ref, gng_ref, gnb_ref, *rest, gb, n_chains, c, n_heads, passes):
    o_ref, st_ref, s_ref, prev_scr = rest[-4:]

    @pl.when(pl.program_id(1) == 0)
    def _():
        for b in range(s_ref.shape[0]):
            s_ref[b] = _block_diag(s0_ref[b])
        prev_scr[...] = prev0_ref[...]

    prm = dict(mu=mu_ref[...], w0=w0_ref[...], a0=a0_ref[...], w2a2=w2a2_ref[...], k_k=kk_ref[...],
               k_a=ka_ref[...], r_k=rk_ref[...], gn_g=gng_ref[...], gn_b=gnb_ref[...])
    ins = []
    for ch in range(n_chains):
        b0 = ch * gb
        ins.append((ash_ref[b0:b0 + gb].reshape(gb * c, ash_ref.shape[-1]),
                    az_ref[b0:b0 + gb].reshape(gb * c, az_ref.shape[-1]),
                    [prev_scr[b0 + b] for b in range(gb)],
                    [s_ref[b0 + b] for b in range(gb)]))
    outs = _run_interleaved([_rwkv_chain(*args, prm, gb=gb, c=c, n_heads=n_heads, passes=passes) for args in ins])
    for ch, (o, s_new, last_rows) in enumerate(outs):
        b0 = ch * gb
        o_ref[b0:b0 + gb] = o.reshape(gb, c, o.shape[-1])
        for b in range(gb):
            s_ref[b0 + b] = s_new[b]
            prev_scr[b0 + b] = last_rows[b]

    @pl.when(pl.program_id(1) == pl.num_programs(1) - 1)
    def _():
        for b in range(s_ref.shape[0]):
            st_ref[b] = _diag_tall(s_ref[b])


def _rwkv_chain(p, z, prev_rows, s_old, prm, *, gb, c, n_heads, passes):
    d = n_heads * HEAD_DIM
    prev = _shift_rows(p, 1, prev_rows, gb, c)
    last_rows = [p[(b + 1) * c - 1:(b + 1) * c, :] for b in range(gb)]
    xs = p + (prev - p) * prm["mu"]
    r = xs[:, 0:d]
    k = xs[:, d:2 * d]
    v = xs[:, 2 * d:3 * d]
    wa = xs[:, 3 * d:3 * d + LANES]
    lane = _iota((1, LANES), 1)
    wa = jnp.where(lane < HEAD_DIM, jnp.tanh(wa), wa)
    pre = _mm(wa, prm["w2a2"])
    yield
    w_ll = -_softplus(-(prm["w0"] + pre[:, 0:d])) - 0.5
    lw = -jnp.exp(w_ll)
    a = _sigmoid(prm["a0"] + pre[:, d:2 * d])

    ones_bd = _head_block_ones(d)
    kk = k * prm["k_k"]
    kk = kk / jnp.maximum(jnp.sqrt(_mm_exact_rhs(kk * kk, ones_bd)), 1e-12)
    kmod = k * (1.0 + (a - 1.0) * prm["k_a"])
    kka = kk * a

    cum = _cumsum_rows(lw, gb, c)
    yield
    cum_last = jnp.concatenate(
        [jnp.broadcast_to(cum[(b + 1) * c - 1:(b + 1) * c, :], (c, d)) for b in range(gb)], axis=0)
    inv_p = jnp.exp(-cum)
    to_end = jnp.exp(cum_last - cum)
    b_t = kk * jnp.exp(cum - lw)
    a_t = -kka * inv_p
    k_t = kmod * inv_p
    r_t = r * jnp.exp(cum)

    st = functools.partial(_stack_heads, gb=gb, c=c, n_heads=n_heads)
    n_rows = gb * c
    n = n_rows * n_heads
    g = _mm(jnp.concatenate([b_t, r_t], axis=0), jnp.concatenate([st(a_t), st(k_t)], axis=0), _NT, passes)
    yield
    strict, incl, _ = _ls_masks(gb, c, n_heads)
    m_ab = jnp.where(strict, g[0:n_rows, 0:n], 0.0)
    m_bk = jnp.where(strict, g[0:n_rows, n:2 * n], 0.0)
    m_ra = jnp.where(incl, g[n_rows:2 * n_rows, 0:n], 0.0)
    m_rk = jnp.where(incl, g[n_rows:2 * n_rows, n:2 * n], 0.0)

    rows = [slice(b * c, (b + 1) * c) for b in range(gb)]
    br = [_mm(jnp.concatenate([b_t[sl], r_t[sl]], axis=0), s_old[b], _NT, passes) for b, sl in enumerate(rows)]
    bh = jnp.concatenate([x[0:c] for x in br], axis=0)
    rh = jnp.concatenate([x[c:2 * c] for x in br], axis=0)
    mv = _mm(jnp.concatenate([m_bk, m_rk], axis=0), st(v), passes=passes)
    rhs_u = bh + mv[0:n_rows]
    o = rh + mv[n_rows:2 * n_rows]
    t_inv = yield from _unit_lower_inverse(m_ab, gb, c, n_heads, passes)
    u = _mm(t_inv, st(rhs_u), passes=passes)
    yield
    o = o + _mm(m_ra, st(u), passes=passes)

    a_end = -kka * to_end
    k_end = kmod * to_end
    same_head = _div_pow2(_iota((d, d), 0), HEAD_DIM) == _div_pow2(_iota((d, d), 1), HEAD_DIM)
    s_new = []
    for b, sl in enumerate(rows):
        p_end = jnp.exp(cum[(b + 1) * c - 1:(b + 1) * c, :])
        upd = _mm(jnp.concatenate([u[sl], v[sl]], axis=0), jnp.concatenate([a_end[sl], k_end[sl]], axis=0),
                  _TN, passes)
        s_new.append(s_old[b] * p_end + jnp.where(same_head, upd, 0.0))
    yield

    mean = _mm_exact_rhs(o, ones_bd) * (1.0 / HEAD_DIM)
    oc = o - mean
    yield
    var = _mm_exact_rhs(oc * oc, ones_bd) * (1.0 / HEAD_DIM)
    o = oc * lax.rsqrt(var + RWKV_GN_EPS) * prm["gn_g"] + prm["gn_b"]
    bonus = _mm_exact_rhs(r * kmod * prm["r_k"], ones_bd) * v
    return ((o + bonus) * _silu(z)).astype(BF16), s_new, last_rows


def _state_specs(s0, s0_layer, stacked, layer, depth, bsz, gb, d):
    in_spec = pl.BlockSpec((None, gb, d, HEAD_DIM), lambda b, s: (s0_layer, b, 0, 0))
    out_spec = pl.BlockSpec((None, gb, d, HEAD_DIM), lambda b, s: (layer, b, 0, 0))
    out_shape = jax.ShapeDtypeStruct((depth, bsz, d, HEAD_DIM), F32)
    extra_specs = [] if stacked is None else [pl.BlockSpec(memory_space=pl.ANY)]
    extra_args = [] if stacked is None else [stacked]
    return in_spec, out_spec, out_shape, extra_specs, extra_args


def _rwkv_mix(a_sh, a_z, prev0, s0, s0_layer, stacked, layer, depth, lp, gb, n_chains, c, passes):
    bsz, t_len, n_shift = a_sh.shape
    d = a_z.shape[-1]
    n_heads = d // HEAD_DIM
    kern = functools.partial(_rwkv_kernel, gb=gb, n_chains=n_chains, c=c, n_heads=n_heads, passes=passes)
    gb = gb * n_chains
    vec = lambda w: pl.BlockSpec((1, w), lambda b, s: (0, 0))
    s_in, s_out, s_shape, extra_specs, extra_args = _state_specs(s0, s0_layer, stacked, layer, depth, bsz, gb, d)
    args = [a_sh, a_z, prev0, s0, lp["mu"], lp["w0"], lp["a0"], lp["w2a2"], lp["k_k"], lp["k_a"], lp["r_k"],
            lp["gn_g"], lp["gn_b"]]
    return pl.pallas_call(
        kern,
        grid=(bsz // gb, t_len // c),
        in_specs=[pl.BlockSpec((gb, c, n_shift), lambda b, s: (b, s, 0)),
                  pl.BlockSpec((gb, c, d), lambda b, s: (b, s, 0)),
                  pl.BlockSpec((gb, 1, n_shift), lambda b, s: (b, 0, 0)),
                  s_in,
                  vec(n_shift), vec(d), vec(d),
                  pl.BlockSpec((LANES, 2 * d), lambda b, s: (0, 0)),
                  vec(d), vec(d), vec(d), vec(d), vec(d)] + extra_specs,
        out_specs=[pl.BlockSpec((gb, c, d), lambda b, s: (b, s, 0)), s_out],
        out_shape=[jax.ShapeDtypeStruct((bsz, t_len, d), BF16), s_shape],
        input_output_aliases={len(args): 1} if extra_args else {},
        scratch_shapes=[pltpu.VMEM((gb, d, d), F32), pltpu.VMEM((gb, 1, n_shift), F32)],
        compiler_params=_cparams(("parallel", "arbitrary")),
        name="rwkv_mix",
    )(*args, *extra_args)


def _gdn_kernel(qkv_ref, small_ref, cz_ref, conv0_ref, s0_ref, convw_ref, alog_ref, dtb_ref, ng_ref,
                eb_ref, ea_ref, *rest, gb, n_chains, c, n_heads, passes):
    o_ref, st_ref, s_ref, conv_scr = rest[-4:]

    @pl.when(pl.program_id(1) == 0)
    def _():
        for b in range(s_ref.shape[0]):
            s_ref[b] = _block_diag(s0_ref[b])
        conv_scr[...] = conv0_ref[...]

    prm = dict(conv_w=convw_ref[...], a_log=alog_ref[...], dt_bias=dtb_ref[...], norm_g=ng_ref[...],
               e_beta=eb_ref[...], e_a=ea_ref[...])
    ins = []
    for ch in range(n_chains):
        b0 = ch * gb
        ins.append((qkv_ref[b0:b0 + gb].reshape(gb * c, qkv_ref.shape[-1]),
                    small_ref[b0:b0 + gb].reshape(gb * c, LANES),
                    cz_ref[b0:b0 + gb].reshape(gb * c, cz_ref.shape[-1]),
                    [conv_scr[b0 + b] for b in range(gb)],
                    [s_ref[b0 + b] for b in range(gb)]))
    outs = _run_interleaved([_gdn_chain(*args, prm, gb=gb, c=c, n_heads=n_heads, passes=passes) for args in ins])
    for ch, (o, s_new, last_rows) in enumerate(outs):
        b0 = ch * gb
        o_ref[b0:b0 + gb] = o.reshape(gb, c, o.shape[-1])
        for b in range(gb):
            s_ref[b0 + b] = s_new[b]
            conv_scr[b0 + b] = last_rows[b]

    @pl.when(pl.program_id(1) == pl.num_programs(1) - 1)
    def _():
        for b in range(s_ref.shape[0]):
            st_ref[b] = _diag_tall(s_ref[b])


def _gdn_chain(x, small, z, prev_rows, s_old, prm, *, gb, c, n_heads, passes):
    d = n_heads * HEAD_DIM
    conv_w = prm["conv_w"]
    n_conv = conv_w.shape[0]
    y = x * conv_w[n_conv - 1:n_conv, :]
    for j in range(1, n_conv):
        y = y + _shift_rows(x, j, prev_rows, gb, c) * conv_w[n_conv - 1 - j:n_conv - j, :]
    last_rows = [x[(b + 1) * c - (n_conv - 1):(b + 1) * c, :] for b in range(gb)]
    y = _silu(y)
    q = y[:, 0:d]
    k = y[:, d:2 * d]
    v = y[:, 2 * d:3 * d]
    ones_bd = _head_block_ones(d)
    q = q * lax.rsqrt(_mm_exact_rhs(q * q, ones_bd) + L2_EPS) * (HEAD_DIM ** -0.5)
    k = k * lax.rsqrt(_mm_exact_rhs(k * k, ones_bd) + L2_EPS)

    beta = _sigmoid(_mm_exact_rhs(small, prm["e_beta"]))
    g = -jnp.exp(prm["a_log"]) * _softplus(_mm_exact_rhs(small, prm["e_a"]) + prm["dt_bias"])
    yield
    gc = _cumsum_rows(g, gb, c)
    yield
    gc_last = jnp.concatenate(
        [jnp.broadcast_to(gc[(b + 1) * c - 1:(b + 1) * c, :], (c, d)) for b in range(gb)], axis=0)
    egc = jnp.exp(gc)

    st = functools.partial(_stack_heads, gb=gb, c=c, n_heads=n_heads)
    n_rows = gb * c
    n = n_rows * n_heads
    kb = k * beta
    strict, incl, _ = _ls_masks(gb, c, n_heads)
    gcol = jnp.min(st(gc), axis=1, keepdims=True)
    eye = _iota((n, n), 0) == _iota((n, n), 1)
    grow = jnp.sum(jnp.where(eye, jnp.broadcast_to(gcol, (n, n)), 0.0), axis=0, keepdims=True)
    if gb == 1 and c == HEAD_DIM:
        gc_ls = gc
    else:
        lane_h = _div_pow2(_iota((d, n), 1), c) & (n_heads - 1)
        pick = jnp.where(_iota((d, n), 0) == lane_h * HEAD_DIM, 1.0, 0.0).astype(BF16)
        gc_ls = _mm_exact_rhs(gc, pick)
    dmat = jnp.exp(jnp.where(incl, gc_ls - grow, 0.0))
    g2 = _mm(jnp.concatenate([kb, q], axis=0), st(k), _NT, passes)
    yield
    m = jnp.where(strict, g2[0:n_rows] * dmat, 0.0)
    qk = jnp.where(incl, g2[n_rows:2 * n_rows] * dmat, 0.0)
    t_inv = yield from _unit_lower_inverse(-m, gb, c, n_heads, passes)
    rhs = jnp.concatenate([st(v * beta), st(kb * egc)], axis=1)
    sol = _mm(t_inv, rhs, passes=passes)
    yield
    u = sol[:, 0:d]
    w = sol[:, d:2 * d]

    rows = [slice(b * c, (b + 1) * c) for b in range(gb)]
    qg = q * egc
    kd = k * jnp.exp(gc_last - gc)
    v_new = u - jnp.concatenate([_mm(w[sl], s_old[b], passes=passes) for b, sl in enumerate(rows)], axis=0)
    o = jnp.concatenate([_mm(qg[sl], s_old[b], passes=passes) for b, sl in enumerate(rows)], axis=0)
    yield
    o = o + _mm(qk, st(v_new), passes=passes)
    same_head = _div_pow2(_iota((d, d), 0), HEAD_DIM) == _div_pow2(_iota((d, d), 1), HEAD_DIM)
    s_new = []
    for b, sl in enumerate(rows):
        gl = jnp.exp(gc[(b + 1) * c - 1:(b + 1) * c, :])
        s_new.append(s_old[b] * gl + jnp.where(same_head, _mm(kd[sl], v_new[sl], _TN, passes), 0.0))
    yield

    ms = _mm_exact_rhs(o * o, ones_bd) * (1.0 / HEAD_DIM)
    o = o * lax.rsqrt(ms + GDN_NORM_EPS) * prm["norm_g"]
    return (o * _silu(z)).astype(BF16), s_new, last_rows


def _gdn_mix(c_qkv, small, c_z, conv0, s0, s0_layer, stacked, layer, depth, lp, gb, n_chains, c, passes):
    bsz, t_len, n_qkv = c_qkv.shape
    d = c_z.shape[-1]
    n_heads = d // HEAD_DIM
    n_conv = lp["conv_w"].shape[0]
    kern = functools.partial(_gdn_kernel, gb=gb, n_chains=n_chains, c=c, n_heads=n_heads, passes=passes)
    gb = gb * n_chains
    vec = lambda w: pl.BlockSpec((1, w), lambda b, s: (0, 0))
    s_in, s_out, s_shape, extra_specs, extra_args = _state_specs(s0, s0_layer, stacked, layer, depth, bsz, gb, d)
    args = [c_qkv, small, c_z, conv0, s0, lp["conv_w"], lp["a_log"], lp["dt_bias"], lp["norm_g"],
            lp["e_beta"], lp["e_a"]]
    return pl.pallas_call(
        kern,
        grid=(bsz // gb, t_len // c),
        in_specs=[pl.BlockSpec((gb, c, n_qkv), lambda b, s: (b, s, 0)),
                  pl.BlockSpec((gb, c, LANES), lambda b, s: (b, s, 0)),
                  pl.BlockSpec((gb, c, d), lambda b, s: (b, s, 0)),
                  pl.BlockSpec((gb, n_conv - 1, n_qkv), lambda b, s: (b, 0, 0)),
                  s_in,
                  pl.BlockSpec((n_conv, n_qkv), lambda b, s: (0, 0)),
                  vec(d), vec(d), vec(d),
                  pl.BlockSpec((LANES, d), lambda b, s: (0, 0)),
                  pl.BlockSpec((LANES, d), lambda b, s: (0, 0))] + extra_specs,
        out_specs=[pl.BlockSpec((gb, c, d), lambda b, s: (b, s, 0)), s_out],
        out_shape=[jax.ShapeDtypeStruct((bsz, t_len, d), BF16), s_shape],
        input_output_aliases={len(args): 1} if extra_args else {},
        scratch_shapes=[pltpu.VMEM((gb, d, d), F32), pltpu.VMEM((gb, n_conv - 1, n_qkv), F32)],
        compiler_params=_cparams(("parallel", "arbitrary")),
        name="gdn_mix",
    )(*args, *extra_args)


def _log_forget_cumsum(small, bf, carry_scr):
    @pl.when(pl.program_id(1) == 0)
    def _():
        carry_scr[...] = jnp.zeros_like(carry_scr)

    logf = -_softplus(-(small + bf))
    tb = logf.shape[0]
    c = _cumsum_rows(logf, 1, tb) + carry_scr[...]
    carry_scr[...] = c[tb - 1:tb, :]
    return logf, c


def _fox_gate_kernel(small_ref, bf_ref, logf_ref, ccol_ref, crow_ref, carry_scr, *, n_heads):
    logf, c = _log_forget_cumsum(small_ref[...], bf_ref[...], carry_scr)
    logf_ref[...] = logf[:, 0:n_heads]
    ccol_ref[...] = c
    sel = jnp.where(_iota((n_heads, LANES), 0) == _iota((n_heads, LANES), 1), 1.0, 0.0).astype(BF16)
    crow_ref[...] = _mm_exact_lhs(sel, c, _NT)


def _fox_gates(small, bf_pad, n_heads, tb):
    bsz, t_len, _ = small.shape
    kern = functools.partial(_fox_gate_kernel, n_heads=n_heads)
    return pl.pallas_call(
        kern,
        grid=(bsz, t_len // tb),
        in_specs=[pl.BlockSpec((None, tb, LANES), lambda b, s: (b, s, 0)),
                  pl.BlockSpec((1, LANES), lambda b, s: (0, 0))],
        out_specs=[pl.BlockSpec((None, tb, n_heads), lambda b, s: (b, s, 0)),
                   pl.BlockSpec((None, tb, LANES), lambda b, s: (b, s, 0)),
                   pl.BlockSpec((None, n_heads, tb), lambda b, s: (b, 0, s))],
        out_shape=[jax.ShapeDtypeStruct((bsz, t_len, n_heads), F32),
                   jax.ShapeDtypeStruct((bsz, t_len, LANES), F32),
                   jax.ShapeDtypeStruct((bsz, n_heads, t_len), F32)],
        scratch_shapes=[pltpu.VMEM((1, LANES), F32)],
        compiler_params=_cparams(("parallel", "arbitrary")),
        name="fox_gates",
    )(small, bf_pad)


N_AUX = 3


def _proj_prompt_kernel(*refs, widths, n_heads, ln_in, n_alias):
    refs = list(refs)
    x_ref, w_ref, bf_ref, eaux_ref = refs[0:4]
    pos = 4
    if ln_in:
        g_ref, b_ref = refs[pos:pos + 2]
        pos += 2
    pos += n_alias
    (ash_ref, az_ref, bz_ref, cqkv_ref, cz_ref, small_ref, k_ref, v_ref, logf_ref,
     qa_ref, ka_ref, va_ref) = refs[pos:pos + 12]
    pos += 12
    if ln_in:
        xn_ref = refs[pos]
        pos += 1
    carry_scr = refs[pos]

    x = x_ref[...]
    if ln_in:
        mu = jnp.mean(x, axis=-1, keepdims=True)
        xc = x - mu
        var = jnp.mean(xc * xc, axis=-1, keepdims=True)
        x = xc * lax.rsqrt(var + LN_EPS) * g_ref[...] + b_ref[...]
        xn_ref[...] = x
    xb = x.astype(BF16)
    offs = np.concatenate([[0], np.cumsum(widths)])
    seg = lambda i: jnp.dot(xb, w_ref[:, int(offs[i]):int(offs[i + 1])], preferred_element_type=F32)
    ash_ref[...] = seg(0)
    az_ref[...] = seg(1)
    q, k, v = seg(2), seg(3), seg(4)
    bz_ref[...] = seg(5)
    cqkv_ref[...] = seg(6)
    cz_ref[...] = seg(7)
    small = seg(8)
    small_ref[...] = small
    k_ref[...] = k
    v_ref[...] = v

    logf, c = _log_forget_cumsum(small, bf_ref[...], carry_scr)
    logf_ref[...] = logf[:, 0:n_heads]
    lane = _iota((1, LANES), 1)
    hi, mid, lo = _split3(c * LOG2E)
    pieces = jnp.where(lane < n_heads, hi, jnp.where(lane < 2 * n_heads, mid, lo))
    aux_k_all = _dg(pieces, eaux_ref[...], _NN)
    for h in range(n_heads):
        pair, hh = divmod(h, 2)
        own = (lane < HEAD_DIM) if hh == 0 else (lane >= HEAD_DIM)
        a0 = HEAD_DIM if hh == 0 else 0
        aux_q = jnp.where((lane >= a0) & (lane < a0 + N_AUX), 1.0, 0.0)
        aux_v = jnp.where(lane == a0, 1.0, 0.0)
        cols = slice(pair * LANES, (pair + 1) * LANES)
        qa_ref[h] = jnp.where(own, q[:, cols] * (HEAD_DIM ** -0.5 * LOG2E), aux_q).astype(BF16)
        ka_ref[h] = jnp.where(own, k[:, cols], aux_k_all[:, cols]).astype(BF16)
        va_ref[h] = jnp.where(own, v[:, cols], aux_v).astype(BF16)


def _project_prompt(x3, w_all, layer, widths, bf_pad, e_aux, n_heads, tm, ln_params, stacked):
    bsz, t_len, d = x3.shape
    depth, _, n_cols = w_all.shape
    d_fox = n_heads * HEAD_DIM
    ln_in = ln_params is not None
    n_alias = 0 if stacked is None else len(stacked)
    kern = functools.partial(_proj_prompt_kernel, widths=tuple(widths), n_heads=n_heads, ln_in=ln_in,
                             n_alias=n_alias)
    row = lambda w: pl.BlockSpec((None, tm, w), lambda b, s: (b, s, 0))
    per_layer = lambda w: pl.BlockSpec((None, None, tm, w), lambda b, s: (layer, b, s, 0))
    per_head = pl.BlockSpec((None, n_heads, tm, LANES), lambda b, s: (b, 0, s, 0))
    const = lambda shape: pl.BlockSpec(shape, lambda b, s: (0,) * len(shape))
    in_specs = [row(d), pl.BlockSpec((None, d, n_cols), lambda b, s: (layer, 0, 0), pipeline_mode=pl.Buffered(1)),
                const((1, LANES)), const((LANES, n_heads * HEAD_DIM))]
    args = [x3, w_all, bf_pad, e_aux]
    if ln_in:
        in_specs += [const((1, d)), const((1, d))]
        args += [p.reshape(1, d) for p in ln_params]
    aliases = {}
    if stacked is not None:
        for j, buf in enumerate(stacked):
            aliases[len(args)] = 6 + j
            in_specs.append(pl.BlockSpec(memory_space=pl.ANY))
            args.append(buf)
    f32 = lambda *shape: jax.ShapeDtypeStruct(shape, F32)
    out_specs = [row(widths[0]), row(widths[1]), row(widths[5]), row(widths[6]), row(widths[7]), row(LANES),
                 per_layer(d_fox), per_layer(d_fox), per_layer(n_heads), per_head, per_head, per_head]
    out_shape = [f32(bsz, t_len, widths[0]), f32(bsz, t_len, widths[1]), f32(bsz, t_len, widths[5]),
                 f32(bsz, t_len, widths[6]), f32(bsz, t_len, widths[7]), f32(bsz, t_len, LANES),
                 f32(depth, bsz, t_len, d_fox), f32(depth, bsz, t_len, d_fox), f32(depth, bsz, t_len, n_heads)]
    out_shape += [jax.ShapeDtypeStruct((bsz, n_heads, t_len, LANES), BF16)] * 3
    if ln_in:
        out_specs.append(row(d))
        out_shape.append(f32(bsz, t_len, d))
    return pl.pallas_call(
        kern,
        grid=(bsz, t_len // tm),
        in_specs=in_specs,
        out_specs=out_specs,
        out_shape=out_shape,
        input_output_aliases=aliases,
        scratch_shapes=[pltpu.VMEM((1, LANES), F32)],
        compiler_params=_cparams(("parallel", "arbitrary")),
        name="proj_prompt",
    )(*args)


def _fox_prompt_kernel(qa_ref, ka_ref, va_ref, bz_ref, o_ref, s_scr, m_scr, acc_scr, *, blk, n_h):
    qi = pl.program_id(2)
    m_scr[...] = jnp.full_like(m_scr, NEG_BIG)
    acc_scr[...] = jnp.zeros_like(acc_scr)

    def logits(ki, slot):
        k0 = pl.multiple_of(ki * blk, blk)
        for h in range(n_h):
            s_scr[slot, h] = _dg(qa_ref[h], ka_ref[h, pl.ds(k0, blk), :], _NT)

    def consume(ki, slot, masked):
        k0 = pl.multiple_of(ki * blk, blk)
        for h in range(n_h):
            s = s_scr[slot, h]
            if masked:
                s = jnp.where(_iota((blk, blk), 1) <= _iota((blk, blk), 0), s, NEG_BIG)
            m_prev = m_scr[h]
            m_new = jnp.maximum(m_prev, jnp.max(s, axis=1, keepdims=True))
            alpha = jnp.exp2(m_prev - m_new)
            p = jnp.exp2(s - jnp.tile(m_new, (1, blk // LANES)))
            acc_scr[h] = alpha * acc_scr[h] + _dg(p.astype(BF16), va_ref[h, pl.ds(k0, blk), :], _NN)
            m_scr[h] = m_new

    def two_blocks(j, carry):
        logits(2 * j + 1, 0)
        consume(2 * j, 1, False)
        logits(jnp.minimum(2 * j + 2, qi - 1), 1)
        consume(2 * j + 1, 0, False)
        return carry

    logits(qi, 0)

    @pl.when(qi == 0)
    def _():
        consume(qi, 0, True)

    @pl.when(qi > 0)
    def _():
        logits(0, 1)
        consume(qi, 0, True)
        lax.fori_loop(0, qi // 2, two_blocks, 0)

        @pl.when(qi % 2 == 1)
        def _():
            consume(qi - 1, 1, False)

    lane = _iota((1, LANES), 1)
    outs = []
    for pair in range(n_h // 2):
        acc0 = acc_scr[2 * pair]
        acc1 = acc_scr[2 * pair + 1]
        outs.append(jnp.where(lane < HEAD_DIM, acc0 / acc0[:, HEAD_DIM:HEAD_DIM + 1], acc1 / acc1[:, 0:1]))
    o = jnp.concatenate(outs, axis=1) if len(outs) > 1 else outs[0]
    o_ref[...] = (o * _silu(bz_ref[...])).astype(BF16)


def _fox_prompt(qa, ka, va, b_z, blk):
    bsz, n_heads, t_len, _ = qa.shape
    d = b_z.shape[-1]
    n_h = min(n_heads, FOX_HEADS)
    w = (n_h // 2) * LANES
    return pl.pallas_call(
        functools.partial(_fox_prompt_kernel, blk=blk, n_h=n_h),
        grid=(bsz, n_heads // n_h, t_len // blk),
        in_specs=[pl.BlockSpec((None, n_h, blk, LANES), lambda b, p, i: (b, p, i, 0)),
                  pl.BlockSpec((None, n_h, t_len, LANES), lambda b, p, i: (b, p, 0, 0)),
                  pl.BlockSpec((None, n_h, t_len, LANES), lambda b, p, i: (b, p, 0, 0)),
                  pl.BlockSpec((None, blk, w), lambda b, p, i: (b, i, p))],
        out_specs=pl.BlockSpec((None, blk, w), lambda b, p, i: (b, i, p)),
        out_shape=jax.ShapeDtypeStruct((bsz, t_len, d), BF16),
        scratch_shapes=[pltpu.VMEM((2, n_h, blk, blk), F32),
                        pltpu.VMEM((n_h, blk, LANES), F32),
                        pltpu.VMEM((n_h, blk, LANES), F32)],
        compiler_params=_cparams(("parallel", "parallel", "arbitrary")),
        name="fox_prompt",
    )(qa, ka, va, b_z)


def _fox_cached_kernel(q_ref, k_ref, v_ref, ck_ref, cv_ref, clf_ref, ccol_ref, crow_ref, bz_ref, o_ref,
                       *, n_heads, lane_blk):
    t_len, d = q_ref.shape
    past = ck_ref.shape[1]
    n_rows = n_heads * t_len

    triu = jnp.where(_iota((lane_blk, lane_blk), 0) <= _iota((lane_blk, lane_blk), 1), 1.0, 0.0).astype(BF16)
    carry = jnp.zeros((n_heads, 1), F32)
    pieces = []
    for j in range(past // lane_blk):
        cj = _mm_exact_rhs(clf_ref[:, j * lane_blk:(j + 1) * lane_blk], triu) + carry
        carry = cj[:, lane_blk - 1:lane_blk]
        pieces.append(cj)
    c_cache = jnp.concatenate(pieces, axis=1)
    total = carry

    rows = lambda f: jnp.concatenate([f(h) for h in range(n_heads)], axis=0)
    ck_cache = rows(lambda h: jnp.broadcast_to(c_cache[h:h + 1, :], (t_len, past)))
    ck_new = rows(lambda h: jnp.broadcast_to(total[h:h + 1, :] + crow_ref[h:h + 1, :], (t_len, t_len)))
    cq = rows(lambda h: total[h:h + 1, :] + ccol_ref[:, h:h + 1])

    lane_head = _div_pow2(_iota((1, d), 1), HEAD_DIM)
    q = q_ref[...] * (HEAD_DIM ** -0.5)
    q_bd = rows(lambda h: jnp.where(lane_head == h, q, 0.0)).astype(BF16)
    s_c = _dg(q_bd, ck_ref[...].astype(BF16), _NN) + cq - ck_cache
    s_n = _dg(q_bd, k_ref[...].astype(BF16), _NT) + cq - ck_new
    q_pos = _iota((n_rows, t_len), 0) & (t_len - 1)
    s_n = jnp.where(_iota((n_rows, t_len), 1) <= q_pos, s_n, NEG_BIG)
    m = jnp.maximum(jnp.max(s_c, axis=1, keepdims=True), jnp.max(s_n, axis=1, keepdims=True))
    p_c = jnp.exp(s_c - m)
    p_n = jnp.exp(s_n - m)
    l = jnp.sum(p_c, axis=1, keepdims=True) + jnp.sum(p_n, axis=1, keepdims=True)
    o_all = (_dg(p_c.astype(BF16), cv_ref[...].astype(BF16), _NT)
             + _dg(p_n.astype(BF16), v_ref[...].astype(BF16), _NN)) / l
    o = jnp.where(lane_head == 0, o_all[0:t_len], 0.0)
    for h in range(1, n_heads):
        o = o + jnp.where(lane_head == h, o_all[h * t_len:(h + 1) * t_len], 0.0)
    o_ref[...] = (o * _silu(bz_ref[...])).astype(BF16)


def _fox_cached(q, k, v, cache_k, cache_v, cache_logf_t, ccol, crow, b_z, layer):
    bsz, t_len, d = q.shape
    n_heads = d // HEAD_DIM
    past = cache_k.shape[3]
    kern = functools.partial(_fox_cached_kernel, n_heads=n_heads, lane_blk=min(past, 256))
    cur = lambda w: pl.BlockSpec((None, t_len, w), lambda b: (b, 0, 0))
    return pl.pallas_call(
        kern,
        grid=(bsz,),
        in_specs=[cur(d), cur(d), cur(d),
                  pl.BlockSpec((None, None, d, past), lambda b: (layer, b, 0, 0)),
                  pl.BlockSpec((None, None, d, past), lambda b: (layer, b, 0, 0)),
                  pl.BlockSpec((None, None, n_heads, past), lambda b: (layer, b, 0, 0)),
                  cur(LANES),
                  pl.BlockSpec((None, n_heads, t_len), lambda b: (b, 0, 0)),
                  cur(d)],
        out_specs=cur(d),
        out_shape=jax.ShapeDtypeStruct((bsz, t_len, d), BF16),
        compiler_params=_cparams(("parallel",)),
        name="fox_cached",
    )(q, k, v, cache_k, cache_v, cache_logf_t, ccol, crow, b_z)


def _out_kernel(oa_ref, ob_ref, oc_ref, x_ref, w_ref, g_ref, b_ref, y_ref, *, alpha):
    da = oa_ref.shape[-1]
    db = ob_ref.shape[-1]
    h = jnp.dot(oa_ref[...], w_ref[0:da, :], preferred_element_type=F32)
    h = h + jnp.dot(ob_ref[...], w_ref[da:da + db, :], preferred_element_type=F32)
    h = h + jnp.dot(oc_ref[...], w_ref[da + db:, :], preferred_element_type=F32)
    y = alpha * x_ref[...] + h
    mu = jnp.mean(y, axis=-1, keepdims=True)
    yc = y - mu
    var = jnp.mean(yc * yc, axis=-1, keepdims=True)
    y_ref[...] = yc * lax.rsqrt(var + LN_EPS) * g_ref[...] + b_ref[...]


def _out_project(o_a, o_b, o_c, x2, w_all, layer, g, b, alpha, tm):
    n, d = x2.shape
    d_mix = w_all.shape[1]
    row = lambda w: pl.BlockSpec((tm, w), lambda i: (i, 0))
    return pl.pallas_call(
        functools.partial(_out_kernel, alpha=alpha),
        grid=(n // tm,),
        in_specs=[row(o_a.shape[-1]), row(o_b.shape[-1]), row(o_c.shape[-1]), row(d),
                  pl.BlockSpec((None, d_mix, d), lambda i: (layer, 0, 0)),
                  pl.BlockSpec((1, d), lambda i: (0, 0)),
                  pl.BlockSpec((1, d), lambda i: (0, 0))],
        out_specs=row(d),
        out_shape=jax.ShapeDtypeStruct((n, d), F32),
        compiler_params=_cparams(("parallel",)),
        name="proj_out",
    )(o_a, o_b, o_c, x2, w_all, g.reshape(1, d), b.reshape(1, d))


def _expand_heads(p):
    return jnp.repeat(p.astype(F32), HEAD_DIM).reshape(1, -1)


def _select_matrix(rows, n_cols, first_row):
    e = np.zeros((LANES, n_cols), np.float32)
    for h in range(rows):
        e[first_row + h, h * HEAD_DIM:(h + 1) * HEAD_DIM] = 1.0
    return jnp.asarray(e, BF16)


def _row_tile(n, largest=256):
    for tm in (1024, 512, 256, 128, 64, 32, 16, 8):
        if tm <= largest and n % tm == 0:
            return tm
    raise ValueError(f"row count {n} must be a multiple of 8")


def kernel(x_prompt, x_sample, cache_fox_k, cache_fox_v, cache_fox_logf, state_rwkv_shift, state_rwkv_wkv,
           state_gdn_conv, state_gdn_wkv, ln_in_g, ln_in_b, w_in, rwkv_mu, rwkv_w0, rwkv_w2, rwkv_a0, rwkv_a2,
           rwkv_k_k, rwkv_k_a, rwkv_r_k, rwkv_gn_g, rwkv_gn_b, fox_b_f, gdn_conv_w, gdn_a_log, gdn_dt_bias,
           gdn_norm_g, w_out, ln_post_g, ln_post_b):
    depth, d_model, _ = w_in.shape
    bp, seq, _ = x_prompt.shape
    bs, dec_seq, _ = x_sample.shape
    h_fox = fox_b_f.shape[1]
    h_gdn = gdn_a_log.shape[1]
    h_rwkv = rwkv_r_k.shape[1]
    d_rwkv, d_fox, d_gdn = h_rwkv * HEAD_DIM, h_fox * HEAD_DIM, h_gdn * HEAD_DIM
    rank_w = rwkv_w2.shape[1]
    rank_a = rwkv_a2.shape[1]
    n_shift = 3 * d_rwkv + rank_w + rank_a
    n_conv_cols = 3 * d_gdn
    n_conv = gdn_conv_w.shape[1]
    alpha = (2 * depth) ** 0.25
    assert rank_w + rank_a == LANES and N_AUX * h_fox + 2 * h_gdn <= LANES

    split = [n_shift, d_rwkv, d_fox, d_fox, d_fox, h_fox, d_fox, n_conv_cols, h_gdn, h_gdn, d_gdn]
    offs = np.concatenate([[0], np.cumsum(split)])
    seg = lambda i: w_in[:, :, offs[i]:offs[i + 1]]
    small_w = jnp.concatenate([seg(5)] * N_AUX + [seg(8), seg(9)], axis=-1)
    small_w = jnp.pad(small_w, ((0, 0), (0, 0), (0, LANES - small_w.shape[-1])))
    w_in_p = jnp.concatenate([seg(0), seg(1), seg(2), seg(3), seg(4), seg(6), seg(7), seg(10), small_w],
                             axis=-1).astype(BF16)
    widths = [n_shift, d_rwkv, d_fox, d_fox, d_fox, d_fox, n_conv_cols, d_gdn, LANES]
    w_out_b = w_out.astype(BF16)

    zw = jnp.zeros((depth, rank_w, d_rwkv), F32)
    w2a2 = jnp.concatenate([jnp.concatenate([rwkv_w2, zw], axis=-1),
                            jnp.concatenate([zw, rwkv_a2], axis=-1)], axis=1).astype(BF16)
    bf_pad = jnp.pad(jnp.tile(fox_b_f, (1, N_AUX)), ((0, 0), (0, LANES - N_AUX * h_fox)))
    e_beta = _select_matrix(h_gdn, d_gdn, N_AUX * h_fox)
    e_a = _select_matrix(h_gdn, d_gdn, N_AUX * h_fox + h_gdn)
    e_aux_np = np.zeros((LANES, d_fox), np.float32)
    for h in range(h_fox):
        for j in range(N_AUX):
            e_aux_np[j * h_fox + h, (h // 2) * LANES + (HEAD_DIM if h % 2 == 0 else 0) + j] = -1.0
    e_aux = jnp.asarray(e_aux_np, BF16)
    cache_logf_t = jnp.swapaxes(cache_fox_logf, 2, 3)
    cache_k_t = jnp.swapaxes(cache_fox_k.reshape(depth, bs, -1, d_fox), 2, 3)
    cache_v_t = jnp.swapaxes(cache_fox_v.reshape(depth, bs, -1, d_fox), 2, 3)

    def layer_params(l):
        row = lambda a: a[l].reshape(1, -1).astype(F32)
        return dict(
            rwkv=dict(mu=row(rwkv_mu), w0=row(rwkv_w0), a0=row(rwkv_a0), w2a2=w2a2[l], k_k=row(rwkv_k_k),
                      k_a=row(rwkv_k_a), r_k=row(rwkv_r_k), gn_g=row(rwkv_gn_g), gn_b=row(rwkv_gn_b)),
            gdn=dict(conv_w=gdn_conv_w[l], a_log=_expand_heads(gdn_a_log[l]),
                     dt_bias=_expand_heads(gdn_dt_bias[l]),
                     norm_g=jnp.tile(gdn_norm_g[l].reshape(1, -1), (1, h_gdn)), e_beta=e_beta, e_a=e_a),
        )

    def mix_and_project(l, x2, bsz, t_len, a_sh, a_z, c_qkv, small, c_z, o_b, prev0, conv0, s_rwkv0, s_gdn0,
                        s0_layer, states):
        lp = layer_params(l)
        c = min(HEAD_DIM, t_len)
        st_rwkv, st_gdn = (None, None) if states is None else states
        gb = N_STACK // (h_rwkv * c)
        n_chains = max(1, min(N_CHAINS, bsz // gb))
        o_a, st_rwkv = _rwkv_mix(a_sh, a_z, prev0, s_rwkv0, s0_layer, st_rwkv, l, depth, lp["rwkv"], gb, n_chains,
                                 c, CHUNK_PASSES)
        gb = N_STACK // (h_gdn * c)
        o_c, st_gdn = _gdn_mix(c_qkv, small, c_z, conv0, s_gdn0, s0_layer, st_gdn, l, depth, lp["gdn"], gb,
                               n_chains, c, CHUNK_PASSES)
        f2 = lambda a: a.reshape(bsz * t_len, a.shape[-1])
        x_new = _out_project(f2(o_a), f2(o_b), f2(o_c), x2, w_out_b, l, ln_post_g[l], ln_post_b[l], alpha,
                             _row_tile(bsz * t_len, OUT_ROWS))
        return x_new, (a_sh[:, -1], c_qkv[:, t_len - (n_conv - 1):]), (st_rwkv, st_gdn)

    def prompt_layer(l, x2, stacked, states):
        ln_params = (ln_in_g, ln_in_b) if l == 0 else None
        res = _project_prompt(x2.reshape(bp, seq, d_model), w_in_p, l, widths, bf_pad[l:l + 1], e_aux, h_fox,
                              min(seq, PROJ_ROWS), ln_params, stacked)
        a_sh, a_z, b_z, c_qkv, c_z, small, k_st, v_st, logf_st, qa, ka, va = res[:12]
        if l == 0:
            x2 = res[12].reshape(bp * seq, d_model)
        o_b = _fox_prompt(qa, ka, va, b_z, min(seq, FOX_BLOCK))
        x_new, outs, states = mix_and_project(l, x2, bp, seq, a_sh, a_z, c_qkv, small, c_z, o_b, *zeros_p, 0, states)
        return x_new, outs, (k_st, v_st, logf_st), states

    def sample_layer(l, x2, states):
        tm = _row_tile(bs * dec_seq)
        r3 = lambda a: a.reshape(bs, dec_seq, a.shape[-1])
        a_sh, a_z, b_q, b_k, b_v, b_z, c_qkv, c_z, small = map(r3, _project(x2, w_in_p, l, widths, tm))
        logf, ccol, crow = _fox_gates(small, bf_pad[l:l + 1], h_fox, min(dec_seq, 256))
        o_b = _fox_cached(b_q, b_k, b_v, cache_k_t, cache_v_t, cache_logf_t, ccol, crow, b_z, l)
        x_new, outs, states = mix_and_project(
            l, x2, bs, dec_seq, a_sh, a_z, c_qkv, small, c_z, o_b, state_rwkv_shift[l][:, None, :],
            state_gdn_conv[l], state_rwkv_wkv.reshape(depth, bs, d_rwkv, HEAD_DIM),
            state_gdn_wkv.reshape(depth, bs, d_gdn, HEAD_DIM), l, states)
        heads = lambda a: a.reshape(bs, dec_seq, h_fox, HEAD_DIM)
        return x_new, (heads(b_k), heads(b_v), logf) + outs, states

    xp = x_prompt.reshape(bp * seq, d_model)
    xs = _layer_norm(x_sample.reshape(bs * dec_seq, d_model), ln_in_g, ln_in_b, _row_tile(bs * dec_seq))
    zeros_p = (jnp.zeros((bp, 1, n_shift), F32), jnp.zeros((bp, n_conv - 1, n_conv_cols), F32),
               jnp.zeros((1, bp, d_rwkv, HEAD_DIM), F32), jnp.zeros((1, bp, d_gdn, HEAD_DIM), F32))
    outs_p, outs_s, stacked, states_p, states_s = [], [], None, None, None
    for l in range(depth):
        xp, o, stacked, states_p = prompt_layer(l, xp, stacked, states_p)
        outs_p.append(o)
        xs, o, states_s = sample_layer(l, xs, states_s)
        outs_s.append(o)
    k_st, v_st, logf_st = stacked
    stack = lambda outs, i: jnp.stack([o[i] for o in outs])
    blocks = lambda st, bsz: st.reshape(depth, bsz, st.shape[2] // HEAD_DIM, HEAD_DIM, HEAD_DIM)
    return (xp.reshape(bp, seq, d_model), xs.reshape(bs, dec_seq, d_model),
            k_st.reshape(depth, bp, seq, h_fox, HEAD_DIM), v_st.reshape(depth, bp, seq, h_fox, HEAD_DIM), logf_st,
            stack(outs_p, 0), blocks(states_p[0], bp), stack(outs_p, 1), blocks(states_p[1], bp),
            stack(outs_s, 0), stack(outs_s, 1), stack(outs_s, 2),
            stack(outs_s, 3), blocks(states_s[0], bs), stack(outs_s, 4), blocks(states_s[1], bs))
```

```python
import functools
import math

import jax
import jax.numpy as jnp
import numpy as np
from jax import lax
from jax.experimental import pallas as pl
from jax.experimental.pallas import tpu as pltpu

F32 = jnp.float32
BF16 = jnp.bfloat16

HEAD_DIM = 64
LANES = 128
N_STACK = 256
PROJ_ROWS = 512
OUT_ROWS = 1024
FOX_BLOCK = 512
FOX_HEADS = 4
N_CHAINS = 8
CHUNK_PASSES = 1
LN_EPS = 1e-5
RWKV_GN_EPS = 64e-5
GDN_NORM_EPS = 1e-6
L2_EPS = 1e-6
NEG_BIG = -1e30
LOG2E = math.log2(math.e)
VMEM_LIMIT = 56 * 1024 * 1024


def _sigmoid(x):
    return 1.0 / (1.0 + jnp.exp(-x))


def _softplus(x):
    return jnp.maximum(x, 0.0) + jnp.log(1.0 + jnp.exp(-jnp.abs(x)))


def _silu(x):
    return x * _sigmoid(x)


def _split3(x):
    hi = x.astype(BF16)
    r1 = x - hi.astype(F32)
    mid = r1.astype(BF16)
    lo = (r1 - mid.astype(F32)).astype(BF16)
    return hi, mid, lo


_NN = (((1,), (0,)), ((), ()))
_NT = (((1,), (1,)), ((), ()))
_TN = (((0,), (0,)), ((), ()))


def _dg(a, b, dims):
    return lax.dot_general(a, b, dims, preferred_element_type=F32)


def _mm(a, b, dims=_NN, passes=1):
    if passes == 1:
        return _dg(a.astype(BF16), b.astype(BF16), dims)
    ah, am, _ = _split3(a)
    bh, bm, _ = _split3(b)
    return _dg(ah, bh, dims) + (_dg(ah, bm, dims) + _dg(am, bh, dims))


def _mm_exact_rhs(a, e, dims=_NN):
    m = a.shape[0]
    r = _dg(jnp.concatenate(_split3(a), axis=0), e, dims)
    return r[0:m] + (r[m:2 * m] + r[2 * m:3 * m])


def _mm_exact_lhs(e, b, dims=_NN):
    hi, mid, lo = _split3(b)
    return _dg(e, hi, dims) + (_dg(e, mid, dims) + _dg(e, lo, dims))


def _iota(shape, dim):
    return lax.broadcasted_iota(jnp.int32, shape, dim)


def _div_pow2(x, n):
    assert n & (n - 1) == 0
    return lax.shift_right_logical(x, jnp.int32(int(math.log2(n))))


def _head_block_ones(n):
    r = _div_pow2(_iota((n, n), 0), HEAD_DIM)
    c = _div_pow2(_iota((n, n), 1), HEAD_DIM)
    return jnp.where(r == c, 1.0, 0.0).astype(BF16)


def _block_diag(tall):
    n = tall.shape[0]
    wide = jnp.concatenate([tall] * (n // HEAD_DIM), axis=1)
    same = _div_pow2(_iota((n, n), 0), HEAD_DIM) == _div_pow2(_iota((n, n), 1), HEAD_DIM)
    return jnp.where(same, wide, 0.0)


def _diag_tall(s_bd):
    n = s_bd.shape[0]
    return jnp.concatenate([s_bd[h * HEAD_DIM:(h + 1) * HEAD_DIM, h * HEAD_DIM:(h + 1) * HEAD_DIM]
                            for h in range(n // HEAD_DIM)], axis=0)


def _stack_heads(x, gb, c, n_heads):
    width = x.shape[1]
    lane_head = _div_pow2(_iota((1, width), 1), HEAD_DIM)
    parts = []
    for b in range(gb):
        xb = x[b * c:(b + 1) * c, :]
        for h in range(n_heads):
            parts.append(jnp.where(lane_head == h, xb, 0.0))
    return jnp.concatenate(parts, axis=0)


def _ls_masks(gb, c, n_heads):
    shape = (gb * c, gb * n_heads * c)
    row = _iota(shape, 0)
    lane = _iota(shape, 1)
    same_b = _div_pow2(row, c) == _div_pow2(lane, n_heads * c)
    i = row & (c - 1)
    j = lane & (c - 1)
    return same_b & (j < i), same_b & (j <= i), same_b & (j == i)


def _expand_bd(m_ls, gb, c, n_heads):
    lane_h = _div_pow2(_iota((1, m_ls.shape[1]), 1), c) & (n_heads - 1)
    parts = []
    for b in range(gb):
        mb = m_ls[b * c:(b + 1) * c, :]
        for h in range(n_heads):
            parts.append(jnp.where(lane_h == h, mb, 0.0))
    return jnp.concatenate(parts, axis=0)


def _unit_lower_inverse(p, gb, c, n_heads, passes):
    n_rows = p.shape[0]
    expand = functools.partial(_expand_bd, gb=gb, c=c, n_heads=n_heads)
    t = jnp.where(_ls_masks(gb, c, n_heads)[2], 1.0, 0.0) + p
    n_rounds = int(math.log2(c)) - 1
    pk = _mm(p, expand(p), passes=passes)
    yield
    for i in range(n_rounds):
        if i + 1 < n_rounds:
            both = _mm(jnp.concatenate([t, pk], axis=0), expand(pk), passes=passes)
            t = t + both[0:n_rows]
            pk = both[n_rows:2 * n_rows]
        else:
            t = t + _mm(t, expand(pk), passes=passes)
        yield
    return t


def _run_interleaved(chains):
    results = [None] * len(chains)
    active = list(enumerate(chains))
    while active:
        still = []
        for i, ch in active:
            try:
                next(ch)
                still.append((i, ch))
            except StopIteration as stop:
                results[i] = stop.value
        active = still
    return results


def _chunk_masks(n, c):
    row = _iota((n, n), 0)
    col = _iota((n, n), 1)
    same = _div_pow2(row, c) == _div_pow2(col, c)
    strict = jnp.logical_and(same, row > col)
    incl = jnp.logical_and(same, row >= col)
    return strict, incl


def _cumsum_rows(x, gb, c):
    n = gb * c
    strict, incl = _chunk_masks(n, c)
    del strict
    tri = jnp.where(incl, 1.0, 0.0).astype(BF16)
    return _mm_exact_lhs(tri, x)


def _shift_rows(x, j, prev_rows, gb, c):
    n_prev = prev_rows[0].shape[0]
    out = pltpu.roll(x, j, 0)
    rid = _iota((x.shape[0], 1), 0)
    for b in range(gb):
        for t in range(j):
            src = n_prev + t - j
            out = jnp.where(rid == b * c + t, prev_rows[b][src:src + 1, :], out)
    return out


def _cparams(sem):
    return pltpu.CompilerParams(dimension_semantics=sem, vmem_limit_bytes=VMEM_LIMIT)


def _ln_kernel(x_ref, g_ref, b_ref, o_ref):
    x = x_ref[...]
    mu = jnp.mean(x, axis=-1, keepdims=True)
    xc = x - mu
    var = jnp.mean(xc * xc, axis=-1, keepdims=True)
    o_ref[...] = xc * lax.rsqrt(var + LN_EPS) * g_ref[...] + b_ref[...]


def _layer_norm(x2, g, b, tm):
    n, d = x2.shape
    return pl.pallas_call(
        _ln_kernel,
        grid=(n // tm,),
        in_specs=[pl.BlockSpec((tm, d), lambda i: (i, 0)),
                  pl.BlockSpec((1, d), lambda i: (0, 0)),
                  pl.BlockSpec((1, d), lambda i: (0, 0))],
        out_specs=pl.BlockSpec((tm, d), lambda i: (i, 0)),
        out_shape=jax.ShapeDtypeStruct((n, d), F32),
        compiler_params=_cparams(("parallel",)),
        name="ln_in",
    )(x2, g.reshape(1, d), b.reshape(1, d))


def _proj_kernel(x_ref, w_ref, *o_refs):
    xb = x_ref[...].astype(BF16)
    off = 0
    for o_ref in o_refs:
        wdt = o_ref.shape[-1]
        o_ref[...] = jnp.dot(xb, w_ref[:, off:off + wdt], preferred_element_type=F32)
        off += wdt


def _project(x2, w_all, layer, widths, tm):
    n, d = x2.shape
    n_cols = w_all.shape[-1]
    return pl.pallas_call(
        _proj_kernel,
        grid=(n // tm,),
        in_specs=[pl.BlockSpec((tm, d), lambda i: (i, 0)),
                  pl.BlockSpec((None, d, n_cols), lambda i: (layer, 0, 0))],
        out_specs=[pl.BlockSpec((tm, w), lambda i: (i, 0)) for w in widths],
        out_shape=[jax.ShapeDtypeStruct((n, w), F32) for w in widths],
        compiler_params=_cparams(("parallel",)),
        name="proj_in",
    )(x2, w_all)


def _rwkv_kernel(ash_ref, az_ref, prev0_ref, s0_ref, mu_ref, w0_ref, a0_ref, w2a2_ref, kk_ref, ka_ref,
                 rk_ref, gng_ref, gnb_ref, *rest, gb, n_chains, c, n_heads, passes):
    o_ref, st_ref, s_ref, prev_scr = rest[-4:]

    @pl.when(pl.program_id(1) == 0)
    def _():
        for b in range(s_ref.shape[0]):
            s_ref[b] = _block_diag(s0_ref[b])
        prev_scr[...] = prev0_ref[...]

    prm = dict(mu=mu_ref[...], w0=w0_ref[...], a0=a0_ref[...], w2a2=w2a2_ref[...], k_k=kk_ref[...],
               k_a=ka_ref[...], r_k=rk_ref[...], gn_g=gng_ref[...], gn_b=gnb_ref[...])
    ins = []
    for ch in range(n_chains):
        b0 = ch * gb
        ins.append((ash_ref[b0:b0 + gb].reshape(gb * c, ash_ref.shape[-1]),
                    az_ref[b0:b0 + gb].reshape(gb * c, az_ref.shape[-1]),
                    [prev_scr[b0 + b] for b in range(gb)],
                    [s_ref[b0 + b] for b in range(gb)]))
    outs = _run_interleaved([_rwkv_chain(*args, prm, gb=gb, c=c, n_heads=n_heads, passes=passes) for args in ins])
    for ch, (o, s_new, last_rows) in enumerate(outs):
        b0 = ch * gb
        o_ref[b0:b0 + gb] = o.reshape(gb, c, o.shape[-1])
        for b in range(gb):
            s_ref[b0 + b] = s_new[b]
            prev_scr[b0 + b] = last_rows[b]

    @pl.when(pl.program_id(1) == pl.num_programs(1) - 1)
    def _():
        for b in range(s_ref.shape[0]):
            st_ref[b] = _diag_tall(s_ref[b])


def _rwkv_chain(p, z, prev_rows, s_old, prm, *, gb, c, n_heads, passes):
    d = n_heads * HEAD_DIM
    prev = _shift_rows(p, 1, prev_rows, gb, c)
    last_rows = [p[(b + 1) * c - 1:(b + 1) * c, :] for b in range(gb)]
    xs = p + (prev - p) * prm["mu"]
    r = xs[:, 0:d]
    k = xs[:, d:2 * d]
    v = xs[:, 2 * d:3 * d]
    wa = xs[:, 3 * d:3 * d + LANES]
    lane = _iota((1, LANES), 1)
    wa = jnp.where(lane < HEAD_DIM, jnp.tanh(wa), wa)
    pre = _mm(wa, prm["w2a2"])
    yield
    w_ll = -_softplus(-(prm["w0"] + pre[:, 0:d])) - 0.5
    lw = -jnp.exp(w_ll)
    a = _sigmoid(prm["a0"] + pre[:, d:2 * d])

    ones_bd = _head_block_ones(d)
    kk = k * prm["k_k"]
    kk = kk / jnp.maximum(jnp.sqrt(_mm_exact_rhs(kk * kk, ones_bd)), 1e-12)
    kmod = k * (1.0 + (a - 1.0) * prm["k_a"])
    kka = kk * a

    cum = _cumsum_rows(lw, gb, c)
    yield
    cum_last = jnp.concatenate(
        [jnp.broadcast_to(cum[(b + 1) * c - 1:(b + 1) * c, :], (c, d)) for b in range(gb)], axis=0)
    inv_p = jnp.exp(-cum)
    to_end = jnp.exp(cum_last - cum)
    b_t = kk * jnp.exp(cum - lw)
    a_t = -kka * inv_p
    k_t = kmod * inv_p
    r_t = r * jnp.exp(cum)

    st = functools.partial(_stack_heads, gb=gb, c=c, n_heads=n_heads)
    n_rows = gb * c
    n = n_rows * n_heads
    g = _mm(jnp.concatenate([b_t, r_t], axis=0), jnp.concatenate([st(a_t), st(k_t)], axis=0), _NT, passes)
    yield
    strict, incl, _ = _ls_masks(gb, c, n_heads)
    m_ab = jnp.where(strict, g[0:n_rows, 0:n], 0.0)
    m_bk = jnp.where(strict, g[0:n_rows, n:2 * n], 0.0)
    m_ra = jnp.where(incl, g[n_rows:2 * n_rows, 0:n], 0.0)
    m_rk = jnp.where(incl, g[n_rows:2 * n_rows, n:2 * n], 0.0)

    rows = [slice(b * c, (b + 1) * c) for b in range(gb)]
    br = [_mm(jnp.concatenate([b_t[sl], r_t[sl]], axis=0), s_old[b], _NT, passes) for b, sl in enumerate(rows)]
    bh = jnp.concatenate([x[0:c] for x in br], axis=0)
    rh = jnp.concatenate([x[c:2 * c] for x in br], axis=0)
    mv = _mm(jnp.concatenate([m_bk, m_rk], axis=0), st(v), passes=passes)
    rhs_u = bh + mv[0:n_rows]
    o = rh + mv[n_rows:2 * n_rows]
    t_inv = yield from _unit_lower_inverse(m_ab, gb, c, n_heads, passes)
    u = _mm(t_inv, st(rhs_u), passes=passes)
    yield
    o = o + _mm(m_ra, st(u), passes=passes)

    a_end = -kka * to_end
    k_end = kmod * to_end
    same_head = _div_pow2(_iota((d, d), 0), HEAD_DIM) == _div_pow2(_iota((d, d), 1), HEAD_DIM)
    s_new = []
    for b, sl in enumerate(rows):
        p_end = jnp.exp(cum[(b + 1) * c - 1:(b + 1) * c, :])
        upd = _mm(jnp.concatenate([u[sl], v[sl]], axis=0), jnp.concatenate([a_end[sl], k_end[sl]], axis=0),
                  _TN, passes)
        s_new.append(s_old[b] * p_end + jnp.where(same_head, upd, 0.0))
    yield

    mean = _mm_exact_rhs(o, ones_bd) * (1.0 / HEAD_DIM)
    oc = o - mean
    yield
    var = _mm_exact_rhs(oc * oc, ones_bd) * (1.0 / HEAD_DIM)
    o = oc * lax.rsqrt(var + RWKV_GN_EPS) * prm["gn_g"] + prm["gn_b"]
    bonus = _mm_exact_rhs(r * kmod * prm["r_k"], ones_bd) * v
    return ((o + bonus) * _silu(z)).astype(BF16), s_new, last_rows


def _state_specs(s0, s0_layer, stacked, layer, depth, bsz, gb, d):
    in_spec = pl.BlockSpec((None, gb, d, HEAD_DIM), lambda b, s: (s0_layer, b, 0, 0))
    out_spec = pl.BlockSpec((None, gb, d, HEAD_DIM), lambda b, s: (layer, b, 0, 0))
    out_shape = jax.ShapeDtypeStruct((depth, bsz, d, HEAD_DIM), F32)
    extra_specs = [] if stacked is None else [pl.BlockSpec(memory_space=pl.ANY)]
    extra_args = [] if stacked is None else [stacked]
    return in_spec, out_spec, out_shape, extra_specs, extra_args


def _rwkv_mix(a_sh, a_z, prev0, s0, s0_layer, stacked, layer, depth, lp, gb, n_chains, c, passes):
    bsz, t_len, n_shift = a_sh.shape
    d = a_z.shape[-1]
    n_heads = d // HEAD_DIM
    kern = functools.partial(_rwkv_kernel, gb=gb, n_chains=n_chains, c=c, n_heads=n_heads, passes=passes)
    gb = gb * n_chains
    vec = lambda w: pl.BlockSpec((1, w), lambda b, s: (0, 0))
    s_in, s_out, s_shape, extra_specs, extra_args = _state_specs(s0, s0_layer, stacked, layer, depth, bsz, gb, d)
    args = [a_sh, a_z, prev0, s0, lp["mu"], lp["w0"], lp["a0"], lp["w2a2"], lp["k_k"], lp["k_a"], lp["r_k"],
            lp["gn_g"], lp["gn_b"]]
    return pl.pallas_call(
        kern,
        grid=(bsz // gb, t_len // c),
        in_specs=[pl.BlockSpec((gb, c, n_shift), lambda b, s: (b, s, 0)),
                  pl.BlockSpec((gb, c, d), lambda b, s: (b, s, 0)),
                  pl.BlockSpec((gb, 1, n_shift), lambda b, s: (b, 0, 0)),
                  s_in,
                  vec(n_shift), vec(d), vec(d),
                  pl.BlockSpec((LANES, 2 * d), lambda b, s: (0, 0)),
                  vec(d), vec(d), vec(d), vec(d), vec(d)] + extra_specs,
        out_specs=[pl.BlockSpec((gb, c, d), lambda b, s: (b, s, 0)), s_out],
        out_shape=[jax.ShapeDtypeStruct((bsz, t_len, d), BF16), s_shape],
        input_output_aliases={len(args): 1} if extra_args else {},
        scratch_shapes=[pltpu.VMEM((gb, d, d), F32), pltpu.VMEM((gb, 1, n_shift), F32)],
        compiler_params=_cparams(("parallel", "arbitrary")),
        name="rwkv_mix",
    )(*args, *extra_args)


def _gdn_kernel(qkv_ref, small_ref, cz_ref, conv0_ref, s0_ref, convw_ref, alog_ref, dtb_ref, ng_ref,
                eb_ref, ea_ref, *rest, gb, n_chains, c, n_heads, passes):
    o_ref, st_ref, s_ref, conv_scr = rest[-4:]

    @pl.when(pl.program_id(1) == 0)
    def _():
        for b in range(s_ref.shape[0]):
            s_ref[b] = _block_diag(s0_ref[b])
        conv_scr[...] = conv0_ref[...]

    prm = dict(conv_w=convw_ref[...], a_log=alog_ref[...], dt_bias=dtb_ref[...], norm_g=ng_ref[...],
               e_beta=eb_ref[...], e_a=ea_ref[...])
    ins = []
    for ch in range(n_chains):
        b0 = ch * gb
        ins.append((qkv_ref[b0:b0 + gb].reshape(gb * c, qkv_ref.shape[-1]),
                    small_ref[b0:b0 + gb].reshape(gb * c, LANES),
                    cz_ref[b0:b0 + gb].reshape(gb * c, cz_ref.shape[-1]),
                    [conv_scr[b0 + b] for b in range(gb)],
                    [s_ref[b0 + b] for b in range(gb)]))
    outs = _run_interleaved([_gdn_chain(*args, prm, gb=gb, c=c, n_heads=n_heads, passes=passes) for args in ins])
    for ch, (o, s_new, last_rows) in enumerate(outs):
        b0 = ch * gb
        o_ref[b0:b0 + gb] = o.reshape(gb, c, o.shape[-1])
        for b in range(gb):
            s_ref[b0 + b] = s_new[b]
            conv_scr[b0 + b] = last_rows[b]

    @pl.when(pl.program_id(1) == pl.num_programs(1) - 1)
    def _():
        for b in range(s_ref.shape[0]):
            st_ref[b] = _diag_tall(s_ref[b])


def _gdn_chain(x, small, z, prev_rows, s_old, prm, *, gb, c, n_heads, passes):
    d = n_heads * HEAD_DIM
    conv_w = prm["conv_w"]
    n_conv = conv_w.shape[0]
    y = x * conv_w[n_conv - 1:n_conv, :]
    for j in range(1, n_conv):
        y = y + _shift_rows(x, j, prev_rows, gb, c) * conv_w[n_conv - 1 - j:n_conv - j, :]
    last_rows = [x[(b + 1) * c - (n_conv - 1):(b + 1) * c, :] for b in range(gb)]
    y = _silu(y)
    q = y[:, 0:d]
    k = y[:, d:2 * d]
    v = y[:, 2 * d:3 * d]
    ones_bd = _head_block_ones(d)
    q = q * lax.rsqrt(_mm_exact_rhs(q * q, ones_bd) + L2_EPS) * (HEAD_DIM ** -0.5)
    k = k * lax.rsqrt(_mm_exact_rhs(k * k, ones_bd) + L2_EPS)

    beta = _sigmoid(_mm_exact_rhs(small, prm["e_beta"]))
    g = -jnp.exp(prm["a_log"]) * _softplus(_mm_exact_rhs(small, prm["e_a"]) + prm["dt_bias"])
    yield
    gc = _cumsum_rows(g, gb, c)
    yield
    gc_last = jnp.concatenate(
        [jnp.broadcast_to(gc[(b + 1) * c - 1:(b + 1) * c, :], (c, d)) for b in range(gb)], axis=0)
    egc = jnp.exp(gc)

    st = functools.partial(_stack_heads, gb=gb, c=c, n_heads=n_heads)
    n_rows = gb * c
    n = n_rows * n_heads
    kb = k * beta
    strict, incl, _ = _ls_masks(gb, c, n_heads)
    gcol = jnp.min(st(gc), axis=1, keepdims=True)
    eye = _iota((n, n), 0) == _iota((n, n), 1)
    grow = jnp.sum(jnp.where(eye, jnp.broadcast_to(gcol, (n, n)), 0.0), axis=0, keepdims=True)
    if gb == 1 and c == HEAD_DIM:
        gc_ls = gc
    else:
        lane_h = _div_pow2(_iota((d, n), 1), c) & (n_heads - 1)
        pick = jnp.where(_iota((d, n), 0) == lane_h * HEAD_DIM, 1.0, 0.0).astype(BF16)
        gc_ls = _mm_exact_rhs(gc, pick)
    dmat = jnp.exp(jnp.where(incl, gc_ls - grow, 0.0))
    g2 = _mm(jnp.concatenate([kb, q], axis=0), st(k), _NT, passes)
    yield
    m = jnp.where(strict, g2[0:n_rows] * dmat, 0.0)
    qk = jnp.where(incl, g2[n_rows:2 * n_rows] * dmat, 0.0)
    t_inv = yield from _unit_lower_inverse(-m, gb, c, n_heads, passes)
    rhs = jnp.concatenate([st(v * beta), st(kb * egc)], axis=1)
    sol = _mm(t_inv, rhs, passes=passes)
    yield
    u = sol[:, 0:d]
    w = sol[:, d:2 * d]

    rows = [slice(b * c, (b + 1) * c) for b in range(gb)]
    qg = q * egc
    kd = k * jnp.exp(gc_last - gc)
    v_new = u - jnp.concatenate([_mm(w[sl], s_old[b], passes=passes) for b, sl in enumerate(rows)], axis=0)
    o = jnp.concatenate([_mm(qg[sl], s_old[b], passes=passes) for b, sl in enumerate(rows)], axis=0)
    yield
    o = o + _mm(qk, st(v_new), passes=passes)
    same_head = _div_pow2(_iota((d, d), 0), HEAD_DIM) == _div_pow2(_iota((d, d), 1), HEAD_DIM)
    s_new = []
    for b, sl in enumerate(rows):
        gl = jnp.exp(gc[(b + 1) * c - 1:(b + 1) * c, :])
        s_new.append(s_old[b] * gl + jnp.where(same_head, _mm(kd[sl], v_new[sl], _TN, passes), 0.0))
    yield

    ms = _mm_exact_rhs(o * o, ones_bd) * (1.0 / HEAD_DIM)
    o = o * lax.rsqrt(ms + GDN_NORM_EPS) * prm["norm_g"]
    return (o * _silu(z)).astype(BF16), s_new, last_rows


def _gdn_mix(c_qkv, small, c_z, conv0, s0, s0_layer, stacked, layer, depth, lp, gb, n_chains, c, passes):
    bsz, t_len, n_qkv = c_qkv.shape
    d = c_z.shape[-1]
    n_heads = d // HEAD_DIM
    n_conv = lp["conv_w"].shape[0]
    kern = functools.partial(_gdn_kernel, gb=gb, n_chains=n_chains, c=c, n_heads=n_heads, passes=passes)
    gb = gb * n_chains
    vec = lambda w: pl.BlockSpec((1, w), lambda b, s: (0, 0))
    s_in, s_out, s_shape, extra_specs, extra_args = _state_specs(s0, s0_layer, stacked, layer, depth, bsz, gb, d)
    args = [c_qkv, small, c_z, conv0, s0, lp["conv_w"], lp["a_log"], lp["dt_bias"], lp["norm_g"],
            lp["e_beta"], lp["e_a"]]
    return pl.pallas_call(
        kern,
        grid=(bsz // gb, t_len // c),
        in_specs=[pl.BlockSpec((gb, c, n_qkv), lambda b, s: (b, s, 0)),
                  pl.BlockSpec((gb, c, LANES), lambda b, s: (b, s, 0)),
                  pl.BlockSpec((gb, c, d), lambda b, s: (b, s, 0)),
                  pl.BlockSpec((gb, n_conv - 1, n_qkv), lambda b, s: (b, 0, 0)),
                  s_in,
                  pl.BlockSpec((n_conv, n_qkv), lambda b, s: (0, 0)),
                  vec(d), vec(d), vec(d),
                  pl.BlockSpec((LANES, d), lambda b, s: (0, 0)),
                  pl.BlockSpec((LANES, d), lambda b, s: (0, 0))] + extra_specs,
        out_specs=[pl.BlockSpec((gb, c, d), lambda b, s: (b, s, 0)), s_out],
        out_shape=[jax.ShapeDtypeStruct((bsz, t_len, d), BF16), s_shape],
        input_output_aliases={len(args): 1} if extra_args else {},
        scratch_shapes=[pltpu.VMEM((gb, d, d), F32), pltpu.VMEM((gb, n_conv - 1, n_qkv), F32)],
        compiler_params=_cparams(("parallel", "arbitrary")),
        name="gdn_mix",
    )(*args, *extra_args)


def _log_forget_cumsum(small, bf, carry_scr):
    @pl.when(pl.program_id(1) == 0)
    def _():
        carry_scr[...] = jnp.zeros_like(carry_scr)

    logf = -_softplus(-(small + bf))
    tb = logf.shape[0]
    c = _cumsum_rows(logf, 1, tb) + carry_scr[...]
    carry_scr[...] = c[tb - 1:tb, :]
    return logf, c


def _fox_gate_kernel(small_ref, bf_ref, logf_ref, ccol_ref, crow_ref, carry_scr, *, n_heads):
    logf, c = _log_forget_cumsum(small_ref[...], bf_ref[...], carry_scr)
    logf_ref[...] = logf[:, 0:n_heads]
    ccol_ref[...] = c
    sel = jnp.where(_iota((n_heads, LANES), 0) == _iota((n_heads, LANES), 1), 1.0, 0.0).astype(BF16)
    crow_ref[...] = _mm_exact_lhs(sel, c, _NT)


def _fox_gates(small, bf_pad, n_heads, tb):
    bsz, t_len, _ = small.shape
    kern = functools.partial(_fox_gate_kernel, n_heads=n_heads)
    return pl.pallas_call(
        kern,
        grid=(bsz, t_len // tb),
        in_specs=[pl.BlockSpec((None, tb, LANES), lambda b, s: (b, s, 0)),
                  pl.BlockSpec((1, LANES), lambda b, s: (0, 0))],
        out_specs=[pl.BlockSpec((None, tb, n_heads), lambda b, s: (b, s, 0)),
                   pl.BlockSpec((None, tb, LANES), lambda b, s: (b, s, 0)),
                   pl.BlockSpec((None, n_heads, tb), lambda b, s: (b, 0, s))],
        out_shape=[jax.ShapeDtypeStruct((bsz, t_len, n_heads), F32),
                   jax.ShapeDtypeStruct((bsz, t_len, LANES), F32),
                   jax.ShapeDtypeStruct((bsz, n_heads, t_len), F32)],
        scratch_shapes=[pltpu.VMEM((1, LANES), F32)],
        compiler_params=_cparams(("parallel", "arbitrary")),
        name="fox_gates",
    )(small, bf_pad)


N_AUX = 3


def _proj_prompt_kernel(*refs, widths, n_heads, ln_in, n_alias):
    refs = list(refs)
    x_ref, w_ref, bf_ref, eaux_ref = refs[0:4]
    pos = 4
    if ln_in:
        g_ref, b_ref = refs[pos:pos + 2]
        pos += 2
    pos += n_alias
    (ash_ref, az_ref, bz_ref, cqkv_ref, cz_ref, small_ref, k_ref, v_ref, logf_ref,
     qa_ref, ka_ref, va_ref) = refs[pos:pos + 12]
    pos += 12
    if ln_in:
        xn_ref = refs[pos]
        pos += 1
    carry_scr = refs[pos]

    x = x_ref[...]
    if ln_in:
        mu = jnp.mean(x, axis=-1, keepdims=True)
        xc = x - mu
        var = jnp.mean(xc * xc, axis=-1, keepdims=True)
        x = xc * lax.rsqrt(var + LN_EPS) * g_ref[...] + b_ref[...]
        xn_ref[...] = x
    xb = x.astype(BF16)
    offs = np.concatenate([[0], np.cumsum(widths)])
    seg = lambda i: jnp.dot(xb, w_ref[:, int(offs[i]):int(offs[i + 1])], preferred_element_type=F32)
    ash_ref[...] = seg(0)
    az_ref[...] = seg(1)
    q, k, v = seg(2), seg(3), seg(4)
    bz_ref[...] = seg(5)
    cqkv_ref[...] = seg(6)
    cz_ref[...] = seg(7)
    small = seg(8)
    small_ref[...] = small
    k_ref[...] = k
    v_ref[...] = v

    logf, c = _log_forget_cumsum(small, bf_ref[...], carry_scr)
    logf_ref[...] = logf[:, 0:n_heads]
    lane = _iota((1, LANES), 1)
    hi, mid, lo = _split3(c * LOG2E)
    pieces = jnp.where(lane < n_heads, hi, jnp.where(lane < 2 * n_heads, mid, lo))
    aux_k_all = _dg(pieces, eaux_ref[...], _NN)
    for h in range(n_heads):
        pair, hh = divmod(h, 2)
        own = (lane < HEAD_DIM) if hh == 0 else (lane >= HEAD_DIM)
        a0 = HEAD_DIM if hh == 0 else 0
        aux_q = jnp.where((lane >= a0) & (lane < a0 + N_AUX), 1.0, 0.0)
        aux_v = jnp.where(lane == a0, 1.0, 0.0)
        cols = slice(pair * LANES, (pair + 1) * LANES)
        qa_ref[h] = jnp.where(own, q[:, cols] * (HEAD_DIM ** -0.5 * LOG2E), aux_q).astype(BF16)
        ka_ref[h] = jnp.where(own, k[:, cols], aux_k_all[:, cols]).astype(BF16)
        va_ref[h] = jnp.where(own, v[:, cols], aux_v).astype(BF16)


def _project_prompt(x3, w_all, layer, widths, bf_pad, e_aux, n_heads, tm, ln_params, stacked):
    bsz, t_len, d = x3.shape
    depth, _, n_cols = w_all.shape
    d_fox = n_heads * HEAD_DIM
    ln_in = ln_params is not None
    n_alias = 0 if stacked is None else len(stacked)
    kern = functools.partial(_proj_prompt_kernel, widths=tuple(widths), n_heads=n_heads, ln_in=ln_in,
                             n_alias=n_alias)
    row = lambda w: pl.BlockSpec((None, tm, w), lambda b, s: (b, s, 0))
    per_layer = lambda w: pl.BlockSpec((None, None, tm, w), lambda b, s: (layer, b, s, 0))
    per_head = pl.BlockSpec((None, n_heads, tm, LANES), lambda b, s: (b, 0, s, 0))
    const = lambda shape: pl.BlockSpec(shape, lambda b, s: (0,) * len(shape))
    in_specs = [row(d), pl.BlockSpec((None, d, n_cols), lambda b, s: (layer, 0, 0), pipeline_mode=pl.Buffered(1)),
                const((1, LANES)), const((LANES, n_heads * HEAD_DIM))]
    args = [x3, w_all, bf_pad, e_aux]
    if ln_in:
        in_specs += [const((1, d)), const((1, d))]
        args += [p.reshape(1, d) for p in ln_params]
    aliases = {}
    if stacked is not None:
        for j, buf in enumerate(stacked):
            aliases[len(args)] = 6 + j
            in_specs.append(pl.BlockSpec(memory_space=pl.ANY))
            args.append(buf)
    f32 = lambda *shape: jax.ShapeDtypeStruct(shape, F32)
    out_specs = [row(widths[0]), row(widths[1]), row(widths[5]), row(widths[6]), row(widths[7]), row(LANES),
                 per_layer(d_fox), per_layer(d_fox), per_layer(n_heads), per_head, per_head, per_head]
    out_shape = [f32(bsz, t_len, widths[0]), f32(bsz, t_len, widths[1]), f32(bsz, t_len, widths[5]),
                 f32(bsz, t_len, widths[6]), f32(bsz, t_len, widths[7]), f32(bsz, t_len, LANES),
                 f32(depth, bsz, t_len, d_fox), f32(depth, bsz, t_len, d_fox), f32(depth, bsz, t_len, n_heads)]
    out_shape += [jax.ShapeDtypeStruct((bsz, n_heads, t_len, LANES), BF16)] * 3
    if ln_in:
        out_specs.append(row(d))
        out_shape.append(f32(bsz, t_len, d))
    return pl.pallas_call(
        kern,
        grid=(bsz, t_len // tm),
        in_specs=in_specs,
        out_specs=out_specs,
        out_shape=out_shape,
        input_output_aliases=aliases,
        scratch_shapes=[pltpu.VMEM((1, LANES), F32)],
        compiler_params=_cparams(("parallel", "arbitrary")),
        name="proj_prompt",
    )(*args)


def _fox_prompt_kernel(qa_ref, ka_ref, va_ref, bz_ref, o_ref, s_scr, m_scr, acc_scr, *, blk, n_h):
    qi = pl.program_id(2)
    m_scr[...] = jnp.full_like(m_scr, NEG_BIG)
    acc_scr[...] = jnp.zeros_like(acc_scr)

    def logits(ki, slot):
        k0 = pl.multiple_of(ki * blk, blk)
        for h in range(n_h):
            s_scr[slot, h] = _dg(qa_ref[h], ka_ref[h, pl.ds(k0, blk), :], _NT)

    def consume(ki, slot, masked):
        k0 = pl.multiple_of(ki * blk, blk)
        for h in range(n_h):
            s = s_scr[slot, h]
            if masked:
                s = jnp.where(_iota((blk, blk), 1) <= _iota((blk, blk), 0), s, NEG_BIG)
            m_prev = m_scr[h]
            m_new = jnp.maximum(m_prev, jnp.max(s, axis=1, keepdims=True))
            alpha = jnp.exp2(m_prev - m_new)
            p = jnp.exp2(s - jnp.tile(m_new, (1, blk // LANES)))
            acc_scr[h] = alpha * acc_scr[h] + _dg(p.astype(BF16), va_ref[h, pl.ds(k0, blk), :], _NN)
            m_scr[h] = m_new

    def two_blocks(j, carry):
        logits(2 * j + 1, 0)
        consume(2 * j, 1, False)
        logits(jnp.minimum(2 * j + 2, qi - 1), 1)
        consume(2 * j + 1, 0, False)
        return carry

    logits(qi, 0)

    @pl.when(qi == 0)
    def _():
        consume(qi, 0, True)

    @pl.when(qi > 0)
    def _():
        logits(0, 1)
        consume(qi, 0, True)
        lax.fori_loop(0, qi // 2, two_blocks, 0)

        @pl.when(qi % 2 == 1)
        def _():
            consume(qi - 1, 1, False)

    lane = _iota((1, LANES), 1)
    outs = []
    for pair in range(n_h // 2):
        acc0 = acc_scr[2 * pair]
        acc1 = acc_scr[2 * pair + 1]
        outs.append(jnp.where(lane < HEAD_DIM, acc0 / acc0[:, HEAD_DIM:HEAD_DIM + 1], acc1 / acc1[:, 0:1]))
    o = jnp.concatenate(outs, axis=1) if len(outs) > 1 else outs[0]
    o_ref[...] = (o * _silu(bz_ref[...])).astype(BF16)


def _fox_prompt(qa, ka, va, b_z, blk):
    bsz, n_heads, t_len, _ = qa.shape
    d = b_z.shape[-1]
    n_h = min(n_heads, FOX_HEADS)
    w = (n_h // 2) * LANES
    return pl.pallas_call(
        functools.partial(_fox_prompt_kernel, blk=blk, n_h=n_h),
        grid=(bsz, n_heads // n_h, t_len // blk),
        in_specs=[pl.BlockSpec((None, n_h, blk, LANES), lambda b, p, i: (b, p, i, 0)),
                  pl.BlockSpec((None, n_h, t_len, LANES), lambda b, p, i: (b, p, 0, 0)),
                  pl.BlockSpec((None, n_h, t_len, LANES), lambda b, p, i: (b, p, 0, 0)),
                  pl.BlockSpec((None, blk, w), lambda b, p, i: (b, i, p))],
        out_specs=pl.BlockSpec((None, blk, w), lambda b, p, i: (b, i, p)),
        out_shape=jax.ShapeDtypeStruct((bsz, t_len, d), BF16),
        scratch_shapes=[pltpu.VMEM((2, n_h, blk, blk), F32),
                        pltpu.VMEM((n_h, blk, LANES), F32),
                        pltpu.VMEM((n_h, blk, LANES), F32)],
        compiler_params=_cparams(("parallel", "parallel", "arbitrary")),
        name="fox_prompt",
    )(qa, ka, va, b_z)


def _fox_cached_kernel(q_ref, k_ref, v_ref, ck_ref, cv_ref, clf_ref, ccol_ref, crow_ref, bz_ref, o_ref,
                       *, n_heads, lane_blk):
    t_len, d = q_ref.shape
    past = ck_ref.shape[1]
    n_rows = n_heads * t_len

    triu = jnp.where(_iota((lane_blk, lane_blk), 0) <= _iota((lane_blk, lane_blk), 1), 1.0, 0.0).astype(BF16)
    carry = jnp.zeros((n_heads, 1), F32)
    pieces = []
    for j in range(past // lane_blk):
        cj = _mm_exact_rhs(clf_ref[:, j * lane_blk:(j + 1) * lane_blk], triu) + carry
        carry = cj[:, lane_blk - 1:lane_blk]
        pieces.append(cj)
    c_cache = jnp.concatenate(pieces, axis=1)
    total = carry

    rows = lambda f: jnp.concatenate([f(h) for h in range(n_heads)], axis=0)
    ck_cache = rows(lambda h: jnp.broadcast_to(c_cache[h:h + 1, :], (t_len, past)))
    ck_new = rows(lambda h: jnp.broadcast_to(total[h:h + 1, :] + crow_ref[h:h + 1, :], (t_len, t_len)))
    cq = rows(lambda h: total[h:h + 1, :] + ccol_ref[:, h:h + 1])

    lane_head = _div_pow2(_iota((1, d), 1), HEAD_DIM)
    q = q_ref[...] * (HEAD_DIM ** -0.5)
    q_bd = rows(lambda h: jnp.where(lane_head == h, q, 0.0)).astype(BF16)
    s_c = _dg(q_bd, ck_ref[...].astype(BF16), _NN) + cq - ck_cache
    s_n = _dg(q_bd, k_ref[...].astype(BF16), _NT) + cq - ck_new
    q_pos = _iota((n_rows, t_len), 0) & (t_len - 1)
    s_n = jnp.where(_iota((n_rows, t_len), 1) <= q_pos, s_n, NEG_BIG)
    m = jnp.maximum(jnp.max(s_c, axis=1, keepdims=True), jnp.max(s_n, axis=1, keepdims=True))
    p_c = jnp.exp(s_c - m)
    p_n = jnp.exp(s_n - m)
    l = jnp.sum(p_c, axis=1, keepdims=True) + jnp.sum(p_n, axis=1, keepdims=True)
    o_all = (_dg(p_c.astype(BF16), cv_ref[...].astype(BF16), _NT)
             + _dg(p_n.astype(BF16), v_ref[...].astype(BF16), _NN)) / l
    o = jnp.where(lane_head == 0, o_all[0:t_len], 0.0)
    for h in range(1, n_heads):
        o = o + jnp.where(lane_head == h, o_all[h * t_len:(h + 1) * t_len], 0.0)
    o_ref[...] = (o * _silu(bz_ref[...])).astype(BF16)


def _fox_cached(q, k, v, cache_k, cache_v, cache_logf_t, ccol, crow, b_z, layer):
    bsz, t_len, d = q.shape
    n_heads = d // HEAD_DIM
    past = cache_k.shape[3]
    kern = functools.partial(_fox_cached_kernel, n_heads=n_heads, lane_blk=min(past, 256))
    cur = lambda w: pl.BlockSpec((None, t_len, w), lambda b: (b, 0, 0))
    return pl.pallas_call(
        kern,
        grid=(bsz,),
        in_specs=[cur(d), cur(d), cur(d),
                  pl.BlockSpec((None, None, d, past), lambda b: (layer, b, 0, 0)),
                  pl.BlockSpec((None, None, d, past), lambda b: (layer, b, 0, 0)),
                  pl.BlockSpec((None, None, n_heads, past), lambda b: (layer, b, 0, 0)),
                  cur(LANES),
                  pl.BlockSpec((None, n_heads, t_len), lambda b: (b, 0, 0)),
                  cur(d)],
        out_specs=cur(d),
        out_shape=jax.ShapeDtypeStruct((bsz, t_len, d), BF16),
        compiler_params=_cparams(("parallel",)),
        name="fox_cached",
    )(q, k, v, cache_k, cache_v, cache_logf_t, ccol, crow, b_z)


def _out_kernel(oa_ref, ob_ref, oc_ref, x_ref, w_ref, g_ref, b_ref, y_ref, *, alpha):
    da = oa_ref.shape[-1]
    db = ob_ref.shape[-1]
    h = jnp.dot(oa_ref[...], w_ref[0:da, :], preferred_element_type=F32)
    h = h + jnp.dot(ob_ref[...], w_ref[da:da + db, :], preferred_element_type=F32)
    h = h + jnp.dot(oc_ref[...], w_ref[da + db:, :], preferred_element_type=F32)
    y = alpha * x_ref[...] + h
    mu = jnp.mean(y, axis=-1, keepdims=True)
    yc = y - mu
    var = jnp.mean(yc * yc, axis=-1, keepdims=True)
    y_ref[...] = yc * lax.rsqrt(var + LN_EPS) * g_ref[...] + b_ref[...]


def _out_project(o_a, o_b, o_c, x2, w_all, layer, g, b, alpha, tm):
    n, d = x2.shape
    d_mix = w_all.shape[1]
    row = lambda w: pl.BlockSpec((tm, w), lambda i: (i, 0))
    return pl.pallas_call(
        functools.partial(_out_kernel, alpha=alpha),
        grid=(n // tm,),
        in_specs=[row(o_a.shape[-1]), row(o_b.shape[-1]), row(o_c.shape[-1]), row(d),
                  pl.BlockSpec((None, d_mix, d), lambda i: (layer, 0, 0)),
                  pl.BlockSpec((1, d), lambda i: (0, 0)),
                  pl.BlockSpec((1, d), lambda i: (0, 0))],
        out_specs=row(d),
        out_shape=jax.ShapeDtypeStruct((n, d), F32),
        compiler_params=_cparams(("parallel",)),
        name="proj_out",
    )(o_a, o_b, o_c, x2, w_all, g.reshape(1, d), b.reshape(1, d))


def _expand_heads(p):
    return jnp.repeat(p.astype(F32), HEAD_DIM).reshape(1, -1)


def _select_matrix(rows, n_cols, first_row):
    e = np.zeros((LANES, n_cols), np.float32)
    for h in range(rows):
        e[first_row + h, h * HEAD_DIM:(h + 1) * HEAD_DIM] = 1.0
    return jnp.asarray(e, BF16)


def _row_tile(n, largest=256):
    for tm in (1024, 512, 256, 128, 64, 32, 16, 8):
        if tm <= largest and n % tm == 0:
            return tm
    raise ValueError(f"row count {n} must be a multiple of 8")


def kernel(x_prompt, x_sample, cache_fox_k, cache_fox_v, cache_fox_logf, state_rwkv_shift, state_rwkv_wkv,
           state_gdn_conv, state_gdn_wkv, ln_in_g, ln_in_b, w_in, rwkv_mu, rwkv_w0, rwkv_w2, rwkv_a0, rwkv_a2,
           rwkv_k_k, rwkv_k_a, rwkv_r_k, rwkv_gn_g, rwkv_gn_b, fox_b_f, gdn_conv_w, gdn_a_log, gdn_dt_bias,
           gdn_norm_g, w_out, ln_post_g, ln_post_b):
    depth, d_model, _ = w_in.shape
    bp, seq, _ = x_prompt.shape
    bs, dec_seq, _ = x_sample.shape
    h_fox = fox_b_f.shape[1]
    h_gdn = gdn_a_log.shape[1]
    h_rwkv = rwkv_r_k.shape[1]
    d_rwkv, d_fox, d_gdn = h_rwkv * HEAD_DIM, h_fox * HEAD_DIM, h_gdn * HEAD_DIM
    rank_w = rwkv_w2.shape[1]
    rank_a = rwkv_a2.shape[1]
    n_shift = 3 * d_rwkv + rank_w + rank_a
    n_conv_cols = 3 * d_gdn
    n_conv = gdn_conv_w.shape[1]
    alpha = (2 * depth) ** 0.25
    assert rank_w + rank_a == LANES and N_AUX * h_fox + 2 * h_gdn <= LANES

    split = [n_shift, d_rwkv, d_fox, d_fox, d_fox, h_fox, d_fox, n_conv_cols, h_gdn, h_gdn, d_gdn]
    offs = np.concatenate([[0], np.cumsum(split)])
    seg = lambda i: w_in[:, :, offs[i]:offs[i + 1]]
    small_w = jnp.concatenate([seg(5)] * N_AUX + [seg(8), seg(9)], axis=-1)
    small_w = jnp.pad(small_w, ((0, 0), (0, 0), (0, LANES - small_w.shape[-1])))
    w_in_p = jnp.concatenate([seg(0), seg(1), seg(2), seg(3), seg(4), seg(6), seg(7), seg(10), small_w],
                             axis=-1).astype(BF16)
    widths = [n_shift, d_rwkv, d_fox, d_fox, d_fox, d_fox, n_conv_cols, d_gdn, LANES]
    w_out_b = w_out.astype(BF16)

    zw = jnp.zeros((depth, rank_w, d_rwkv), F32)
    w2a2 = jnp.concatenate([jnp.concatenate([rwkv_w2, zw], axis=-1),
                            jnp.concatenate([zw, rwkv_a2], axis=-1)], axis=1).astype(BF16)
    bf_pad = jnp.pad(jnp.tile(fox_b_f, (1, N_AUX)), ((0, 0), (0, LANES - N_AUX * h_fox)))
    e_beta = _select_matrix(h_gdn, d_gdn, N_AUX * h_fox)
    e_a = _select_matrix(h_gdn, d_gdn, N_AUX * h_fox + h_gdn)
    e_aux_np = np.zeros((LANES, d_fox), np.float32)
    for h in range(h_fox):
        for j in range(N_AUX):
            e_aux_np[j * h_fox + h, (h // 2) * LANES + (HEAD_DIM if h % 2 == 0 else 0) + j] = -1.0
    e_aux = jnp.asarray(e_aux_np, BF16)
    cache_logf_t = jnp.swapaxes(cache_fox_logf, 2, 3)
    cache_k_t = jnp.swapaxes(cache_fox_k.reshape(depth, bs, -1, d_fox), 2, 3)
    cache_v_t = jnp.swapaxes(cache_fox_v.reshape(depth, bs, -1, d_fox), 2, 3)

    def layer_params(l):
        row = lambda a: a[l].reshape(1, -1).astype(F32)
        return dict(
            rwkv=dict(mu=row(rwkv_mu), w0=row(rwkv_w0), a0=row(rwkv_a0), w2a2=w2a2[l], k_k=row(rwkv_k_k),
                      k_a=row(rwkv_k_a), r_k=row(rwkv_r_k), gn_g=row(rwkv_gn_g), gn_b=row(rwkv_gn_b)),
            gdn=dict(conv_w=gdn_conv_w[l], a_log=_expand_heads(gdn_a_log[l]),
                     dt_bias=_expand_heads(gdn_dt_bias[l]),
                     norm_g=jnp.tile(gdn_norm_g[l].reshape(1, -1), (1, h_gdn)), e_beta=e_beta, e_a=e_a),
        )

    def mix_and_project(l, x2, bsz, t_len, a_sh, a_z, c_qkv, small, c_z, o_b, prev0, conv0, s_rwkv0, s_gdn0,
                        s0_layer, states):
        lp = layer_params(l)
        c = min(HEAD_DIM, t_len)
        st_rwkv, st_gdn = (None, None) if states is None else states
        gb = N_STACK // (h_rwkv * c)
        n_chains = max(1, min(N_CHAINS, bsz // gb))
        o_a, st_rwkv = _rwkv_mix(a_sh, a_z, prev0, s_rwkv0, s0_layer, st_rwkv, l, depth, lp["rwkv"], gb, n_chains,
                                 c, CHUNK_PASSES)
        gb = N_STACK // (h_gdn * c)
        o_c, st_gdn = _gdn_mix(c_qkv, small, c_z, conv0, s_gdn0, s0_layer, st_gdn, l, depth, lp["gdn"], gb,
                               n_chains, c, CHUNK_PASSES)
        f2 = lambda a: a.reshape(bsz * t_len, a.shape[-1])
        x_new = _out_project(f2(o_a), f2(o_b), f2(o_c), x2, w_out_b, l, ln_post_g[l], ln_post_b[l], alpha,
                             _row_tile(bsz * t_len, OUT_ROWS))
        return x_new, (a_sh[:, -1], c_qkv[:, t_len - (n_conv - 1):]), (st_rwkv, st_gdn)

    def prompt_layer(l, x2, stacked, states):
        ln_params = (ln_in_g, ln_in_b) if l == 0 else None
        res = _project_prompt(x2.reshape(bp, seq, d_model), w_in_p, l, widths, bf_pad[l:l + 1], e_aux, h_fox,
                              min(seq, PROJ_ROWS), ln_params, stacked)
        a_sh, a_z, b_z, c_qkv, c_z, small, k_st, v_st, logf_st, qa, ka, va = res[:12]
        if l == 0:
            x2 = res[12].reshape(bp * seq, d_model)
        o_b = _fox_prompt(qa, ka, va, b_z, min(seq, FOX_BLOCK))
        x_new, outs, states = mix_and_project(l, x2, bp, seq, a_sh, a_z, c_qkv, small, c_z, o_b, *zeros_p, 0, states)
        return x_new, outs, (k_st, v_st, logf_st), states

    def sample_layer(l, x2, states):
        tm = _row_tile(bs * dec_seq)
        r3 = lambda a: a.reshape(bs, dec_seq, a.shape[-1])
        a_sh, a_z, b_q, b_k, b_v, b_z, c_qkv, c_z, small = map(r3, _project(x2, w_in_p, l, widths, tm))
        logf, ccol, crow = _fox_gates(small, bf_pad[l:l + 1], h_fox, min(dec_seq, 256))
        o_b = _fox_cached(b_q, b_k, b_v, cache_k_t, cache_v_t, cache_logf_t, ccol, crow, b_z, l)
        x_new, outs, states = mix_and_project(
            l, x2, bs, dec_seq, a_sh, a_z, c_qkv, small, c_z, o_b, state_rwkv_shift[l][:, None, :],
            state_gdn_conv[l], state_rwkv_wkv.reshape(depth, bs, d_rwkv, HEAD_DIM),
            state_gdn_wkv.reshape(depth, bs, d_gdn, HEAD_DIM), l, states)
        heads = lambda a: a.reshape(bs, dec_seq, h_fox, HEAD_DIM)
        return x_new, (heads(b_k), heads(b_v), logf) + outs, states

    xp = x_prompt.reshape(bp * seq, d_model)
    xs = _layer_norm(x_sample.reshape(bs * dec_seq, d_model), ln_in_g, ln_in_b, _row_tile(bs * dec_seq))
    zeros_p = (jnp.zeros((bp, 1, n_shift), F32), jnp.zeros((bp, n_conv - 1, n_conv_cols), F32),
               jnp.zeros((1, bp, d_rwkv, HEAD_DIM), F32), jnp.zeros((1, bp, d_gdn, HEAD_DIM), F32))
    outs_p, outs_s, stacked, states_p, states_s = [], [], None, None, None
    for l in range(depth):
        xs, o, states_s = sample_layer(l, xs, states_s)
        outs_s.append(o)
    for l in range(depth):
        xp, o, stacked, states_p = prompt_layer(l, xp, stacked, states_p)
        outs_p.append(o)
    k_st, v_st, logf_st = stacked
    stack = lambda outs, i: jnp.stack([o[i] for o in outs])
    blocks = lambda st, bsz: st.reshape(depth, bsz, st.shape[2] // HEAD_DIM, HEAD_DIM, HEAD_DIM)
    return (xp.reshape(bp, seq, d_model), xs.reshape(bs, dec_seq, d_model),
            k_st.reshape(depth, bp, seq, h_fox, HEAD_DIM), v_st.reshape(depth, bp, seq, h_fox, HEAD_DIM), logf_st,
            stack(outs_p, 0), blocks(states_p[0], bp), stack(outs_p, 1), blocks(states_p[1], bp),
            stack(outs_s, 0), stack(outs_s, 1), stack(outs_s, 2),
            stack(outs_s, 3), blocks(states_s[0], bs), stack(outs_s, 4), blocks(states_s[1], bs))
```

```python
import functools
import math

import jax
import jax.numpy as jnp
import numpy as np
from jax import lax
from jax.experimental import pallas as pl
from jax.experimental.pallas import tpu as pltpu

F32 = jnp.float32
BF16 = jnp.bfloat16

HEAD_DIM = 64
LANES = 128
N_STACK = 256
PROJ_ROWS = 512
OUT_ROWS = 1024
FOX_BLOCK = 512
FOX_HEADS = 4
N_CHAINS = 8
CHUNK_PASSES = 1
LN_EPS = 1e-5
RWKV_GN_EPS = 64e-5
GDN_NORM_EPS = 1e-6
L2_EPS = 1e-6
NEG_BIG = -1e30
LOG2E = math.log2(math.e)
VMEM_LIMIT = 56 * 1024 * 1024


def _sigmoid(x):
    return 1.0 / (1.0 + jnp.exp(-x))


def _softplus(x):
    return jnp.maximum(x, 0.0) + jnp.log(1.0 + jnp.exp(-jnp.abs(x)))


def _silu(x):
    return x * _sigmoid(x)


def _split3(x):
    hi = x.astype(BF16)
    r1 = x - hi.astype(F32)
    mid = r1.astype(BF16)
    lo = (r1 - mid.astype(F32)).astype(BF16)
    return hi, mid, lo


_NN = (((1,), (0,)), ((), ()))
_NT = (((1,), (1,)), ((), ()))
_TN = (((0,), (0,)), ((), ()))


def _dg(a, b, dims):
    return lax.dot_general(a, b, dims, preferred_element_type=F32)


def _mm(a, b, dims=_NN, passes=1):
    if passes == 1:
        return _dg(a.astype(BF16), b.astype(BF16), dims)
    ah, am, _ = _split3(a)
    bh, bm, _ = _split3(b)
    return _dg(ah, bh, dims) + (_dg(ah, bm, dims) + _dg(am, bh, dims))


def _mm_exact_rhs(a, e, dims=_NN):
    m = a.shape[0]
    r = _dg(jnp.concatenate(_split3(a), axis=0), e, dims)
    return r[0:m] + (r[m:2 * m] + r[2 * m:3 * m])


def _mm_exact_lhs(e, b, dims=_NN):
    hi, mid, lo = _split3(b)
    return _dg(e, hi, dims) + (_dg(e, mid, dims) + _dg(e, lo, dims))


def _iota(shape, dim):
    return lax.broadcasted_iota(jnp.int32, shape, dim)


def _div_pow2(x, n):
    assert n & (n - 1) == 0
    return lax.shift_right_logical(x, jnp.int32(int(math.log2(n))))


def _head_block_ones(n):
    r = _div_pow2(_iota((n, n), 0), HEAD_DIM)
    c = _div_pow2(_iota((n, n), 1), HEAD_DIM)
    return jnp.where(r == c, 1.0, 0.0).astype(BF16)


def _block_diag(tall):
    n = tall.shape[0]
    wide = jnp.concatenate([tall] * (n // HEAD_DIM), axis=1)
    same = _div_pow2(_iota((n, n), 0), HEAD_DIM) == _div_pow2(_iota((n, n), 1), HEAD_DIM)
    return jnp.where(same, wide, 0.0)


def _diag_tall(s_bd):
    n = s_bd.shape[0]
    return jnp.concatenate([s_bd[h * HEAD_DIM:(h + 1) * HEAD_DIM, h * HEAD_DIM:(h + 1) * HEAD_DIM]
                            for h in range(n // HEAD_DIM)], axis=0)


def _stack_heads(x, gb, c, n_heads):
    width = x.shape[1]
    lane_head = _div_pow2(_iota((1, width), 1), HEAD_DIM)
    parts = []
    for b in range(gb):
        xb = x[b * c:(b + 1) * c, :]
        for h in range(n_heads):
            parts.append(jnp.where(lane_head == h, xb, 0.0))
    return jnp.concatenate(parts, axis=0)


def _ls_masks(gb, c, n_heads):
    shape = (gb * c, gb * n_heads * c)
    row = _iota(shape, 0)
    lane = _iota(shape, 1)
    same_b = _div_pow2(row, c) == _div_pow2(lane, n_heads * c)
    i = row & (c - 1)
    j = lane & (c - 1)
    return same_b & (j < i), same_b & (j <= i), same_b & (j == i)


def _expand_bd(m_ls, gb, c, n_heads):
    lane_h = _div_pow2(_iota((1, m_ls.shape[1]), 1), c) & (n_heads - 1)
    parts = []
    for b in range(gb):
        mb = m_ls[b * c:(b + 1) * c, :]
        for h in range(n_heads):
            parts.append(jnp.where(lane_h == h, mb, 0.0))
    return jnp.concatenate(parts, axis=0)


def _unit_lower_inverse(p, gb, c, n_heads, passes):
    n_rows = p.shape[0]
    expand = functools.partial(_expand_bd, gb=gb, c=c, n_heads=n_heads)
    t = jnp.where(_ls_masks(gb, c, n_heads)[2], 1.0, 0.0) + p
    n_rounds = int(math.log2(c)) - 1
    pk = _mm(p, expand(p), passes=passes)
    yield
    for i in range(n_rounds):
        if i + 1 < n_rounds:
            both = _mm(jnp.concatenate([t, pk], axis=0), expand(pk), passes=passes)
            t = t + both[0:n_rows]
            pk = both[n_rows:2 * n_rows]
        else:
            t = t + _mm(t, expand(pk), passes=passes)
        yield
    return t


def _run_interleaved(chains):
    results = [None] * len(chains)
    active = list(enumerate(chains))
    while active:
        still = []
        for i, ch in active:
            try:
                next(ch)
                still.append((i, ch))
            except StopIteration as stop:
                results[i] = stop.value
        active = still
    return results


def _chunk_masks(n, c):
    row = _iota((n, n), 0)
    col = _iota((n, n), 1)
    same = _div_pow2(row, c) == _div_pow2(col, c)
    strict = jnp.logical_and(same, row > col)
    incl = jnp.logical_and(same, row >= col)
    return strict, incl


def _cumsum_rows(x, gb, c):
    n = gb * c
    strict, incl = _chunk_masks(n, c)
    del strict
    tri = jnp.where(incl, 1.0, 0.0).astype(BF16)
    return _mm_exact_lhs(tri, x)


def _shift_rows(x, j, prev_rows, gb, c):
    n_prev = prev_rows[0].shape[0]
    out = pltpu.roll(x, j, 0)
    rid = _iota((x.shape[0], 1), 0)
    for b in range(gb):
        for t in range(j):
            src = n_prev + t - j
            out = jnp.where(rid == b * c + t, prev_rows[b][src:src + 1, :], out)
    return out


def _cparams(sem):
    return pltpu.CompilerParams(dimension_semantics=sem, vmem_limit_bytes=VMEM_LIMIT)


def _ln_kernel(x_ref, g_ref, b_ref, o_ref):
    x = x_ref[...]
    mu = jnp.mean(x, axis=-1, keepdims=True)
    xc = x - mu
    var = jnp.mean(xc * xc, axis=-1, keepdims=True)
    o_ref[...] = xc * lax.rsqrt(var + LN_EPS) * g_ref[...] + b_ref[...]


def _layer_norm(x2, g, b, tm):
    n, d = x2.shape
    return pl.pallas_call(
        _ln_kernel,
        grid=(n // tm,),
        in_specs=[pl.BlockSpec((tm, d), lambda i: (i, 0)),
                  pl.BlockSpec((1, d), lambda i: (0, 0)),
                  pl.BlockSpec((1, d), lambda i: (0, 0))],
        out_specs=pl.BlockSpec((tm, d), lambda i: (i, 0)),
        out_shape=jax.ShapeDtypeStruct((n, d), F32),
        compiler_params=_cparams(("parallel",)),
        name="ln_in",
    )(x2, g.reshape(1, d), b.reshape(1, d))


def _proj_kernel(x_ref, w_ref, *o_refs):
    xb = x_ref[...].astype(BF16)
    off = 0
    for o_ref in o_refs:
        wdt = o_ref.shape[-1]
        o_ref[...] = jnp.dot(xb, w_ref[:, off:off + wdt], preferred_element_type=F32)
        off += wdt


def _project(x2, w_all, layer, widths, tm):
    n, d = x2.shape
    n_cols = w_all.shape[-1]
    return pl.pallas_call(
        _proj_kernel,
        grid=(n // tm,),
        in_specs=[pl.BlockSpec((tm, d), lambda i: (i, 0)),
                  pl.BlockSpec((None, d, n_cols), lambda i: (layer, 0, 0))],
        out_specs=[pl.BlockSpec((tm, w), lambda i: (i, 0)) for w in widths],
        out_shape=[jax.ShapeDtypeStruct((n, w), F32) for w in widths],
        compiler_params=_cparams(("parallel",)),
        name="proj_in",
    )(x2, w_all)


def _rwkv_kernel(ash_ref, az_ref, prev0_ref, s0_ref, mu_ref, w0_ref, a0_ref, w2a2_ref, kk_ref, ka_ref,
                 rk_ref, gng_ref, gnb_ref, *rest, gb, n_chains, c, n_heads, passes):
    o_ref, st_ref, s_ref, prev_scr = rest[-4:]

    @pl.when(pl.program_id(1) == 0)
    def _():
        for b in range(s_ref.shape[0]):
            s_ref[b] = _block_diag(s0_ref[b])
        prev_scr[...] = prev0_ref[...]

    prm = dict(mu=mu_ref[...], w0=w0_ref[...], a0=a0_ref[...], w2a2=w2a2_ref[...], k_k=kk_ref[...],
               k_a=ka_ref[...], r_k=rk_ref[...], gn_g=gng_ref[...], gn_b=gnb_ref[...])
    ins = []
    for ch in range(n_chains):
        b0 = ch * gb
        ins.append((ash_ref[b0:b0 + gb].reshape(gb * c, ash_ref.shape[-1]),
                    az_ref[b0:b0 + gb].reshape(gb * c, az_ref.shape[-1]),
                    [prev_scr[b0 + b] for b in range(gb)],
                    [s_ref[b0 + b] for b in range(gb)]))
    outs = _run_interleaved([_rwkv_chain(*args, prm, gb=gb, c=c, n_heads=n_heads, passes=passes) for args in ins])
    for ch, (o, s_new, last_rows) in enumerate(outs):
        b0 = ch * gb
        o_ref[b0:b0 + gb] = o.reshape(gb, c, o.shape[-1])
        for b in range(gb):
            s_ref[b0 + b] = s_new[b]
            prev_scr[b0 + b] = last_rows[b]

    @pl.when(pl.program_id(1) == pl.num_programs(1) - 1)
    def _():
        for b in range(s_ref.shape[0]):
            st_ref[b] = _diag_tall(s_ref[b])


def _rwkv_chain(p, z, prev_rows, s_old, prm, *, gb, c, n_heads, passes):
    d = n_heads * HEAD_DIM
    prev = _shift_rows(p, 1, prev_rows, gb, c)
    last_rows = [p[(b + 1) * c - 1:(b + 1) * c, :] for b in range(gb)]
    xs = p + (prev - p) * prm["mu"]
    r = xs[:, 0:d]
    k = xs[:, d:2 * d]
    v = xs[:, 2 * d:3 * d]
    wa = xs[:, 3 * d:3 * d + LANES]
    lane = _iota((1, LANES), 1)
    wa = jnp.where(lane < HEAD_DIM, jnp.tanh(wa), wa)
    pre = _mm(wa, prm["w2a2"])
    yield
    w_ll = -_softplus(-(prm["w0"] + pre[:, 0:d])) - 0.5
    lw = -jnp.exp(w_ll)
    a = _sigmoid(prm["a0"] + pre[:, d:2 * d])

    ones_bd = _head_block_ones(d)
    kk = k * prm["k_k"]
    kk = kk / jnp.maximum(jnp.sqrt(_mm_exact_rhs(kk * kk, ones_bd)), 1e-12)
    kmod = k * (1.0 + (a - 1.0) * prm["k_a"])
    kka = kk * a

    cum = _cumsum_rows(lw, gb, c)
    yield
    cum_last = jnp.concatenate(
        [jnp.broadcast_to(cum[(b + 1) * c - 1:(b + 1) * c, :], (c, d)) for b in range(gb)], axis=0)
    inv_p = jnp.exp(-cum)
    to_end = jnp.exp(cum_last - cum)
    b_t = kk * jnp.exp(cum - lw)
    a_t = -kka * inv_p
    k_t = kmod * inv_p
    r_t = r * jnp.exp(cum)

    st = functools.partial(_stack_heads, gb=gb, c=c, n_heads=n_heads)
    n_rows = gb * c
    n = n_rows * n_heads
    g = _mm(jnp.concatenate([b_t, r_t], axis=0), jnp.concatenate([st(a_t), st(k_t)], axis=0), _NT, passes)
    yield
    strict, incl, _ = _ls_masks(gb, c, n_heads)
    m_ab = jnp.where(strict, g[0:n_rows, 0:n], 0.0)
    m_bk = jnp.where(strict, g[0:n_rows, n:2 * n], 0.0)
    m_ra = jnp.where(incl, g[n_rows:2 * n_rows, 0:n], 0.0)
    m_rk = jnp.where(incl, g[n_rows:2 * n_rows, n:2 * n], 0.0)

    rows = [slice(b * c, (b + 1) * c) for b in range(gb)]
    br = [_mm(jnp.concatenate([b_t[sl], r_t[sl]], axis=0), s_old[b], _NT, passes) for b, sl in enumerate(rows)]
    bh = jnp.concatenate([x[0:c] for x in br], axis=0)
    rh = jnp.concatenate([x[c:2 * c] for x in br], axis=0)
    mv = _mm(jnp.concatenate([m_bk, m_rk], axis=0), st(v), passes=passes)
    rhs_u = bh + mv[0:n_rows]
    o = rh + mv[n_rows:2 * n_rows]
    t_inv = yield from _unit_lower_inverse(m_ab, gb, c, n_heads, passes)
    u = _mm(t_inv, st(rhs_u), passes=passes)
    yield
    o = o + _mm(m_ra, st(u), passes=passes)

    a_end = -kka * to_end
    k_end = kmod * to_end
    same_head = _div_pow2(_iota((d, d), 0), HEAD_DIM) == _div_pow2(_iota((d, d), 1), HEAD_DIM)
    s_new = []
    for b, sl in enumerate(rows):
        p_end = jnp.exp(cum[(b + 1) * c - 1:(b + 1) * c, :])
        upd = _mm(jnp.concatenate([u[sl], v[sl]], axis=0), jnp.concatenate([a_end[sl], k_end[sl]], axis=0),
                  _TN, passes)
        s_new.append(s_old[b] * p_end + jnp.where(same_head, upd, 0.0))
    yield

    mean = _mm_exact_rhs(o, ones_bd) * (1.0 / HEAD_DIM)
    oc = o - mean
    yield
    var = _mm_exact_rhs(oc * oc, ones_bd) * (1.0 / HEAD_DIM)
    o = oc * lax.rsqrt(var + RWKV_GN_EPS) * prm["gn_g"] + prm["gn_b"]
    bonus = _mm_exact_rhs(r * kmod * prm["r_k"], ones_bd) * v
    return ((o + bonus) * _silu(z)).astype(BF16), s_new, last_rows


def _state_specs(s0, s0_layer, stacked, layer, depth, bsz, gb, d):
    in_spec = pl.BlockSpec((None, gb, d, HEAD_DIM), lambda b, s: (s0_layer, b, 0, 0))
    out_spec = pl.BlockSpec((None, gb, d, HEAD_DIM), lambda b, s: (layer, b, 0, 0))
    out_shape = jax.ShapeDtypeStruct((depth, bsz, d, HEAD_DIM), F32)
    extra_specs = [] if stacked is None else [pl.BlockSpec(memory_space=pl.ANY)]
    extra_args = [] if stacked is None else [stacked]
    return in_spec, out_spec, out_shape, extra_specs, extra_args


def _rwkv_mix(a_sh, a_z, prev0, s0, s0_layer, stacked, layer, depth, lp, gb, n_chains, c, passes):
    bsz, t_len, n_shift = a_sh.shape
    d = a_z.shape[-1]
    n_heads = d // HEAD_DIM
    kern = functools.partial(_rwkv_kernel, gb=gb, n_chains=n_chains, c=c, n_heads=n_heads, passes=passes)
    gb = gb * n_chains
    vec = lambda w: pl.BlockSpec((1, w), lambda b, s: (0, 0))
    s_in, s_out, s_shape, extra_specs, extra_args = _state_specs(s0, s0_layer, stacked, layer, depth, bsz, gb, d)
    args = [a_sh, a_z, prev0, s0, lp["mu"], lp["w0"], lp["a0"], lp["w2a2"], lp["k_k"], lp["k_a"], lp["r_k"],
            lp["gn_g"], lp["gn_b"]]
    return pl.pallas_call(
        kern,
        grid=(bsz // gb, t_len // c),
        in_specs=[pl.BlockSpec((gb, c, n_shift), lambda b, s: (b, s, 0)),
                  pl.BlockSpec((gb, c, d), lambda b, s: (b, s, 0)),
                  pl.BlockSpec((gb, 1, n_shift), lambda b, s: (b, 0, 0)),
                  s_in,
                  vec(n_shift), vec(d), vec(d),
                  pl.BlockSpec((LANES, 2 * d), lambda b, s: (0, 0)),
                  vec(d), vec(d), vec(d), vec(d), vec(d)] + extra_specs,
        out_specs=[pl.BlockSpec((gb, c, d), lambda b, s: (b, s, 0)), s_out],
        out_shape=[jax.ShapeDtypeStruct((bsz, t_len, d), BF16), s_shape],
        input_output_aliases={len(args): 1} if extra_args else {},
        scratch_shapes=[pltpu.VMEM((gb, d, d), F32), pltpu.VMEM((gb, 1, n_shift), F32)],
        compiler_params=_cparams(("parallel", "arbitrary")),
        name="rwkv_mix",
    )(*args, *extra_args)


def _gdn_kernel(qkv_ref, small_ref, cz_ref, conv0_ref, s0_ref, convw_ref, alog_ref, dtb_ref, ng_ref,
                eb_ref, ea_ref, *rest, gb, n_chains, c, n_heads, passes):
    o_ref, st_ref, s_ref, conv_scr = rest[-4:]

    @pl.when(pl.program_id(1) == 0)
    def _():
        for b in range(s_ref.shape[0]):
            s_ref[b] = _block_diag(s0_ref[b])
        conv_scr[...] = conv0_ref[...]

    prm = dict(conv_w=convw_ref[...], a_log=alog_ref[...], dt_bias=dtb_ref[...], norm_g=ng_ref[...],
               e_beta=eb_ref[...], e_a=ea_ref[...])
    ins = []
    for ch in range(n_chains):
        b0 = ch * gb
        ins.append((qkv_ref[b0:b0 + gb].reshape(gb * c, qkv_ref.shape[-1]),
                    small_ref[b0:b0 + gb].reshape(gb * c, LANES),
                    cz_ref[b0:b0 + gb].reshape(gb * c, cz_ref.shape[-1]),
                    [conv_scr[b0 + b] for b in range(gb)],
                    [s_ref[b0 + b] for b in range(gb)]))
    outs = _run_interleaved([_gdn_chain(*args, prm, gb=gb, c=c, n_heads=n_heads, passes=passes) for args in ins])
    for ch, (o, s_new, last_rows) in enumerate(outs):
        b0 = ch * gb
        o_ref[b0:b0 + gb] = o.reshape(gb, c, o.shape[-1])
        for b in range(gb):
            s_ref[b0 + b] = s_new[b]
            conv_scr[b0 + b] = last_rows[b]

    @pl.when(pl.program_id(1) == pl.num_programs(1) - 1)
    def _():
        for b in range(s_ref.shape[0]):
            st_ref[b] = _diag_tall(s_ref[b])


def _gdn_chain(x, small, z, prev_rows, s_old, prm, *, gb, c, n_heads, passes):
    d = n_heads * HEAD_DIM
    conv_w = prm["conv_w"]
    n_conv = conv_w.shape[0]
    y = x * conv_w[n_conv - 1:n_conv, :]
    for j in range(1, n_conv):
        y = y + _shift_rows(x, j, prev_rows, gb, c) * conv_w[n_conv - 1 - j:n_conv - j, :]
    last_rows = [x[(b + 1) * c - (n_conv - 1):(b + 1) * c, :] for b in range(gb)]
    y = _silu(y)
    q = y[:, 0:d]
    k = y[:, d:2 * d]
    v = y[:, 2 * d:3 * d]
    ones_bd = _head_block_ones(d)
    q = q * lax.rsqrt(_mm_exact_rhs(q * q, ones_bd) + L2_EPS) * (HEAD_DIM ** -0.5)
    k = k * lax.rsqrt(_mm_exact_rhs(k * k, ones_bd) + L2_EPS)

    beta = _sigmoid(_mm_exact_rhs(small, prm["e_beta"]))
    g = -jnp.exp(prm["a_log"]) * _softplus(_mm_exact_rhs(small, prm["e_a"]) + prm["dt_bias"])
    yield
    gc = _cumsum_rows(g, gb, c)
    yield
    gc_last = jnp.concatenate(
        [jnp.broadcast_to(gc[(b + 1) * c - 1:(b + 1) * c, :], (c, d)) for b in range(gb)], axis=0)
    egc = jnp.exp(gc)

    st = functools.partial(_stack_heads, gb=gb, c=c, n_heads=n_heads)
    n_rows = gb * c
    n = n_rows * n_heads
    kb = k * beta
    strict, incl, _ = _ls_masks(gb, c, n_heads)
    gcol = jnp.min(st(gc), axis=1, keepdims=True)
    eye = _iota((n, n), 0) == _iota((n, n), 1)
    grow = jnp.sum(jnp.where(eye, jnp.broadcast_to(gcol, (n, n)), 0.0), axis=0, keepdims=True)
    if gb == 1 and c == HEAD_DIM:
        gc_ls = gc
    else:
        lane_h = _div_pow2(_iota((d, n), 1), c) & (n_heads - 1)
        pick = jnp.where(_iota((d, n), 0) == lane_h * HEAD_DIM, 1.0, 0.0).astype(BF16)
        gc_ls = _mm_exact_rhs(gc, pick)
    dmat = jnp.exp(jnp.where(incl, gc_ls - grow, 0.0))
    g2 = _mm(jnp.concatenate([kb, q], axis=0), st(k), _NT, passes)
    yield
    m = jnp.where(strict, g2[0:n_rows] * dmat, 0.0)
    qk = jnp.where(incl, g2[n_rows:2 * n_rows] * dmat, 0.0)
    t_inv = yield from _unit_lower_inverse(-m, gb, c, n_heads, passes)
    rhs = jnp.concatenate([st(v * beta), st(kb * egc)], axis=1)
    sol = _mm(t_inv, rhs, passes=passes)
    yield
    u = sol[:, 0:d]
    w = sol[:, d:2 * d]

    rows = [slice(b * c, (b + 1) * c) for b in range(gb)]
    qg = q * egc
    kd = k * jnp.exp(gc_last - gc)
    v_new = u - jnp.concatenate([_mm(w[sl], s_old[b], passes=passes) for b, sl in enumerate(rows)], axis=0)
    o = jnp.concatenate([_mm(qg[sl], s_old[b], passes=passes) for b, sl in enumerate(rows)], axis=0)
    yield
    o = o + _mm(qk, st(v_new), passes=passes)
    same_head = _div_pow2(_iota((d, d), 0), HEAD_DIM) == _div_pow2(_iota((d, d), 1), HEAD_DIM)
    s_new = []
    for b, sl in enumerate(rows):
        gl = jnp.exp(gc[(b + 1) * c - 1:(b + 1) * c, :])
        s_new.append(s_old[b] * gl + jnp.where(same_head, _mm(kd[sl], v_new[sl], _TN, passes), 0.0))
    yield

    ms = _mm_exact_rhs(o * o, ones_bd) * (1.0 / HEAD_DIM)
    o = o * lax.rsqrt(ms + GDN_NORM_EPS) * prm["norm_g"]
    return (o * _silu(z)).astype(BF16), s_new, last_rows


def _gdn_mix(c_qkv, small, c_z, conv0, s0, s0_layer, stacked, layer, depth, lp, gb, n_chains, c, passes):
    bsz, t_len, n_qkv = c_qkv.shape
    d = c_z.shape[-1]
    n_heads = d // HEAD_DIM
    n_conv = lp["conv_w"].shape[0]
    kern = functools.partial(_gdn_kernel, gb=gb, n_chains=n_chains, c=c, n_heads=n_heads, passes=passes)
    gb = gb * n_chains
    vec = lambda w: pl.BlockSpec((1, w), lambda b, s: (0, 0))
    s_in, s_out, s_shape, extra_specs, extra_args = _state_specs(s0, s0_layer, stacked, layer, depth, bsz, gb, d)
    args = [c_qkv, small, c_z, conv0, s0, lp["conv_w"], lp["a_log"], lp["dt_bias"], lp["norm_g"],
            lp["e_beta"], lp["e_a"]]
    return pl.pallas_call(
        kern,
        grid=(bsz // gb, t_len // c),
        in_specs=[pl.BlockSpec((gb, c, n_qkv), lambda b, s: (b, s, 0)),
                  pl.BlockSpec((gb, c, LANES), lambda b, s: (b, s, 0)),
                  pl.BlockSpec((gb, c, d), lambda b, s: (b, s, 0)),
                  pl.BlockSpec((gb, n_conv - 1, n_qkv), lambda b, s: (b, 0, 0)),
                  s_in,
                  pl.BlockSpec((n_conv, n_qkv), lambda b, s: (0, 0)),
                  vec(d), vec(d), vec(d),
                  pl.BlockSpec((LANES, d), lambda b, s: (0, 0)),
                  pl.BlockSpec((LANES, d), lambda b, s: (0, 0))] + extra_specs,
        out_specs=[pl.BlockSpec((gb, c, d), lambda b, s: (b, s, 0)), s_out],
        out_shape=[jax.ShapeDtypeStruct((bsz, t_len, d), BF16), s_shape],
        input_output_aliases={len(args): 1} if extra_args else {},
        scratch_shapes=[pltpu.VMEM((gb, d, d), F32), pltpu.VMEM((gb, n_conv - 1, n_qkv), F32)],
        compiler_params=_cparams(("parallel", "arbitrary")),
        name="gdn_mix",
    )(*args, *extra_args)


def _log_forget_cumsum(small, bf, carry_scr):
    @pl.when(pl.program_id(1) == 0)
    def _():
        carry_scr[...] = jnp.zeros_like(carry_scr)

    logf = -_softplus(-(small + bf))
    tb = logf.shape[0]
    c = _cumsum_rows(logf, 1, tb) + carry_scr[...]
    carry_scr[...] = c[tb - 1:tb, :]
    return logf, c


def _fox_gate_kernel(small_ref, bf_ref, logf_ref, ccol_ref, crow_ref, carry_scr, *, n_heads):
    logf, c = _log_forget_cumsum(small_ref[...], bf_ref[...], carry_scr)
    logf_ref[...] = logf[:, 0:n_heads]
    ccol_ref[...] = c
    sel = jnp.where(_iota((n_heads, LANES), 0) == _iota((n_heads, LANES), 1), 1.0, 0.0).astype(BF16)
    crow_ref[...] = _mm_exact_lhs(sel, c, _NT)


def _fox_gates(small, bf_pad, n_heads, tb):
    bsz, t_len, _ = small.shape
    kern = functools.partial(_fox_gate_kernel, n_heads=n_heads)
    return pl.pallas_call(
        kern,
        grid=(bsz, t_len // tb),
        in_specs=[pl.BlockSpec((None, tb, LANES), lambda b, s: (b, s, 0)),
                  pl.BlockSpec((1, LANES), lambda b, s: (0, 0))],
        out_specs=[pl.BlockSpec((None, tb, n_heads), lambda b, s: (b, s, 0)),
                   pl.BlockSpec((None, tb, LANES), lambda b, s: (b, s, 0)),
                   pl.BlockSpec((None, n_heads, tb), lambda b, s: (b, 0, s))],
        out_shape=[jax.ShapeDtypeStruct((bsz, t_len, n_heads), F32),
                   jax.ShapeDtypeStruct((bsz, t_len, LANES), F32),
                   jax.ShapeDtypeStruct((bsz, n_heads, t_len), F32)],
        scratch_shapes=[pltpu.VMEM((1, LANES), F32)],
        compiler_params=_cparams(("parallel", "arbitrary")),
        name="fox_gates",
    )(small, bf_pad)


N_AUX = 3


def _proj_prompt_kernel(*refs, widths, n_heads, ln_in, n_alias):
    refs = list(refs)
    x_ref, w_ref, bf_ref, eaux_ref = refs[0:4]
    pos = 4
    if ln_in:
        g_ref, b_ref = refs[pos:pos + 2]
        pos += 2
    pos += n_alias
    (ash_ref, az_ref, bz_ref, cqkv_ref, cz_ref, small_ref, k_ref, v_ref, logf_ref,
     qa_ref, ka_ref, va_ref) = refs[pos:pos + 12]
    pos += 12
    if ln_in:
        xn_ref = refs[pos]
        pos += 1
    carry_scr = refs[pos]

    x = x_ref[...]
    if ln_in:
        mu = jnp.mean(x, axis=-1, keepdims=True)
        xc = x - mu
        var = jnp.mean(xc * xc, axis=-1, keepdims=True)
        x = xc * lax.rsqrt(var + LN_EPS) * g_ref[...] + b_ref[...]
        xn_ref[...] = x
    xb = x.astype(BF16)
    offs = np.concatenate([[0], np.cumsum(widths)])
    seg = lambda i: jnp.dot(xb, w_ref[:, int(offs[i]):int(offs[i + 1])], preferred_element_type=F32)
    ash_ref[...] = seg(0)
    az_ref[...] = seg(1)
    q, k, v = seg(2), seg(3), seg(4)
    bz_ref[...] = seg(5)
    cqkv_ref[...] = seg(6)
    cz_ref[...] = seg(7)
    small = seg(8)
    small_ref[...] = small
    k_ref[...] = k
    v_ref[...] = v

    logf, c = _log_forget_cumsum(small, bf_ref[...], carry_scr)
    logf_ref[...] = logf[:, 0:n_heads]
    lane = _iota((1, LANES), 1)
    hi, mid, lo = _split3(c * LOG2E)
    pieces = jnp.where(lane < n_heads, hi, jnp.where(lane < 2 * n_heads, mid, lo))
    aux_k_all = _dg(pieces, eaux_ref[...], _NN)
    for h in range(n_heads):
        pair, hh = divmod(h, 2)
        own = (lane < HEAD_DIM) if hh == 0 else (lane >= HEAD_DIM)
        a0 = HEAD_DIM if hh == 0 else 0
        aux_q = jnp.where((lane >= a0) & (lane < a0 + N_AUX), 1.0, 0.0)
        aux_v = jnp.where(lane == a0, 1.0, 0.0)
        cols = slice(pair * LANES, (pair + 1) * LANES)
        qa_ref[h] = jnp.where(own, q[:, cols] * (HEAD_DIM ** -0.5 * LOG2E), aux_q).astype(BF16)
        ka_ref[h] = jnp.where(own, k[:, cols], aux_k_all[:, cols]).astype(BF16)
        va_ref[h] = jnp.where(own, v[:, cols], aux_v).astype(BF16)


def _project_prompt(x3, w_all, layer, widths, bf_pad, e_aux, n_heads, tm, ln_params, stacked):
    bsz, t_len, d = x3.shape
    depth, _, n_cols = w_all.shape
    d_fox = n_heads * HEAD_DIM
    ln_in = ln_params is not None
    n_alias = 0 if stacked is None else len(stacked)
    kern = functools.partial(_proj_prompt_kernel, widths=tuple(widths), n_heads=n_heads, ln_in=ln_in,
                             n_alias=n_alias)
    row = lambda w: pl.BlockSpec((None, tm, w), lambda b, s: (b, s, 0))
    per_layer = lambda w: pl.BlockSpec((None, None, tm, w), lambda b, s: (layer, b, s, 0))
    per_head = pl.BlockSpec((None, n_heads, tm, LANES), lambda b, s: (b, 0, s, 0))
    const = lambda shape: pl.BlockSpec(shape, lambda b, s: (0,) * len(shape))
    in_specs = [row(d), pl.BlockSpec((None, d, n_cols), lambda b, s: (layer, 0, 0), pipeline_mode=pl.Buffered(1)),
                const((1, LANES)), const((LANES, n_heads * HEAD_DIM))]
    args = [x3, w_all, bf_pad, e_aux]
    if ln_in:
        in_specs += [const((1, d)), const((1, d))]
        args += [p.reshape(1, d) for p in ln_params]
    aliases = {}
    if stacked is not None:
        for j, buf in enumerate(stacked):
            aliases[len(args)] = 6 + j
            in_specs.append(pl.BlockSpec(memory_space=pl.ANY))
            args.append(buf)
    f32 = lambda *shape: jax.ShapeDtypeStruct(shape, F32)
    out_specs = [row(widths[0]), row(widths[1]), row(widths[5]), row(widths[6]), row(widths[7]), row(LANES),
                 per_layer(d_fox), per_layer(d_fox), per_layer(n_heads), per_head, per_head, per_head]
    out_shape = [f32(bsz, t_len, widths[0]), f32(bsz, t_len, widths[1]), f32(bsz, t_len, widths[5]),
                 f32(bsz, t_len, widths[6]), f32(bsz, t_len, widths[7]), f32(bsz, t_len, LANES),
                 f32(depth, bsz, t_len, d_fox), f32(depth, bsz, t_len, d_fox), f32(depth, bsz, t_len, n_heads)]
    out_shape += [jax.ShapeDtypeStruct((bsz, n_heads, t_len, LANES), BF16)] * 3
    if ln_in:
        out_specs.append(row(d))
        out_shape.append(f32(bsz, t_len, d))
    return pl.pallas_call(
        kern,
        grid=(bsz, t_len // tm),
        in_specs=in_specs,
        out_specs=out_specs,
        out_shape=out_shape,
        input_output_aliases=aliases,
        scratch_shapes=[pltpu.VMEM((1, LANES), F32)],
        compiler_params=_cparams(("parallel", "arbitrary")),
        name="proj_prompt",
    )(*args)


def _fox_prompt_kernel(qa_ref, ka_ref, va_ref, bz_ref, o_ref, s_scr, m_scr, acc_scr, *, blk, n_h):
    qi = pl.program_id(2)
    m_scr[...] = jnp.full_like(m_scr, NEG_BIG)
    acc_scr[...] = jnp.zeros_like(acc_scr)

    def logits(ki, slot):
        k0 = pl.multiple_of(ki * blk, blk)
        for h in range(n_h):
            s_scr[slot, h] = _dg(qa_ref[h], ka_ref[h, pl.ds(k0, blk), :], _NT)

    def consume(ki, slot, masked):
        k0 = pl.multiple_of(ki * blk, blk)
        for h in range(n_h):
            s = s_scr[slot, h]
            if masked:
                s = jnp.where(_iota((blk, blk), 1) <= _iota((blk, blk), 0), s, NEG_BIG)
            m_prev = m_scr[h]
            m_new = jnp.maximum(m_prev, jnp.max(s, axis=1, keepdims=True))
            alpha = jnp.exp2(m_prev - m_new)
            p = jnp.exp2(s - jnp.tile(m_new, (1, blk // LANES)))
            acc_scr[h] = alpha * acc_scr[h] + _dg(p.astype(BF16), va_ref[h, pl.ds(k0, blk), :], _NN)
            m_scr[h] = m_new

    def two_blocks(j, carry):
        logits(2 * j + 1, 0)
        consume(2 * j, 1, False)
        logits(jnp.minimum(2 * j + 2, qi - 1), 1)
        consume(2 * j + 1, 0, False)
        return carry

    logits(qi, 0)

    @pl.when(qi == 0)
    def _():
        consume(qi, 0, True)

    @pl.when(qi > 0)
    def _():
        logits(0, 1)
        consume(qi, 0, True)
        lax.fori_loop(0, qi // 2, two_blocks, 0)

        @pl.when(qi % 2 == 1)
        def _():
            consume(qi - 1, 1, False)

    lane = _iota((1, LANES), 1)
    outs = []
    for pair in range(n_h // 2):
        acc0 = acc_scr[2 * pair]
        acc1 = acc_scr[2 * pair + 1]
        outs.append(jnp.where(lane < HEAD_DIM, acc0 / acc0[:, HEAD_DIM:HEAD_DIM + 1], acc1 / acc1[:, 0:1]))
    o = jnp.concatenate(outs, axis=1) if len(outs) > 1 else outs[0]
    o_ref[...] = (o * _silu(bz_ref[...])).astype(BF16)


def _fox_prompt(qa, ka, va, b_z, blk):
    bsz, n_heads, t_len, _ = qa.shape
    d = b_z.shape[-1]
    n_h = min(n_heads, FOX_HEADS)
    w = (n_h // 2) * LANES
    return pl.pallas_call(
        functools.partial(_fox_prompt_kernel, blk=blk, n_h=n_h),
        grid=(bsz, n_heads // n_h, t_len // blk),
        in_specs=[pl.BlockSpec((None, n_h, blk, LANES), lambda b, p, i: (b, p, i, 0)),
                  pl.BlockSpec((None, n_h, t_len, LANES), lambda b, p, i: (b, p, 0, 0)),
                  pl.BlockSpec((None, n_h, t_len, LANES), lambda b, p, i: (b, p, 0, 0)),
                  pl.BlockSpec((None, blk, w), lambda b, p, i: (b, i, p))],
        out_specs=pl.BlockSpec((None, blk, w), lambda b, p, i: (b, i, p)),
        out_shape=jax.ShapeDtypeStruct((bsz, t_len, d), BF16),
        scratch_shapes=[pltpu.VMEM((2, n_h, blk, blk), F32),
                        pltpu.VMEM((n_h, blk, LANES), F32),
                        pltpu.VMEM((n_h, blk, LANES), F32)],
        compiler_params=_cparams(("parallel", "parallel", "arbitrary")),
        name="fox_prompt",
    )(qa, ka, va, b_z)


def _fox_cached_kernel(q_ref, k_ref, v_ref, ck_ref, cv_ref, clf_ref, ccol_ref, crow_ref, bz_ref, o_ref,
                       *, n_heads, lane_blk):
    t_len, d = q_ref.shape
    past = ck_ref.shape[1]
    n_rows = n_heads * t_len

    triu = jnp.where(_iota((lane_blk, lane_blk), 0) <= _iota((lane_blk, lane_blk), 1), 1.0, 0.0).astype(BF16)
    carry = jnp.zeros((n_heads, 1), F32)
    pieces = []
    for j in range(past // lane_blk):
        cj = _mm_exact_rhs(clf_ref[:, j * lane_blk:(j + 1) * lane_blk], triu) + carry
        carry = cj[:, lane_blk - 1:lane_blk]
        pieces.append(cj)
    c_cache = jnp.concatenate(pieces, axis=1)
    total = carry

    rows = lambda f: jnp.concatenate([f(h) for h in range(n_heads)], axis=0)
    ck_cache = rows(lambda h: jnp.broadcast_to(c_cache[h:h + 1, :], (t_len, past)))
    ck_new = rows(lambda h: jnp.broadcast_to(total[h:h + 1, :] + crow_ref[h:h + 1, :], (t_len, t_len)))
    cq = rows(lambda h: total[h:h + 1, :] + ccol_ref[:, h:h + 1])

    lane_head = _div_pow2(_iota((1, d), 1), HEAD_DIM)
    q = q_ref[...] * (HEAD_DIM ** -0.5)
    q_bd = rows(lambda h: jnp.where(lane_head == h, q, 0.0)).astype(BF16)
    s_c = _dg(q_bd, ck_ref[...].astype(BF16), _NN) + cq - ck_cache
    s_n = _dg(q_bd, k_ref[...].astype(BF16), _NT) + cq - ck_new
    q_pos = _iota((n_rows, t_len), 0) & (t_len - 1)
    s_n = jnp.where(_iota((n_rows, t_len), 1) <= q_pos, s_n, NEG_BIG)
    m = jnp.maximum(jnp.max(s_c, axis=1, keepdims=True), jnp.max(s_n, axis=1, keepdims=True))
    p_c = jnp.exp(s_c - m)
    p_n = jnp.exp(s_n - m)
    l = jnp.sum(p_c, axis=1, keepdims=True) + jnp.sum(p_n, axis=1, keepdims=True)
    o_all = (_dg(p_c.astype(BF16), cv_ref[...].astype(BF16), _NT)
             + _dg(p_n.astype(BF16), v_ref[...].astype(BF16), _NN)) / l
    o = jnp.where(lane_head == 0, o_all[0:t_len], 0.0)
    for h in range(1, n_heads):
        o = o + jnp.where(lane_head == h, o_all[h * t_len:(h + 1) * t_len], 0.0)
    o_ref[...] = (o * _silu(bz_ref[...])).astype(BF16)


def _fox_cached(q, k, v, cache_k, cache_v, cache_logf_t, ccol, crow, b_z, layer):
    bsz, t_len, d = q.shape
    n_heads = d // HEAD_DIM
    past = cache_k.shape[3]
    kern = functools.partial(_fox_cached_kernel, n_heads=n_heads, lane_blk=min(past, 256))
    cur = lambda w: pl.BlockSpec((None, t_len, w), lambda b: (b, 0, 0))
    return pl.pallas_call(
        kern,
        grid=(bsz,),
        in_specs=[cur(d), cur(d), cur(d),
                  pl.BlockSpec((None, None, d, past), lambda b: (layer, b, 0, 0)),
                  pl.BlockSpec((None, None, d, past), lambda b: (layer, b, 0, 0)),
                  pl.BlockSpec((None, None, n_heads, past), lambda b: (layer, b, 0, 0)),
                  cur(LANES),
                  pl.BlockSpec((None, n_heads, t_len), lambda b: (b, 0, 0)),
                  cur(d)],
        out_specs=cur(d),
        out_shape=jax.ShapeDtypeStruct((bsz, t_len, d), BF16),
        compiler_params=_cparams(("parallel",)),
        name="fox_cached",
    )(q, k, v, cache_k, cache_v, cache_logf_t, ccol, crow, b_z)


def _out_kernel(oa_ref, ob_ref, oc_ref, x_ref, w_ref, g_ref, b_ref, y_ref, *, alpha):
    da = oa_ref.shape[-1]
    db = ob_ref.shape[-1]
    h = jnp.dot(oa_ref[...], w_ref[0:da, :], preferred_element_type=F32)
    h = h + jnp.dot(ob_ref[...], w_ref[da:da + db, :], preferred_element_type=F32)
    h = h + jnp.dot(oc_ref[...], w_ref[da + db:, :], preferred_element_type=F32)
    y = alpha * x_ref[...] + h
    mu = jnp.mean(y, axis=-1, keepdims=True)
    yc = y - mu
    var = jnp.mean(yc * yc, axis=-1, keepdims=True)
    y_ref[...] = yc * lax.rsqrt(var + LN_EPS) * g_ref[...] + b_ref[...]


def _out_project(o_a, o_b, o_c, x2, w_all, layer, g, b, alpha, tm):
    n, d = x2.shape
    d_mix = w_all.shape[1]
    row = lambda w: pl.BlockSpec((tm, w), lambda i: (i, 0))
    return pl.pallas_call(
        functools.partial(_out_kernel, alpha=alpha),
        grid=(n // tm,),
        in_specs=[row(o_a.shape[-1]), row(o_b.shape[-1]), row(o_c.shape[-1]), row(d),
                  pl.BlockSpec((None, d_mix, d), lambda i: (layer, 0, 0)),
                  pl.BlockSpec((1, d), lambda i: (0, 0)),
                  pl.BlockSpec((1, d), lambda i: (0, 0))],
        out_specs=row(d),
        out_shape=jax.ShapeDtypeStruct((n, d), F32),
        compiler_params=_cparams(("parallel",)),
        name="proj_out",
    )(o_a, o_b, o_c, x2, w_all, g.reshape(1, d), b.reshape(1, d))


def _expand_heads(p):
    return jnp.repeat(p.astype(F32), HEAD_DIM).reshape(1, -1)


def _select_matrix(rows, n_cols, first_row):
    e = np.zeros((LANES, n_cols), np.float32)
    for h in range(rows):
        e[first_row + h, h * HEAD_DIM:(h + 1) * HEAD_DIM] = 1.0
    return jnp.asarray(e, BF16)


def _row_tile(n, largest=256):
    for tm in (1024, 512, 256, 128, 64, 32, 16, 8):
        if tm <= largest and n % tm == 0:
            return tm
    raise ValueError(f"row count {n} must be a multiple of 8")


def kernel(x_prompt, x_sample, cache_fox_k, cache_fox_v, cache_fox_logf, state_rwkv_shift, state_rwkv_wkv,
           state_gdn_conv, state_gdn_wkv, ln_in_g, ln_in_b, w_in, rwkv_mu, rwkv_w0, rwkv_w2, rwkv_a0, rwkv_a2,
           rwkv_k_k, rwkv_k_a, rwkv_r_k, rwkv_gn_g, rwkv_gn_b, fox_b_f, gdn_conv_w, gdn_a_log, gdn_dt_bias,
           gdn_norm_g, w_out, ln_post_g, ln_post_b):
    depth, d_model, _ = w_in.shape
    bp, seq, _ = x_prompt.shape
    bs, dec_seq, _ = x_sample.shape
    h_fox = fox_b_f.shape[1]
    h_gdn = gdn_a_log.shape[1]
    h_rwkv = rwkv_r_k.shape[1]
    d_rwkv, d_fox, d_gdn = h_rwkv * HEAD_DIM, h_fox * HEAD_DIM, h_gdn * HEAD_DIM
    rank_w = rwkv_w2.shape[1]
    rank_a = rwkv_a2.shape[1]
    n_shift = 3 * d_rwkv + rank_w + rank_a
    n_conv_cols = 3 * d_gdn
    n_conv = gdn_conv_w.shape[1]
    alpha = (2 * depth) ** 0.25
    assert rank_w + rank_a == LANES and N_AUX * h_fox + 2 * h_gdn <= LANES

    split = [n_shift, d_rwkv, d_fox, d_fox, d_fox, h_fox, d_fox, n_conv_cols, h_gdn, h_gdn, d_gdn]
    offs = np.concatenate([[0], np.cumsum(split)])
    seg = lambda i: w_in[:, :, offs[i]:offs[i + 1]]
    small_w = jnp.concatenate([seg(5)] * N_AUX + [seg(8), seg(9)], axis=-1)
    small_w = jnp.pad(small_w, ((0, 0), (0, 0), (0, LANES - small_w.shape[-1])))
    w_in_p = jnp.concatenate([seg(0), seg(1), seg(2), seg(3), seg(4), seg(6), seg(7), seg(10), small_w],
                             axis=-1).astype(BF16)
    widths = [n_shift, d_rwkv, d_fox, d_fox, d_fox, d_fox, n_conv_cols, d_gdn, LANES]
    w_out_b = w_out.astype(BF16)

    zw = jnp.zeros((depth, rank_w, d_rwkv), F32)
    w2a2 = jnp.concatenate([jnp.concatenate([rwkv_w2, zw], axis=-1),
                            jnp.concatenate([zw, rwkv_a2], axis=-1)], axis=1).astype(BF16)
    bf_pad = jnp.pad(jnp.tile(fox_b_f, (1, N_AUX)), ((0, 0), (0, LANES - N_AUX * h_fox)))
    e_beta = _select_matrix(h_gdn, d_gdn, N_AUX * h_fox)
    e_a = _select_matrix(h_gdn, d_gdn, N_AUX * h_fox + h_gdn)
    e_aux_np = np.zeros((LANES, d_fox), np.float32)
    for h in range(h_fox):
        for j in range(N_AUX):
            e_aux_np[j * h_fox + h, (h // 2) * LANES + (HEAD_DIM if h % 2 == 0 else 0) + j] = -1.0
    e_aux = jnp.asarray(e_aux_np, BF16)
    cache_logf_t = jnp.swapaxes(cache_fox_logf, 2, 3)
    cache_k_t = jnp.swapaxes(cache_fox_k.reshape(depth, bs, -1, d_fox), 2, 3)
    cache_v_t = jnp.swapaxes(cache_fox_v.reshape(depth, bs, -1, d_fox), 2, 3)

    def layer_params(l):
        row = lambda a: a[l].reshape(1, -1).astype(F32)
        return dict(
            rwkv=dict(mu=row(rwkv_mu), w0=row(rwkv_w0), a0=row(rwkv_a0), w2a2=w2a2[l], k_k=row(rwkv_k_k),
                      k_a=row(rwkv_k_a), r_k=row(rwkv_r_k), gn_g=row(rwkv_gn_g), gn_b=row(rwkv_gn_b)),
            gdn=dict(conv_w=gdn_conv_w[l], a_log=_expand_heads(gdn_a_log[l]),
                     dt_bias=_expand_heads(gdn_dt_bias[l]),
                     norm_g=jnp.tile(gdn_norm_g[l].reshape(1, -1), (1, h_gdn)), e_beta=e_beta, e_a=e_a),
        )

    def mix_and_project(l, x2, bsz, t_len, a_sh, a_z, c_qkv, small, c_z, o_b, prev0, conv0, s_rwkv0, s_gdn0,
                        s0_layer, states):
        lp = layer_params(l)
        c = min(HEAD_DIM, t_len)
        st_rwkv, st_gdn = (None, None) if states is None else states
        gb = N_STACK // (h_rwkv * c)
        n_chains = max(1, min(N_CHAINS, bsz // gb))
        o_a, st_rwkv = _rwkv_mix(a_sh, a_z, prev0, s_rwkv0, s0_layer, st_rwkv, l, depth, lp["rwkv"], gb, n_chains,
                                 c, CHUNK_PASSES)
        gb = N_STACK // (h_gdn * c)
        o_c, st_gdn = _gdn_mix(c_qkv, small, c_z, conv0, s_gdn0, s0_layer, st_gdn, l, depth, lp["gdn"], gb,
                               n_chains, c, CHUNK_PASSES)
        f2 = lambda a: a.reshape(bsz * t_len, a.shape[-1])
        x_new = _out_project(f2(o_a), f2(o_b), f2(o_c), x2, w_out_b, l, ln_post_g[l], ln_post_b[l], alpha,
                             _row_tile(bsz * t_len, OUT_ROWS))
        return x_new, (a_sh[:, -1], c_qkv[:, t_len - (n_conv - 1):]), (st_rwkv, st_gdn)

    def prompt_layer(l, x2, stacked, states):
        ln_params = (ln_in_g, ln_in_b) if l == 0 else None
        res = _project_prompt(x2.reshape(bp, seq, d_model), w_in_p, l, widths, bf_pad[l:l + 1], e_aux, h_fox,
                              min(seq, PROJ_ROWS), ln_params, stacked)
        a_sh, a_z, b_z, c_qkv, c_z, small, k_st, v_st, logf_st, qa, ka, va = res[:12]
        if l == 0:
            x2 = res[12].reshape(bp * seq, d_model)
        o_b = _fox_prompt(qa, ka, va, b_z, min(seq, FOX_BLOCK))
        x_new, outs, states = mix_and_project(l, x2, bp, seq, a_sh, a_z, c_qkv, small, c_z, o_b, *zeros_p, 0, states)
        return x_new, outs, (k_st, v_st, logf_st), states

    def sample_layer(l, x2, states):
        tm = _row_tile(bs * dec_seq)
        r3 = lambda a: a.reshape(bs, dec_seq, a.shape[-1])
        a_sh, a_z, b_q, b_k, b_v, b_z, c_qkv, c_z, small = map(r3, _project(x2, w_in_p, l, widths, tm))
        logf, ccol, crow = _fox_gates(small, bf_pad[l:l + 1], h_fox, min(dec_seq, 256))
        o_b = _fox_cached(b_q, b_k, b_v, cache_k_t, cache_v_t, cache_logf_t, ccol, crow, b_z, l)
        x_new, outs, states = mix_and_project(
            l, x2, bs, dec_seq, a_sh, a_z, c_qkv, small, c_z, o_b, state_rwkv_shift[l][:, None, :],
            state_gdn_conv[l], state_rwkv_wkv.reshape(depth, bs, d_rwkv, HEAD_DIM),
            state_gdn_wkv.reshape(depth, bs, d_gdn, HEAD_DIM), l, states)
        heads = lambda a: a.reshape(bs, dec_seq, h_fox, HEAD_DIM)
        return x_new, (heads(b_k), heads(b_v), logf) + outs, states

    xp = x_prompt.reshape(bp * seq, d_model)
    xs = _layer_norm(x_sample.reshape(bs * dec_seq, d_model), ln_in_g, ln_in_b, _row_tile(bs * dec_seq))
    zeros_p = (jnp.zeros((bp, 1, n_shift), F32), jnp.zeros((bp, n_conv - 1, n_conv_cols), F32),
               jnp.zeros((1, bp, d_rwkv, HEAD_DIM), F32), jnp.zeros((1, bp, d_gdn, HEAD_DIM), F32))
    outs_p, outs_s, stacked, states_p, states_s = [], [], None, None, None
    for l in range(depth):
        xs, o, states_s = sample_layer(l, xs, states_s)
        outs_s.append(o)
    xp, xs = lax.optimization_barrier((xp, xs))
    for l in range(depth):
        xp, o, stacked, states_p = prompt_layer(l, xp, stacked, states_p)
        outs_p.append(o)
    k_st, v_st, logf_st = stacked
    stack = lambda outs, i: jnp.stack([o[i] for o in outs])
    blocks = lambda st, bsz: st.reshape(depth, bsz, st.shape[2] // HEAD_DIM, HEAD_DIM, HEAD_DIM)
    return (xp.reshape(bp, seq, d_model), xs.reshape(bs, dec_seq, d_model),
            k_st.reshape(depth, bp, seq, h_fox, HEAD_DIM), v_st.reshape(depth, bp, seq, h_fox, HEAD_DIM), logf_st,
            stack(outs_p, 0), blocks(states_p[0], bp), stack(outs_p, 1), blocks(states_p[1], bp),
            stack(outs_s, 0), stack(outs_s, 1), stack(outs_s, 2),
            stack(outs_s, 3), blocks(states_s[0], bs), stack(outs_s, 4), blocks(states_s[1], bs))
```
